```python
import math
import jax, jax.numpy as jnp
from jax import lax
import numpy as np

D_MODEL = 1024
BATCH = 8
SEQ = 2048
DEPTH = 1
DEC_BATCH = 128
DEC_SEQ = 4
PAST_LEN = 2048
PAGE_SIZE = 128

HEAD_DIM = 64
MIX_WIDTH = D_MODEL
SB_WIDTH = MIX_WIDTH // 2
NSA_WIDTH = MIX_WIDTH - SB_WIDTH
H_SB = SB_WIDTH // HEAD_DIM
H_NSA = NSA_WIDTH // HEAD_DIM
KV_GROUPS = 2
HPG = H_NSA // KV_GROUPS
COMP_BLOCK = 32
COMP_STRIDE = 16
COMP_HID = 128
SEL_BLOCK = 64
TOP_N = 8
WINDOW = 512
Q_BLOCK = 128
PEER_HEADS = 8
N_KEYS = 128
N_EXPERTS = N_KEYS * N_KEYS
PEER_TOPK = 16
D_KEY = 256
TOK_CHUNK = 256
ALPHA = (2.0 * DEPTH) ** 0.25
BETA = (8.0 * DEPTH) ** -0.25
LN_EPS = 1e-5
NEG = -1e30
FORCE = 1e4
SCALE = HEAD_DIM ** -0.5

OFF_SBKV = SB_WIDTH
OFF_QB = 3 * SB_WIDTH
OFF_NSAKV = OFF_QB + NSA_WIDTH
OFF_WIN = OFF_NSAKV + 4 * KV_GROUPS * HEAD_DIM
OFF_GATE = OFF_WIN + 2 * KV_GROUPS * HEAD_DIM
IN_COLS = OFF_GATE + 3 * H_NSA

kernel_name = 'hybrid_stickbreak_nsa_peer_step'


def layer_norm(x, g, b):
    xf = x.astype(jnp.float32)
    mu = xf.mean(-1, keepdims=True)
    var = jnp.square(xf - mu).mean(-1, keepdims=True)
    return ((xf - mu) * lax.rsqrt(var + LN_EPS) * g + b).astype(x.dtype)


def rms_norm(x, g):
    xf = x.astype(jnp.float32)
    return (xf * lax.rsqrt(jnp.square(xf).mean(-1, keepdims=True) + LN_EPS) * g).astype(x.dtype)


def masked_softmax(s, mask):
    s = jnp.where(mask, s, NEG)
    e = jnp.exp(s - s.max(-1, keepdims=True)) * mask
    tot = e.sum(-1, keepdims=True)
    return e / jnp.where(tot > 0, tot, 1.0)


def alibi_slopes():
    i = jnp.arange(1, H_NSA + 1, dtype=jnp.float32)
    return (2.0 ** (-8.0 * i / H_NSA)).reshape(KV_GROUPS, HPG)


def project(x, w_in):
    B, T, _ = x.shape
    z = x @ w_in
    q_a = z[..., :SB_WIDTH].reshape(B, T, H_SB, HEAD_DIM)
    sb_kv = z[..., OFF_SBKV:OFF_QB].reshape(B, T, 2, H_SB, HEAD_DIM)
    q_b = z[..., OFF_QB:OFF_NSAKV].reshape(B, T, H_NSA, HEAD_DIM)
    nsa_kv = z[..., OFF_NSAKV:OFF_WIN].reshape(B, T, 4, KV_GROUPS, HEAD_DIM)
    win_kv = z[..., OFF_WIN:OFF_GATE].reshape(B, T, 2, KV_GROUPS, HEAD_DIM)
    gates = jax.nn.sigmoid(z[..., OFF_GATE:]).reshape(B, T, H_NSA, 3)
    return q_a, sb_kv, q_b, nsa_kv, win_kv, gates


def stick_breaking(q, q_pos, k, v, k_pos):
    z = jnp.einsum('bthd,bshd->bhts', q, k).astype(jnp.float32) * SCALE
    mask = k_pos[None, :] < q_pos[:, None]
    log_keep = jnp.where(mask, jax.nn.log_sigmoid(-z), 0.0)
    cum = jnp.cumsum(log_keep, axis=-1)
    after = cum[..., -1:] - cum
    w = jnp.where(mask, jnp.exp(jax.nn.log_sigmoid(z) + after), 0.0)
    return jnp.einsum('bhts,bshd->bthd', w.astype(v.dtype), v)


def compress(kc, vc, cmp_pe, cmp_w1, cmp_w2):
    L = kc.shape[1]
    n_c = (L - COMP_BLOCK) // COMP_STRIDE + 1
    starts = jnp.arange(n_c) * COMP_STRIDE
    idx = starts[:, None] + jnp.arange(COMP_BLOCK)[None, :]

    def phi(t, j):
        blk = t[:, idx] + cmp_pe[j][None, None, :, None, :]
        hdn = jax.nn.gelu(jnp.einsum('bnlgd,ldh->bngh', blk, cmp_w1[j]))
        return jnp.einsum('bngh,hd->bngd', hdn, cmp_w2[j])

    ck = phi(kc, 0)
    cv = phi(vc, 1)
    c_pos = starts + COMP_BLOCK - 1
    n_sel = -(-L // SEL_BLOCK)
    sel_map = jax.nn.one_hot(starts // SEL_BLOCK, n_sel, dtype=jnp.float32)
    return ck, cv, c_pos, sel_map


def sel_prep(k, n_sel):
    pad = n_sel * SEL_BLOCK - k.shape[1]
    return jnp.pad(k, ((0, 0), (0, pad), (0, 0), (0, 0))).transpose(0, 2, 1, 3)


def gather_rows(a, idx):
    return jax.vmap(jax.vmap(lambda r, i: r[i]))(a, idx)


def nsa_attend(q, q_pos, gates, ck, cv, c_pos, sel_map, sk, sv, wk, wv, w_pos):
    B, Tq = q.shape[:2]
    qg = q.reshape(B, Tq, KV_GROUPS, HPG, HEAD_DIM)
    slopes = alibi_slopes()[None, :, :, None, None]
    qp = q_pos.astype(jnp.float32)
    s = jnp.einsum('btgrd,bngd->bgrtn', qg, ck).astype(jnp.float32) * SCALE
    s = s - slopes * (qp[:, None] - c_pos[None, :].astype(jnp.float32))
    p_c = masked_softmax(s, c_pos[None, :] <= q_pos[:, None])
    o_c = jnp.einsum('bgrtn,bngd->btgrd', p_c.astype(cv.dtype), cv)
    imp = jnp.einsum('bgrtn,nj->bgtj', p_c, sel_map)
    n_sel = sel_map.shape[1]
    blk = jnp.arange(n_sel)
    valid = blk[None, :] * SEL_BLOCK <= q_pos[:, None]
    forced = (blk[None, :] == (q_pos // SEL_BLOCK)[:, None]) | (blk[None, :] == 0)
    imp = jnp.where(forced, FORCE, jnp.where(valid, imp, -1.0))
    n_top = min(TOP_N, n_sel)
    _, idx = lax.top_k(imp, n_top)
    kpos = (idx[..., None] * SEL_BLOCK + jnp.arange(SEL_BLOCK)).reshape(B, KV_GROUPS, Tq, n_top * SEL_BLOCK)
    flat = kpos.reshape(B, KV_GROUPS, -1)
    ksel = gather_rows(sk, flat).reshape(B, KV_GROUPS, Tq, n_top * SEL_BLOCK, HEAD_DIM)
    vsel = gather_rows(sv, flat).reshape(B, KV_GROUPS, Tq, n_top * SEL_BLOCK, HEAD_DIM)
    dist_s = (q_pos[None, None, :, None] - kpos)[:, :, None]
    s = jnp.einsum('btgrd,bgtsd->bgrts', qg, ksel).astype(jnp.float32) * SCALE
    s = s - slopes * dist_s.astype(jnp.float32)
    p_s = masked_softmax(s, dist_s >= 0)
    o_s = jnp.einsum('bgrts,bgtsd->btgrd', p_s.astype(vsel.dtype), vsel)
    dist_w = q_pos[:, None] - w_pos[None, :]
    mask_w = (dist_w >= 0) & (dist_w < WINDOW) & (w_pos[None, :] >= 0)
    s = jnp.einsum('btgrd,bsgd->bgrts', qg, wk).astype(jnp.float32) * SCALE
    s = s - slopes * dist_w.astype(jnp.float32)
    p_w = masked_softmax(s, mask_w)
    o_w = jnp.einsum('bgrts,bsgd->btgrd', p_w.astype(wv.dtype), wv)
    g = gates.reshape(B, Tq, KV_GROUPS, HPG, 3)
    o = g[..., 0:1] * o_c + g[..., 1:2] * o_s + g[..., 2:3] * o_w
    return o.reshape(B, Tq, NSA_WIDTH)


def mixer_prompt(x, w_in, cmp_pe, cmp_w1, cmp_w2):
    B, T, _ = x.shape
    q_a, sb_kv, q_b, nsa_kv, win_kv, gates = project(x, w_in)
    k_a, v_a = sb_kv[:, :, 0], sb_kv[:, :, 1]
    ck, cv, c_pos, sel_map = compress(nsa_kv[:, :, 0], nsa_kv[:, :, 1], cmp_pe, cmp_w1, cmp_w2)
    n_sel = sel_map.shape[1]
    sk = sel_prep(nsa_kv[:, :, 2], n_sel)
    sv = sel_prep(nsa_kv[:, :, 3], n_sel)
    win_pad = jnp.pad(win_kv, ((0, 0), (WINDOW, 0), (0, 0), (0, 0), (0, 0)))
    k_pos = jnp.arange(T)

    def block(qb):
        q0 = qb * Q_BLOCK
        q_pos = q0 + jnp.arange(Q_BLOCK)
        sl = lambda a: lax.dynamic_slice_in_dim(a, q0, Q_BLOCK, axis=1)
        o_a = stick_breaking(sl(q_a), q_pos, k_a, v_a, k_pos).reshape(B, Q_BLOCK, SB_WIDTH)
        w = lax.dynamic_slice_in_dim(win_pad, q0, WINDOW + Q_BLOCK, axis=1)
        w_pos = q0 - WINDOW + jnp.arange(WINDOW + Q_BLOCK)
        o_b = nsa_attend(sl(q_b), q_pos, sl(gates), ck, cv, c_pos, sel_map, sk, sv,
                         w[:, :, 0], w[:, :, 1], w_pos)
        return o_a, o_b

    o_a, o_b = lax.map(block, jnp.arange(T // Q_BLOCK))
    o_a = jnp.moveaxis(o_a, 0, 1).reshape(B, T, SB_WIDTH)
    o_b = jnp.moveaxis(o_b, 0, 1).reshape(B, T, NSA_WIDTH)
    new_win = win_kv[:, T - min(WINDOW, T):]
    return o_a, o_b, sb_kv, nsa_kv, new_win


def mixer_sample(x, cache_sb, cache_nsa, cache_win, page_table, w_in, cmp_pe, cmp_w1, cmp_w2):
    B, T, _ = x.shape
    past = page_table.shape[1] * cache_sb.shape[1]
    q_a, sb_kv, q_b, nsa_kv, win_kv, gates = project(x, w_in)
    q_pos = past + jnp.arange(T)
    k_pos = jnp.arange(past + T)
    sb_all = jnp.concatenate([cache_sb[page_table].reshape(B, past, 2, H_SB, HEAD_DIM), sb_kv], axis=1)
    o_a = stick_breaking(q_a, q_pos, sb_all[:, :, 0], sb_all[:, :, 1], k_pos).reshape(B, T, SB_WIDTH)
    nsa_all = jnp.concatenate([cache_nsa[page_table].reshape(B, past, 4, KV_GROUPS, HEAD_DIM), nsa_kv], axis=1)
    ck, cv, c_pos, sel_map = compress(nsa_all[:, :, 0], nsa_all[:, :, 1], cmp_pe, cmp_w1, cmp_w2)
    n_sel = sel_map.shape[1]
    sk = sel_prep(nsa_all[:, :, 2], n_sel)
    sv = sel_prep(nsa_all[:, :, 3], n_sel)
    wb = cache_win.shape[1]
    win_all = jnp.concatenate([cache_win, win_kv], axis=1)
    w_pos = past - wb + jnp.arange(wb + T)
    o_b = nsa_attend(q_b, q_pos, gates, ck, cv, c_pos, sel_map, sk, sv,
                     win_all[:, :, 0], win_all[:, :, 1], w_pos)
    new_win = win_all[:, T:]
    return o_a, o_b, sb_kv, nsa_kv, new_win


def peer_chunk(h, w_q, sub_keys, u_tab, v_tab):
    C = h.shape[0]
    q = (h @ w_q).reshape(C, PEER_HEADS, 2, D_KEY // 2)
    s = jnp.einsum('cphd,phkd->cphk', q, sub_keys).astype(jnp.float32)
    s1, i1 = lax.top_k(s[:, :, 0], PEER_TOPK)
    s2, i2 = lax.top_k(s[:, :, 1], PEER_TOPK)
    cand = (s1[..., :, None] + s2[..., None, :]).reshape(C, PEER_HEADS, PEER_TOPK * PEER_TOPK)
    cidx = (i1[..., :, None] * N_KEYS + i2[..., None, :]).reshape(C, PEER_HEADS, PEER_TOPK * PEER_TOPK)
    top_s, top_i = lax.top_k(cand, PEER_TOPK)
    e = jnp.take_along_axis(cidx, top_i, axis=-1)
    g = jax.nn.softmax(top_s, axis=-1)
    a = jax.nn.gelu(jnp.einsum('cpkd,cd->cpk', u_tab[e], h)).astype(jnp.float32)
    return jnp.einsum('cpk,cpkd->cd', (g * a).astype(v_tab.dtype), v_tab[e])


def peer(h, w_q, sub_keys, u_tab, v_tab):
    n = h.shape[0]
    n_chunks = -(-n // TOK_CHUNK)
    hp = jnp.pad(h, ((0, n_chunks * TOK_CHUNK - n), (0, 0))).reshape(n_chunks, TOK_CHUNK, D_MODEL)
    out = lax.map(lambda hc: peer_chunk(hc, w_q, sub_keys, u_tab, v_tab), hp)
    return out.reshape(-1, D_MODEL)[:n]


def block_out(x, o_a, o_b, mix_g, w_out, ln1_g, ln1_b, w_q, sub_keys, u_tab, v_tab, ln2_g, ln2_b):
    m = jnp.concatenate([rms_norm(o_a, mix_g[:SB_WIDTH]), rms_norm(o_b, mix_g[SB_WIDTH:])], axis=-1) @ w_out
    h = layer_norm(ALPHA * x + m, ln1_g, ln1_b)
    f = peer(h.reshape(-1, D_MODEL), w_q, sub_keys, u_tab, v_tab).reshape(h.shape)
    return layer_norm(ALPHA * h + f, ln2_g, ln2_b)


def setup_inputs(seed: int = 0) -> dict:
    key = jax.random.key(seed)
    ks = jax.random.split(key, 20)
    nrm = lambda k, shape, scale: jax.random.normal(k, shape, jnp.float32) * scale
    n_pages = PAST_LEN // PAGE_SIZE
    n_used = DEC_BATCH * n_pages
    n_pool = n_used + max(1, n_used // 4)
    win_buf = min(WINDOW, PAST_LEN)
    page_table = jax.random.permutation(ks[0], n_pool)[:n_used].reshape(DEC_BATCH, n_pages).astype(jnp.int32)
    return {
        'x_prompt': nrm(ks[1], (BATCH, SEQ, D_MODEL), 1.0),
        'x_sample': nrm(ks[2], (DEC_BATCH, DEC_SEQ, D_MODEL), 1.0),
        'cache_sb_kv': nrm(ks[3], (DEPTH, n_pool, PAGE_SIZE, 2, H_SB, HEAD_DIM), 1.0),
        'cache_nsa_kv': nrm(ks[4], (DEPTH, n_pool, PAGE_SIZE, 4, KV_GROUPS, HEAD_DIM), 1.0),
        'cache_win_kv': nrm(ks[5], (DEPTH, DEC_BATCH, win_buf, 2, KV_GROUPS, HEAD_DIM), 1.0),
        'page_table': page_table,
        'w_in': nrm(ks[6], (DEPTH, D_MODEL, IN_COLS), D_MODEL ** -0.5),
        'cmp_pe': nrm(ks[7], (DEPTH, 2, COMP_BLOCK, HEAD_DIM), 0.1),
        'cmp_w1': nrm(ks[8], (DEPTH, 2, COMP_BLOCK, HEAD_DIM, COMP_HID), (COMP_BLOCK * HEAD_DIM) ** -0.5),
        'cmp_w2': nrm(ks[9], (DEPTH, 2, COMP_HID, HEAD_DIM), COMP_HID ** -0.5),
        'mix_norm_g': 1.0 + nrm(ks[10], (DEPTH, MIX_WIDTH), 0.02),
        'w_out': nrm(ks[11], (DEPTH, MIX_WIDTH, D_MODEL), BETA * MIX_WIDTH ** -0.5),
        'ln1_g': 1.0 + nrm(ks[12], (DEPTH, D_MODEL), 0.02),
        'ln1_b': nrm(ks[13], (DEPTH, D_MODEL), 0.02),
        'peer_w_q': nrm(ks[14], (DEPTH, D_MODEL, PEER_HEADS * D_KEY), D_MODEL ** -0.5),
        'peer_sub_keys': nrm(ks[15], (DEPTH, PEER_HEADS, 2, N_KEYS, D_KEY // 2), (D_KEY // 2) ** -0.5),
        'peer_u': nrm(ks[16], (DEPTH, N_EXPERTS, D_MODEL), D_MODEL ** -0.5),
        'peer_v': nrm(ks[17], (DEPTH, N_EXPERTS, D_MODEL), BETA),
        'ln2_g': 1.0 + nrm(ks[18], (DEPTH, D_MODEL), 0.02),
        'ln2_b': nrm(ks[19], (DEPTH, D_MODEL), 0.02),
    }


def reference(x_prompt, x_sample, cache_sb_kv, cache_nsa_kv, cache_win_kv, page_table,
              w_in, cmp_pe, cmp_w1, cmp_w2, mix_norm_g, w_out, ln1_g, ln1_b,
              peer_w_q, peer_sub_keys, peer_u, peer_v, ln2_g, ln2_b):
    h_p, h_s = x_prompt, x_sample
    sb_p_l, nsa_p_l, win_p_l, sb_s_l, nsa_s_l, win_s_l = [], [], [], [], [], []
    for l in range(DEPTH):
        o_a, o_b, sb_p, nsa_p, win_p = mixer_prompt(h_p, w_in[l], cmp_pe[l], cmp_w1[l], cmp_w2[l])
        h_p = block_out(h_p, o_a, o_b, mix_norm_g[l], w_out[l], ln1_g[l], ln1_b[l],
                        peer_w_q[l], peer_sub_keys[l], peer_u[l], peer_v[l], ln2_g[l], ln2_b[l])
        o_a, o_b, sb_s, nsa_s, win_s = mixer_sample(h_s, cache_sb_kv[l], cache_nsa_kv[l], cache_win_kv[l],
                                                    page_table, w_in[l], cmp_pe[l], cmp_w1[l], cmp_w2[l])
        h_s = block_out(h_s, o_a, o_b, mix_norm_g[l], w_out[l], ln1_g[l], ln1_b[l],
                        peer_w_q[l], peer_sub_keys[l], peer_u[l], peer_v[l], ln2_g[l], ln2_b[l])
        sb_p_l.append(sb_p); nsa_p_l.append(nsa_p); win_p_l.append(win_p)
        sb_s_l.append(sb_s); nsa_s_l.append(nsa_s); win_s_l.append(win_s)
    sb_kv_prompt = jnp.stack(sb_p_l)
    nsa_kv_prompt = jnp.stack(nsa_p_l)
    win_kv_prompt = jnp.stack(win_p_l)
    sb_kv_sample = jnp.stack(sb_s_l)
    nsa_kv_sample = jnp.stack(nsa_s_l)
    win_kv_sample = jnp.stack(win_s_l)
    return (h_p, h_s, sb_kv_prompt, nsa_kv_prompt, win_kv_prompt, sb_kv_sample, nsa_kv_sample, win_kv_sample)
```

```python
import math
from functools import partial

import jax, jax.numpy as jnp
from jax import lax
import numpy as np
from jax.experimental import pallas as pl
from jax.experimental.pallas import tpu as pltpu

D_MODEL = 1024
HEAD_DIM = 64
MIX_WIDTH = D_MODEL
SB_WIDTH = MIX_WIDTH // 2
NSA_WIDTH = MIX_WIDTH - SB_WIDTH
H_SB = SB_WIDTH // HEAD_DIM
H_NSA = NSA_WIDTH // HEAD_DIM
KV_GROUPS = 2
HPG = H_NSA // KV_GROUPS
COMP_BLOCK = 32
COMP_STRIDE = 16
COMP_HID = 128
SEL_BLOCK = 64
TOP_N = 8
WINDOW = 512
Q_BLOCK = 128
PEER_HEADS = 8
N_KEYS = 128
N_EXPERTS = N_KEYS * N_KEYS
PEER_TOPK = 16
D_KEY = 256
TOK_CHUNK = 256
DEPTH = 1
ALPHA = (2.0 * DEPTH) ** 0.25
LN_EPS = 1e-5
NEG = -1e30
FORCE = 1e4
SCALE = HEAD_DIM ** -0.5

OFF_SBKV = SB_WIDTH
OFF_QB = 3 * SB_WIDTH
OFF_NSAKV = OFF_QB + NSA_WIDTH
OFF_WIN = OFF_NSAKV + 4 * KV_GROUPS * HEAD_DIM
OFF_GATE = OFF_WIN + 2 * KV_GROUPS * HEAD_DIM
IN_COLS = OFF_GATE + 3 * H_NSA

LANE = 128
IN_COLS_PAD = -(-IN_COLS // LANE) * LANE
PROJ_ROWS = 512


def _proj_kernel(x_ref, w_ref, z_ref):
    z_ref[...] = jnp.dot(x_ref[...].astype(jnp.bfloat16), w_ref[...],
                         preferred_element_type=jnp.float32)


def project_pallas(x2d, w_in_pad_bf16):
    n = x2d.shape[0]
    rows = min(PROJ_ROWS, n)
    return pl.pallas_call(
        _proj_kernel,
        grid=(n // rows,),
        in_specs=[pl.BlockSpec((rows, D_MODEL), lambda i: (i, 0)),
                  pl.BlockSpec((D_MODEL, IN_COLS_PAD), lambda i: (0, 0))],
        out_specs=pl.BlockSpec((rows, IN_COLS_PAD), lambda i: (i, 0)),
        out_shape=jax.ShapeDtypeStruct((n, IN_COLS_PAD), jnp.float32),
        compiler_params=pltpu.CompilerParams(
            dimension_semantics=("arbitrary",), vmem_limit_bytes=48 * 1024 * 1024),
        name="in_proj",
    )(x2d, w_in_pad_bf16)


def layer_norm(x, g, b):
    xf = x.astype(jnp.float32)
    mu = xf.mean(-1, keepdims=True)
    var = jnp.square(xf - mu).mean(-1, keepdims=True)
    return ((xf - mu) * lax.rsqrt(var + LN_EPS) * g + b).astype(x.dtype)


def rms_norm(x, g):
    xf = x.astype(jnp.float32)
    return (xf * lax.rsqrt(jnp.square(xf).mean(-1, keepdims=True) + LN_EPS) * g).astype(x.dtype)


def masked_softmax(s, mask):
    s = jnp.where(mask, s, NEG)
    e = jnp.exp(s - s.max(-1, keepdims=True)) * mask
    tot = e.sum(-1, keepdims=True)
    return e / jnp.where(tot > 0, tot, 1.0)


def alibi_slopes():
    i = jnp.arange(1, H_NSA + 1, dtype=jnp.float32)
    return (2.0 ** (-8.0 * i / H_NSA)).reshape(KV_GROUPS, HPG)


def project(x, w_in_pad):
    B, T, _ = x.shape
    z = project_pallas(x.reshape(B * T, D_MODEL), w_in_pad).reshape(B, T, IN_COLS_PAD)
    q_a = z[..., :SB_WIDTH].reshape(B, T, H_SB, HEAD_DIM)
    sb_kv = z[..., OFF_SBKV:OFF_QB].reshape(B, T, 2, H_SB, HEAD_DIM)
    q_b = z[..., OFF_QB:OFF_NSAKV].reshape(B, T, H_NSA, HEAD_DIM)
    nsa_kv = z[..., OFF_NSAKV:OFF_WIN].reshape(B, T, 4, KV_GROUPS, HEAD_DIM)
    win_kv = z[..., OFF_WIN:OFF_GATE].reshape(B, T, 2, KV_GROUPS, HEAD_DIM)
    gates = jax.nn.sigmoid(z[..., OFF_GATE:IN_COLS]).reshape(B, T, H_NSA, 3)
    return q_a, sb_kv, q_b, nsa_kv, win_kv, gates


def stick_breaking(q, q_pos, k, v, k_pos):
    z = jnp.einsum('bthd,bshd->bhts', q, k).astype(jnp.float32) * SCALE
    mask = k_pos[None, :] < q_pos[:, None]
    log_keep = jnp.where(mask, jax.nn.log_sigmoid(-z), 0.0)
    cum = jnp.cumsum(log_keep, axis=-1)
    after = cum[..., -1:] - cum
    w = jnp.where(mask, jnp.exp(jax.nn.log_sigmoid(z) + after), 0.0)
    return jnp.einsum('bhts,bshd->bthd', w.astype(v.dtype), v)


def compress(kc, vc, cmp_pe, cmp_w1, cmp_w2):
    L = kc.shape[1]
    n_c = (L - COMP_BLOCK) // COMP_STRIDE + 1
    starts = jnp.arange(n_c) * COMP_STRIDE
    idx = starts[:, None] + jnp.arange(COMP_BLOCK)[None, :]

    def phi(t, j):
        blk = t[:, idx] + cmp_pe[j][None, None, :, None, :]
        hdn = jax.nn.gelu(jnp.einsum('bnlgd,ldh->bngh', blk, cmp_w1[j]))
        return jnp.einsum('bngh,hd->bngd', hdn, cmp_w2[j])

    ck = phi(kc, 0)
    cv = phi(vc, 1)
    c_pos = starts + COMP_BLOCK - 1
    n_sel = -(-L // SEL_BLOCK)
    sel_map = jax.nn.one_hot(starts // SEL_BLOCK, n_sel, dtype=jnp.float32)
    return ck, cv, c_pos, sel_map


def sel_prep(k, n_sel):
    pad = n_sel * SEL_BLOCK - k.shape[1]
    return jnp.pad(k, ((0, 0), (0, pad), (0, 0), (0, 0))).transpose(0, 2, 1, 3)


def gather_rows(a, idx):
    return jax.vmap(jax.vmap(lambda r, i: r[i]))(a, idx)


def nsa_attend(q, q_pos, gates, ck, cv, c_pos, sel_map, sk, sv, wk, wv, w_pos):
    B, Tq = q.shape[:2]
    qg = q.reshape(B, Tq, KV_GROUPS, HPG, HEAD_DIM)
    slopes = alibi_slopes()[None, :, :, None, None]
    qp = q_pos.astype(jnp.float32)
    s = jnp.einsum('btgrd,bngd->bgrtn', qg, ck).astype(jnp.float32) * SCALE
    s = s - slopes * (qp[:, None] - c_pos[None, :].astype(jnp.float32))
    p_c = masked_softmax(s, c_pos[None, :] <= q_pos[:, None])
    o_c = jnp.einsum('bgrtn,bngd->btgrd', p_c.astype(cv.dtype), cv)
    imp = jnp.einsum('bgrtn,nj->bgtj', p_c, sel_map)
    n_sel = sel_map.shape[1]
    blk = jnp.arange(n_sel)
    valid = blk[None, :] * SEL_BLOCK <= q_pos[:, None]
    forced = (blk[None, :] == (q_pos // SEL_BLOCK)[:, None]) | (blk[None, :] == 0)
    imp = jnp.where(forced, FORCE, jnp.where(valid, imp, -1.0))
    n_top = min(TOP_N, n_sel)
    _, idx = lax.top_k(imp, n_top)
    kpos = (idx[..., None] * SEL_BLOCK + jnp.arange(SEL_BLOCK)).reshape(B, KV_GROUPS, Tq, n_top * SEL_BLOCK)
    flat = kpos.reshape(B, KV_GROUPS, -1)
    ksel = gather_rows(sk, flat).reshape(B, KV_GROUPS, Tq, n_top * SEL_BLOCK, HEAD_DIM)
    vsel = gather_rows(sv, flat).reshape(B, KV_GROUPS, Tq, n_top * SEL_BLOCK, HEAD_DIM)
    dist_s = (q_pos[None, None, :, None] - kpos)[:, :, None]
    s = jnp.einsum('btgrd,bgtsd->bgrts', qg, ksel).astype(jnp.float32) * SCALE
    s = s - slopes * dist_s.astype(jnp.float32)
    p_s = masked_softmax(s, dist_s >= 0)
    o_s = jnp.einsum('bgrts,bgtsd->btgrd', p_s.astype(vsel.dtype), vsel)
    dist_w = q_pos[:, None] - w_pos[None, :]
    mask_w = (dist_w >= 0) & (dist_w < WINDOW) & (w_pos[None, :] >= 0)
    s = jnp.einsum('btgrd,bsgd->bgrts', qg, wk).astype(jnp.float32) * SCALE
    s = s - slopes * dist_w.astype(jnp.float32)
    p_w = masked_softmax(s, mask_w)
    o_w = jnp.einsum('bgrts,bsgd->btgrd', p_w.astype(wv.dtype), wv)
    g = gates.reshape(B, Tq, KV_GROUPS, HPG, 3)
    o = g[..., 0:1] * o_c + g[..., 1:2] * o_s + g[..., 2:3] * o_w
    return o.reshape(B, Tq, NSA_WIDTH)


def mixer_prompt(x, w_in_pad, cmp_pe, cmp_w1, cmp_w2):
    B, T, _ = x.shape
    q_a, sb_kv, q_b, nsa_kv, win_kv, gates = project(x, w_in_pad)
    k_a, v_a = sb_kv[:, :, 0], sb_kv[:, :, 1]
    ck, cv, c_pos, sel_map = compress(nsa_kv[:, :, 0], nsa_kv[:, :, 1], cmp_pe, cmp_w1, cmp_w2)
    n_sel = sel_map.shape[1]
    sk = sel_prep(nsa_kv[:, :, 2], n_sel)
    sv = sel_prep(nsa_kv[:, :, 3], n_sel)
    win_pad = jnp.pad(win_kv, ((0, 0), (WINDOW, 0), (0, 0), (0, 0), (0, 0)))
    k_pos = jnp.arange(T)

    def block(qb):
        q0 = qb * Q_BLOCK
        q_pos = q0 + jnp.arange(Q_BLOCK)
        sl = lambda a: lax.dynamic_slice_in_dim(a, q0, Q_BLOCK, axis=1)
        o_a = stick_breaking(sl(q_a), q_pos, k_a, v_a, k_pos).reshape(B, Q_BLOCK, SB_WIDTH)
        w = lax.dynamic_slice_in_dim(win_pad, q0, WINDOW + Q_BLOCK, axis=1)
        w_pos = q0 - WINDOW + jnp.arange(WINDOW + Q_BLOCK)
        o_b = nsa_attend(sl(q_b), q_pos, sl(gates), ck, cv, c_pos, sel_map, sk, sv,
                         w[:, :, 0], w[:, :, 1], w_pos)
        return o_a, o_b

    o_a, o_b = lax.map(block, jnp.arange(T // Q_BLOCK))
    o_a = jnp.moveaxis(o_a, 0, 1).reshape(B, T, SB_WIDTH)
    o_b = jnp.moveaxis(o_b, 0, 1).reshape(B, T, NSA_WIDTH)
    new_win = win_kv[:, T - min(WINDOW, T):]
    return o_a, o_b, sb_kv, nsa_kv, new_win


def mixer_sample(x, cache_sb, cache_nsa, cache_win, page_table, w_in_pad, cmp_pe, cmp_w1, cmp_w2):
    B, T, _ = x.shape
    past = page_table.shape[1] * cache_sb.shape[1]
    q_a, sb_kv, q_b, nsa_kv, win_kv, gates = project(x, w_in_pad)
    q_pos = past + jnp.arange(T)
    k_pos = jnp.arange(past + T)
    sb_all = jnp.concatenate([cache_sb[page_table].reshape(B, past, 2, H_SB, HEAD_DIM), sb_kv], axis=1)
    o_a = stick_breaking(q_a, q_pos, sb_all[:, :, 0], sb_all[:, :, 1], k_pos).reshape(B, T, SB_WIDTH)
    nsa_all = jnp.concatenate([cache_nsa[page_table].reshape(B, past, 4, KV_GROUPS, HEAD_DIM), nsa_kv], axis=1)
    ck, cv, c_pos, sel_map = compress(nsa_all[:, :, 0], nsa_all[:, :, 1], cmp_pe, cmp_w1, cmp_w2)
    n_sel = sel_map.shape[1]
    sk = sel_prep(nsa_all[:, :, 2], n_sel)
    sv = sel_prep(nsa_all[:, :, 3], n_sel)
    wb = cache_win.shape[1]
    win_all = jnp.concatenate([cache_win, win_kv], axis=1)
    w_pos = past - wb + jnp.arange(wb + T)
    o_b = nsa_attend(q_b, q_pos, gates, ck, cv, c_pos, sel_map, sk, sv,
                     win_all[:, :, 0], win_all[:, :, 1], w_pos)
    new_win = win_all[:, T:]
    return o_a, o_b, sb_kv, nsa_kv, new_win


def peer_chunk(h, w_q, sub_keys, u_tab, v_tab):
    C = h.shape[0]
    q = (h @ w_q).reshape(C, PEER_HEADS, 2, D_KEY // 2)
    s = jnp.einsum('cphd,phkd->cphk', q, sub_keys).astype(jnp.float32)
    s1, i1 = lax.top_k(s[:, :, 0], PEER_TOPK)
    s2, i2 = lax.top_k(s[:, :, 1], PEER_TOPK)
    cand = (s1[..., :, None] + s2[..., None, :]).reshape(C, PEER_HEADS, PEER_TOPK * PEER_TOPK)
    cidx = (i1[..., :, None] * N_KEYS + i2[..., None, :]).reshape(C, PEER_HEADS, PEER_TOPK * PEER_TOPK)
    top_s, top_i = lax.top_k(cand, PEER_TOPK)
    e = jnp.take_along_axis(cidx, top_i, axis=-1)
    g = jax.nn.softmax(top_s, axis=-1)
    a = jax.nn.gelu(jnp.einsum('cpkd,cd->cpk', u_tab[e], h)).astype(jnp.float32)
    return jnp.einsum('cpk,cpkd->cd', (g * a).astype(v_tab.dtype), v_tab[e])


def peer(h, w_q, sub_keys, u_tab, v_tab):
    n = h.shape[0]
    n_chunks = -(-n // TOK_CHUNK)
    hp = jnp.pad(h, ((0, n_chunks * TOK_CHUNK - n), (0, 0))).reshape(n_chunks, TOK_CHUNK, D_MODEL)
    out = lax.map(lambda hc: peer_chunk(hc, w_q, sub_keys, u_tab, v_tab), hp)
    return out.reshape(-1, D_MODEL)[:n]


def block_out(x, o_a, o_b, mix_g, w_out, ln1_g, ln1_b, w_q, sub_keys, u_tab, v_tab, ln2_g, ln2_b):
    m = jnp.concatenate([rms_norm(o_a, mix_g[:SB_WIDTH]), rms_norm(o_b, mix_g[SB_WIDTH:])], axis=-1) @ w_out
    h = layer_norm(ALPHA * x + m, ln1_g, ln1_b)
    f = peer(h.reshape(-1, D_MODEL), w_q, sub_keys, u_tab, v_tab).reshape(h.shape)
    return layer_norm(ALPHA * h + f, ln2_g, ln2_b)


def kernel(x_prompt, x_sample, cache_sb_kv, cache_nsa_kv, cache_win_kv, page_table, w_in, cmp_pe, cmp_w1, cmp_w2, mix_norm_g, w_out, ln1_g, ln1_b, peer_w_q, peer_sub_keys, peer_u, peer_v, ln2_g, ln2_b):
    l = 0
    w_in_pad = jnp.pad(w_in[l], ((0, 0), (0, IN_COLS_PAD - IN_COLS))).astype(jnp.bfloat16)
    o_a, o_b, sb_p, nsa_p, win_p = mixer_prompt(x_prompt, w_in_pad, cmp_pe[l], cmp_w1[l], cmp_w2[l])
    h_p = block_out(x_prompt, o_a, o_b, mix_norm_g[l], w_out[l], ln1_g[l], ln1_b[l],
                    peer_w_q[l], peer_sub_keys[l], peer_u[l], peer_v[l], ln2_g[l], ln2_b[l])
    o_a, o_b, sb_s, nsa_s, win_s = mixer_sample(x_sample, cache_sb_kv[l], cache_nsa_kv[l], cache_win_kv[l],
                                                page_table, w_in_pad, cmp_pe[l], cmp_w1[l], cmp_w2[l])
    h_s = block_out(x_sample, o_a, o_b, mix_norm_g[l], w_out[l], ln1_g[l], ln1_b[l],
                    peer_w_q[l], peer_sub_keys[l], peer_u[l], peer_v[l], ln2_g[l], ln2_b[l])
    return (h_p, h_s, sb_p[None], nsa_p[None], win_p[None], sb_s[None], nsa_s[None], win_s[None])
```

```python
import math
from functools import partial

import jax, jax.numpy as jnp
from jax import lax
import numpy as np
from jax.experimental import pallas as pl
from jax.experimental.pallas import tpu as pltpu

D_MODEL = 1024
HEAD_DIM = 64
MIX_WIDTH = D_MODEL
SB_WIDTH = MIX_WIDTH // 2
NSA_WIDTH = MIX_WIDTH - SB_WIDTH
H_SB = SB_WIDTH // HEAD_DIM
H_NSA = NSA_WIDTH // HEAD_DIM
KV_GROUPS = 2
HPG = H_NSA // KV_GROUPS
COMP_BLOCK = 32
COMP_STRIDE = 16
COMP_HID = 128
SEL_BLOCK = 64
TOP_N = 8
WINDOW = 512
Q_BLOCK = 128
PEER_HEADS = 8
N_KEYS = 128
N_EXPERTS = N_KEYS * N_KEYS
PEER_TOPK = 16
D_KEY = 256
TOK_CHUNK = 256
DEPTH = 1
ALPHA = (2.0 * DEPTH) ** 0.25
LN_EPS = 1e-5
NEG = -1e30
FORCE = 1e4
SCALE = HEAD_DIM ** -0.5

OFF_SBKV = SB_WIDTH
OFF_QB = 3 * SB_WIDTH
OFF_NSAKV = OFF_QB + NSA_WIDTH
OFF_WIN = OFF_NSAKV + 4 * KV_GROUPS * HEAD_DIM
OFF_GATE = OFF_WIN + 2 * KV_GROUPS * HEAD_DIM
IN_COLS = OFF_GATE + 3 * H_NSA

LANE = 128
IN_COLS_PAD = -(-IN_COLS // LANE) * LANE
PROJ_ROWS = 512


def _proj_kernel(x_ref, w_ref, z_ref):
    z_ref[...] = jnp.dot(x_ref[...].astype(jnp.bfloat16), w_ref[...],
                         preferred_element_type=jnp.float32)


def project_pallas(x2d, w_in_pad_bf16):
    n = x2d.shape[0]
    rows = min(PROJ_ROWS, n)
    return pl.pallas_call(
        _proj_kernel,
        grid=(n // rows,),
        in_specs=[pl.BlockSpec((rows, D_MODEL), lambda i: (i, 0)),
                  pl.BlockSpec((D_MODEL, IN_COLS_PAD), lambda i: (0, 0))],
        out_specs=pl.BlockSpec((rows, IN_COLS_PAD), lambda i: (i, 0)),
        out_shape=jax.ShapeDtypeStruct((n, IN_COLS_PAD), jnp.float32),
        compiler_params=pltpu.CompilerParams(
            dimension_semantics=("arbitrary",), vmem_limit_bytes=48 * 1024 * 1024),
        name="in_proj",
    )(x2d, w_in_pad_bf16)


N_SEL = 32
N_CMP_PAD = 128
SEL_TILE = 512
WIN_TILE = 128
ALIBI = [[2.0 ** (-8.0 * (g * HPG + r + 1) / H_NSA) for r in range(HPG)] for g in range(KV_GROUPS)]


def _dot_nt(a, b):
    return lax.dot_general(a, b, (((1,), (1,)), ((), ())), preferred_element_type=jnp.float32)


def _split_dot(a, b_exact):
    hi = a.astype(jnp.bfloat16)
    r1 = a - hi.astype(jnp.float32)
    mid = r1.astype(jnp.bfloat16)
    lo = (r1 - mid.astype(jnp.float32)).astype(jnp.bfloat16)
    d = lambda x: jnp.dot(x, b_exact, preferred_element_type=jnp.float32)
    return d(hi) + d(mid) + d(lo)


def _nsa_prompt_kernel(q_ref, gz_ref, ck_ref, cv_ref, nsa_ref, win_ref, o_ref, m_ref, l_ref, acc_ref):
    QB = Q_BLOCK
    qb = pl.program_id(1)
    q0 = qb * QB
    bf = jnp.bfloat16
    f32 = jnp.float32
    gate = jax.nn.sigmoid(gz_ref[0])
    row = lax.broadcasted_iota(jnp.int32, (QB, 1), 0)
    qpos = q0 + row

    def reset():
        m_ref[...] = jnp.full(m_ref.shape, NEG, f32)
        l_ref[...] = jnp.zeros(l_ref.shape, f32)
        acc_ref[...] = jnp.zeros(acc_ref.shape, f32)

    def flash_step(r, s, valid, v):
        s = jnp.where(valid, s, NEG)
        m_old = m_ref[r]
        m_new = jnp.maximum(m_old, s.max(-1, keepdims=True))
        p = jnp.where(valid, jnp.exp(s - m_new), 0.0)
        alpha = jnp.exp(m_old - m_new)
        l_ref[r] = alpha * l_ref[r] + p.sum(-1, keepdims=True)
        acc_ref[r] = alpha * acc_ref[r] + jnp.dot(p.astype(bf), v, preferred_element_type=f32)
        m_ref[r] = m_new
        return p

    def finish(g, branch, first):
        for r in range(HPG):
            h = g * HPG + r
            tot = l_ref[r]
            o = acc_ref[r] / jnp.where(tot > 0, tot, 1.0)
            o = gate[:, 3 * h + branch:3 * h + branch + 1] * o
            sl = (0, slice(None), slice(h * HEAD_DIM, (h + 1) * HEAD_DIM))
            if first:
                o_ref[sl] = o
            else:
                o_ref[sl] = o_ref[sl] + o

    for g in range(KV_GROUPS):
        qh = [(q_ref[0, :, (g * HPG + r) * HEAD_DIM:(g * HPG + r + 1) * HEAD_DIM] * SCALE).astype(bf)
              for r in range(HPG)]

        reset()
        n_i = lax.broadcasted_iota(jnp.int32, (1, N_CMP_PAD), 1)
        cpos = n_i * COMP_STRIDE + (COMP_BLOCK - 1)
        n_cmp = (pl.num_programs(1) * QB - COMP_BLOCK) // COMP_STRIDE + 1
        valid_c = (cpos <= qpos) & (n_i < n_cmp)
        dist_c = (qpos - cpos).astype(f32)
        ckg = ck_ref[0, g].astype(bf)
        cvg = cv_ref[0, g].astype(bf)
        psum = jnp.zeros((QB, N_CMP_PAD), f32)
        for r in range(HPG):
            s = _dot_nt(qh[r], ckg) - ALIBI[g][r] * dist_c
            p = flash_step(r, s, valid_c, cvg)
            tot = l_ref[r]
            psum = psum + p / jnp.where(tot > 0, tot, 1.0)
        finish(g, 0, True)

        nn = lax.broadcasted_iota(jnp.int32, (N_CMP_PAD, N_SEL), 0)
        jj = lax.broadcasted_iota(jnp.int32, (N_CMP_PAD, N_SEL), 1)
        sel_map = ((nn * COMP_STRIDE) // SEL_BLOCK == jj).astype(bf)
        imp = _split_dot(psum, sel_map)
        blk = lax.broadcasted_iota(jnp.int32, (QB, N_SEL), 1)
        forced = (blk == qpos // SEL_BLOCK) | (blk == 0)
        imp = jnp.where(forced, FORCE, jnp.where(blk * SEL_BLOCK <= qpos, imp, -1.0))
        chosen = jnp.zeros((QB, N_SEL), f32)
        for _ in range(TOP_N):
            mx = imp.max(-1, keepdims=True)
            first_max = jnp.where(imp == mx, blk, N_SEL).min(-1, keepdims=True)
            pick = blk == first_max
            chosen = jnp.where(pick, 1.0, chosen)
            imp = jnp.where(pick, -2.0, imp)
        chosen = chosen.astype(bf)

        reset()

        def sel_body(kt, carry):
            k0 = pl.multiple_of(kt * SEL_TILE, SEL_TILE)
            k = nsa_ref[0, pl.ds(k0, SEL_TILE), (4 + g) * HEAD_DIM:(5 + g) * HEAD_DIM].astype(bf)
            v = nsa_ref[0, pl.ds(k0, SEL_TILE), (6 + g) * HEAD_DIM:(7 + g) * HEAD_DIM].astype(bf)
            col = lax.broadcasted_iota(jnp.int32, (1, SEL_TILE), 1) + k0
            dist = qpos - col
            ej = lax.broadcasted_iota(jnp.int32, (N_SEL, SEL_TILE), 0)
            ec = lax.broadcasted_iota(jnp.int32, (N_SEL, SEL_TILE), 1) + k0
            expand = (ec // SEL_BLOCK == ej).astype(bf)
            in_block = jnp.dot(chosen, expand, preferred_element_type=f32)
            valid = (in_block > 0.5) & (dist >= 0)
            distf = dist.astype(f32)
            for r in range(HPG):
                s = _dot_nt(qh[r], k) - ALIBI[g][r] * distf
                flash_step(r, s, valid, v)
            return carry

        lax.fori_loop(0, (q0 + QB + SEL_TILE - 1) // SEL_TILE, sel_body, 0)
        finish(g, 1, False)

        reset()

        def win_body(kt, carry):
            k0 = pl.multiple_of(kt * WIN_TILE, WIN_TILE)
            k = win_ref[0, pl.ds(k0, WIN_TILE), g * HEAD_DIM:(g + 1) * HEAD_DIM].astype(bf)
            v = win_ref[0, pl.ds(k0, WIN_TILE), (2 + g) * HEAD_DIM:(3 + g) * HEAD_DIM].astype(bf)
            col = lax.broadcasted_iota(jnp.int32, (1, WIN_TILE), 1) + k0
            dist = qpos - col
            valid = (dist >= 0) & (dist < WINDOW)
            distf = dist.astype(f32)
            for r in range(HPG):
                s = _dot_nt(qh[r], k) - ALIBI[g][r] * distf
                flash_step(r, s, valid, v)
            return carry

        lax.fori_loop(jnp.maximum(qb - WINDOW // WIN_TILE, 0), qb + 1, win_body, 0)
        finish(g, 2, False)


def nsa_prompt_pallas(q_b, gate_z, ck, cv, nsa_kv, win_kv):
    B, T, _ = q_b.shape
    assert T % Q_BLOCK == 0 and T // SEL_BLOCK <= N_SEL and (T - COMP_BLOCK) // COMP_STRIDE + 1 <= N_CMP_PAD
    per_b = lambda b, i: (b, 0, 0)
    return pl.pallas_call(
        _nsa_prompt_kernel,
        grid=(B, T // Q_BLOCK),
        in_specs=[pl.BlockSpec((1, Q_BLOCK, NSA_WIDTH), lambda b, i: (b, i, 0)),
                  pl.BlockSpec((1, Q_BLOCK, 3 * H_NSA), lambda b, i: (b, i, 0)),
                  pl.BlockSpec((1, KV_GROUPS, N_CMP_PAD, HEAD_DIM), lambda b, i: (b, 0, 0, 0)),
                  pl.BlockSpec((1, KV_GROUPS, N_CMP_PAD, HEAD_DIM), lambda b, i: (b, 0, 0, 0)),
                  pl.BlockSpec((1, T, 4 * KV_GROUPS * HEAD_DIM), per_b),
                  pl.BlockSpec((1, T, 2 * KV_GROUPS * HEAD_DIM), per_b)],
        out_specs=pl.BlockSpec((1, Q_BLOCK, NSA_WIDTH), lambda b, i: (b, i, 0)),
        out_shape=jax.ShapeDtypeStruct((B, T, NSA_WIDTH), jnp.float32),
        scratch_shapes=[pltpu.VMEM((HPG, Q_BLOCK, 1), jnp.float32),
                        pltpu.VMEM((HPG, Q_BLOCK, 1), jnp.float32),
                        pltpu.VMEM((HPG, Q_BLOCK, HEAD_DIM), jnp.float32)],
        compiler_params=pltpu.CompilerParams(
            dimension_semantics=("arbitrary", "arbitrary"), vmem_limit_bytes=48 * 1024 * 1024),
        name="nsa_prompt",
    )(q_b, gate_z, ck, cv, nsa_kv, win_kv)


def layer_norm(x, g, b):
    xf = x.astype(jnp.float32)
    mu = xf.mean(-1, keepdims=True)
    var = jnp.square(xf - mu).mean(-1, keepdims=True)
    return ((xf - mu) * lax.rsqrt(var + LN_EPS) * g + b).astype(x.dtype)


def rms_norm(x, g):
    xf = x.astype(jnp.float32)
    return (xf * lax.rsqrt(jnp.square(xf).mean(-1, keepdims=True) + LN_EPS) * g).astype(x.dtype)


def masked_softmax(s, mask):
    s = jnp.where(mask, s, NEG)
    e = jnp.exp(s - s.max(-1, keepdims=True)) * mask
    tot = e.sum(-1, keepdims=True)
    return e / jnp.where(tot > 0, tot, 1.0)


def alibi_slopes():
    i = jnp.arange(1, H_NSA + 1, dtype=jnp.float32)
    return (2.0 ** (-8.0 * i / H_NSA)).reshape(KV_GROUPS, HPG)


def project(x, w_in_pad):
    B, T, _ = x.shape
    z = project_pallas(x.reshape(B * T, D_MODEL), w_in_pad).reshape(B, T, IN_COLS_PAD)
    q_a = z[..., :SB_WIDTH].reshape(B, T, H_SB, HEAD_DIM)
    sb_kv = z[..., OFF_SBKV:OFF_QB].reshape(B, T, 2, H_SB, HEAD_DIM)
    q_b = z[..., OFF_QB:OFF_NSAKV].reshape(B, T, H_NSA, HEAD_DIM)
    nsa_kv = z[..., OFF_NSAKV:OFF_WIN].reshape(B, T, 4, KV_GROUPS, HEAD_DIM)
    win_kv = z[..., OFF_WIN:OFF_GATE].reshape(B, T, 2, KV_GROUPS, HEAD_DIM)
    gates = jax.nn.sigmoid(z[..., OFF_GATE:IN_COLS]).reshape(B, T, H_NSA, 3)
    return q_a, sb_kv, q_b, nsa_kv, win_kv, gates


def stick_breaking(q, q_pos, k, v, k_pos):
    z = jnp.einsum('bthd,bshd->bhts', q, k).astype(jnp.float32) * SCALE
    mask = k_pos[None, :] < q_pos[:, None]
    log_keep = jnp.where(mask, jax.nn.log_sigmoid(-z), 0.0)
    cum = jnp.cumsum(log_keep, axis=-1)
    after = cum[..., -1:] - cum
    w = jnp.where(mask, jnp.exp(jax.nn.log_sigmoid(z) + after), 0.0)
    return jnp.einsum('bhts,bshd->bthd', w.astype(v.dtype), v)


def compress(kc, vc, cmp_pe, cmp_w1, cmp_w2):
    L = kc.shape[1]
    n_c = (L - COMP_BLOCK) // COMP_STRIDE + 1
    starts = jnp.arange(n_c) * COMP_STRIDE
    idx = starts[:, None] + jnp.arange(COMP_BLOCK)[None, :]

    def phi(t, j):
        blk = t[:, idx] + cmp_pe[j][None, None, :, None, :]
        hdn = jax.nn.gelu(jnp.einsum('bnlgd,ldh->bngh', blk, cmp_w1[j]))
        return jnp.einsum('bngh,hd->bngd', hdn, cmp_w2[j])

    ck = phi(kc, 0)
    cv = phi(vc, 1)
    c_pos = starts + COMP_BLOCK - 1
    n_sel = -(-L // SEL_BLOCK)
    sel_map = jax.nn.one_hot(starts // SEL_BLOCK, n_sel, dtype=jnp.float32)
    return ck, cv, c_pos, sel_map


def sel_prep(k, n_sel):
    pad = n_sel * SEL_BLOCK - k.shape[1]
    return jnp.pad(k, ((0, 0), (0, pad), (0, 0), (0, 0))).transpose(0, 2, 1, 3)


def gather_rows(a, idx):
    return jax.vmap(jax.vmap(lambda r, i: r[i]))(a, idx)


def nsa_attend(q, q_pos, gates, ck, cv, c_pos, sel_map, sk, sv, wk, wv, w_pos):
    B, Tq = q.shape[:2]
    qg = q.reshape(B, Tq, KV_GROUPS, HPG, HEAD_DIM)
    slopes = alibi_slopes()[None, :, :, None, None]
    qp = q_pos.astype(jnp.float32)
    s = jnp.einsum('btgrd,bngd->bgrtn', qg, ck).astype(jnp.float32) * SCALE
    s = s - slopes * (qp[:, None] - c_pos[None, :].astype(jnp.float32))
    p_c = masked_softmax(s, c_pos[None, :] <= q_pos[:, None])
    o_c = jnp.einsum('bgrtn,bngd->btgrd', p_c.astype(cv.dtype), cv)
    imp = jnp.einsum('bgrtn,nj->bgtj', p_c, sel_map)
    n_sel = sel_map.shape[1]
    blk = jnp.arange(n_sel)
    valid = blk[None, :] * SEL_BLOCK <= q_pos[:, None]
    forced = (blk[None, :] == (q_pos // SEL_BLOCK)[:, None]) | (blk[None, :] == 0)
    imp = jnp.where(forced, FORCE, jnp.where(valid, imp, -1.0))
    n_top = min(TOP_N, n_sel)
    _, idx = lax.top_k(imp, n_top)
    kpos = (idx[..., None] * SEL_BLOCK + jnp.arange(SEL_BLOCK)).reshape(B, KV_GROUPS, Tq, n_top * SEL_BLOCK)
    flat = kpos.reshape(B, KV_GROUPS, -1)
    ksel = gather_rows(sk, flat).reshape(B, KV_GROUPS, Tq, n_top * SEL_BLOCK, HEAD_DIM)
    vsel = gather_rows(sv, flat).reshape(B, KV_GROUPS, Tq, n_top * SEL_BLOCK, HEAD_DIM)
    dist_s = (q_pos[None, None, :, None] - kpos)[:, :, None]
    s = jnp.einsum('btgrd,bgtsd->bgrts', qg, ksel).astype(jnp.float32) * SCALE
    s = s - slopes * dist_s.astype(jnp.float32)
    p_s = masked_softmax(s, dist_s >= 0)
    o_s = jnp.einsum('bgrts,bgtsd->btgrd', p_s.astype(vsel.dtype), vsel)
    dist_w = q_pos[:, None] - w_pos[None, :]
    mask_w = (dist_w >= 0) & (dist_w < WINDOW) & (w_pos[None, :] >= 0)
    s = jnp.einsum('btgrd,bsgd->bgrts', qg, wk).astype(jnp.float32) * SCALE
    s = s - slopes * dist_w.astype(jnp.float32)
    p_w = masked_softmax(s, mask_w)
    o_w = jnp.einsum('bgrts,bsgd->btgrd', p_w.astype(wv.dtype), wv)
    g = gates.reshape(B, Tq, KV_GROUPS, HPG, 3)
    o = g[..., 0:1] * o_c + g[..., 1:2] * o_s + g[..., 2:3] * o_w
    return o.reshape(B, Tq, NSA_WIDTH)


def mixer_prompt(x, w_in_pad, cmp_pe, cmp_w1, cmp_w2):
    B, T, _ = x.shape
    z = project_pallas(x.reshape(B * T, D_MODEL), w_in_pad).reshape(B, T, IN_COLS_PAD)
    q_a = z[..., :SB_WIDTH].reshape(B, T, H_SB, HEAD_DIM)
    sb_kv = z[..., OFF_SBKV:OFF_QB].reshape(B, T, 2, H_SB, HEAD_DIM)
    nsa_kv = z[..., OFF_NSAKV:OFF_WIN].reshape(B, T, 4, KV_GROUPS, HEAD_DIM)
    win_kv = z[..., OFF_WIN:OFF_GATE].reshape(B, T, 2, KV_GROUPS, HEAD_DIM)
    k_a, v_a = sb_kv[:, :, 0], sb_kv[:, :, 1]
    ck, cv, c_pos, sel_map = compress(nsa_kv[:, :, 0], nsa_kv[:, :, 1], cmp_pe, cmp_w1, cmp_w2)
    pad_c = lambda c: jnp.pad(c, ((0, 0), (0, N_CMP_PAD - c.shape[1]), (0, 0), (0, 0))).transpose(0, 2, 1, 3)
    o_b = nsa_prompt_pallas(z[..., OFF_QB:OFF_NSAKV], z[..., OFF_GATE:IN_COLS], pad_c(ck), pad_c(cv),
                            z[..., OFF_NSAKV:OFF_WIN], z[..., OFF_WIN:OFF_GATE])
    k_pos = jnp.arange(T)

    def block(qb):
        q0 = qb * Q_BLOCK
        q_pos = q0 + jnp.arange(Q_BLOCK)
        sl = lambda a: lax.dynamic_slice_in_dim(a, q0, Q_BLOCK, axis=1)
        return stick_breaking(sl(q_a), q_pos, k_a, v_a, k_pos).reshape(B, Q_BLOCK, SB_WIDTH)

    o_a = lax.map(block, jnp.arange(T // Q_BLOCK))
    o_a = jnp.moveaxis(o_a, 0, 1).reshape(B, T, SB_WIDTH)
    new_win = win_kv[:, T - min(WINDOW, T):]
    return o_a, o_b, sb_kv, nsa_kv, new_win


def mixer_sample(x, cache_sb, cache_nsa, cache_win, page_table, w_in_pad, cmp_pe, cmp_w1, cmp_w2):
    B, T, _ = x.shape
    past = page_table.shape[1] * cache_sb.shape[1]
    q_a, sb_kv, q_b, nsa_kv, win_kv, gates = project(x, w_in_pad)
    q_pos = past + jnp.arange(T)
    k_pos = jnp.arange(past + T)
    sb_all = jnp.concatenate([cache_sb[page_table].reshape(B, past, 2, H_SB, HEAD_DIM), sb_kv], axis=1)
    o_a = stick_breaking(q_a, q_pos, sb_all[:, :, 0], sb_all[:, :, 1], k_pos).reshape(B, T, SB_WIDTH)
    nsa_all = jnp.concatenate([cache_nsa[page_table].reshape(B, past, 4, KV_GROUPS, HEAD_DIM), nsa_kv], axis=1)
    ck, cv, c_pos, sel_map = compress(nsa_all[:, :, 0], nsa_all[:, :, 1], cmp_pe, cmp_w1, cmp_w2)
    n_sel = sel_map.shape[1]
    sk = sel_prep(nsa_all[:, :, 2], n_sel)
    sv = sel_prep(nsa_all[:, :, 3], n_sel)
    wb = cache_win.shape[1]
    win_all = jnp.concatenate([cache_win, win_kv], axis=1)
    w_pos = past - wb + jnp.arange(wb + T)
    o_b = nsa_attend(q_b, q_pos, gates, ck, cv, c_pos, sel_map, sk, sv,
                     win_all[:, :, 0], win_all[:, :, 1], w_pos)
    new_win = win_all[:, T:]
    return o_a, o_b, sb_kv, nsa_kv, new_win


def peer_chunk(h, w_q, sub_keys, u_tab, v_tab):
    C = h.shape[0]
    q = (h @ w_q).reshape(C, PEER_HEADS, 2, D_KEY // 2)
    s = jnp.einsum('cphd,phkd->cphk', q, sub_keys).astype(jnp.float32)
    s1, i1 = lax.top_k(s[:, :, 0], PEER_TOPK)
    s2, i2 = lax.top_k(s[:, :, 1], PEER_TOPK)
    cand = (s1[..., :, None] + s2[..., None, :]).reshape(C, PEER_HEADS, PEER_TOPK * PEER_TOPK)
    cidx = (i1[..., :, None] * N_KEYS + i2[..., None, :]).reshape(C, PEER_HEADS, PEER_TOPK * PEER_TOPK)
    top_s, top_i = lax.top_k(cand, PEER_TOPK)
    e = jnp.take_along_axis(cidx, top_i, axis=-1)
    g = jax.nn.softmax(top_s, axis=-1)
    a = jax.nn.gelu(jnp.einsum('cpkd,cd->cpk', u_tab[e], h)).astype(jnp.float32)
    return jnp.einsum('cpk,cpkd->cd', (g * a).astype(v_tab.dtype), v_tab[e])


def peer(h, w_q, sub_keys, u_tab, v_tab):
    n = h.shape[0]
    n_chunks = -(-n // TOK_CHUNK)
    hp = jnp.pad(h, ((0, n_chunks * TOK_CHUNK - n), (0, 0))).reshape(n_chunks, TOK_CHUNK, D_MODEL)
    out = lax.map(lambda hc: peer_chunk(hc, w_q, sub_keys, u_tab, v_tab), hp)
    return out.reshape(-1, D_MODEL)[:n]


def block_out(x, o_a, o_b, mix_g, w_out, ln1_g, ln1_b, w_q, sub_keys, u_tab, v_tab, ln2_g, ln2_b):
    m = jnp.concatenate([rms_norm(o_a, mix_g[:SB_WIDTH]), rms_norm(o_b, mix_g[SB_WIDTH:])], axis=-1) @ w_out
    h = layer_norm(ALPHA * x + m, ln1_g, ln1_b)
    f = peer(h.reshape(-1, D_MODEL), w_q, sub_keys, u_tab, v_tab).reshape(h.shape)
    return layer_norm(ALPHA * h + f, ln2_g, ln2_b)


def kernel(x_prompt, x_sample, cache_sb_kv, cache_nsa_kv, cache_win_kv, page_table, w_in, cmp_pe, cmp_w1, cmp_w2, mix_norm_g, w_out, ln1_g, ln1_b, peer_w_q, peer_sub_keys, peer_u, peer_v, ln2_g, ln2_b):
    l = 0
    w_in_pad = jnp.pad(w_in[l], ((0, 0), (0, IN_COLS_PAD - IN_COLS))).astype(jnp.bfloat16)
    o_a, o_b, sb_p, nsa_p, win_p = mixer_prompt(x_prompt, w_in_pad, cmp_pe[l], cmp_w1[l], cmp_w2[l])
    h_p = block_out(x_prompt, o_a, o_b, mix_norm_g[l], w_out[l], ln1_g[l], ln1_b[l],
                    peer_w_q[l], peer_sub_keys[l], peer_u[l], peer_v[l], ln2_g[l], ln2_b[l])
    o_a, o_b, sb_s, nsa_s, win_s = mixer_sample(x_sample, cache_sb_kv[l], cache_nsa_kv[l], cache_win_kv[l],
                                                page_table, w_in_pad, cmp_pe[l], cmp_w1[l], cmp_w2[l])
    h_s = block_out(x_sample, o_a, o_b, mix_norm_g[l], w_out[l], ln1_g[l], ln1_b[l],
                    peer_w_q[l], peer_sub_keys[l], peer_u[l], peer_v[l], ln2_g[l], ln2_b[l])
    return (h_p, h_s, sb_p[None], nsa_p[None], win_p[None], sb_s[None], nsa_s[None], win_s[None])
```

```python
import math
from functools import partial

import jax, jax.numpy as jnp
from jax import lax
import numpy as np
from jax.experimental import pallas as pl
from jax.experimental.pallas import tpu as pltpu

D_MODEL = 1024
HEAD_DIM = 64
MIX_WIDTH = D_MODEL
SB_WIDTH = MIX_WIDTH // 2
NSA_WIDTH = MIX_WIDTH - SB_WIDTH
H_SB = SB_WIDTH // HEAD_DIM
H_NSA = NSA_WIDTH // HEAD_DIM
KV_GROUPS = 2
HPG = H_NSA // KV_GROUPS
COMP_BLOCK = 32
COMP_STRIDE = 16
COMP_HID = 128
SEL_BLOCK = 64
TOP_N = 8
WINDOW = 512
Q_BLOCK = 128
PEER_HEADS = 8
N_KEYS = 128
N_EXPERTS = N_KEYS * N_KEYS
PEER_TOPK = 16
D_KEY = 256
TOK_CHUNK = 256
DEPTH = 1
ALPHA = (2.0 * DEPTH) ** 0.25
LN_EPS = 1e-5
NEG = -1e30
FORCE = 1e4
SCALE = HEAD_DIM ** -0.5

OFF_SBKV = SB_WIDTH
OFF_QB = 3 * SB_WIDTH
OFF_NSAKV = OFF_QB + NSA_WIDTH
OFF_WIN = OFF_NSAKV + 4 * KV_GROUPS * HEAD_DIM
OFF_GATE = OFF_WIN + 2 * KV_GROUPS * HEAD_DIM
IN_COLS = OFF_GATE + 3 * H_NSA

LANE = 128
IN_COLS_PAD = -(-IN_COLS // LANE) * LANE
PROJ_ROWS = 512


def _proj_kernel(x_ref, w_ref, z_ref):
    z_ref[...] = jnp.dot(x_ref[...].astype(jnp.bfloat16), w_ref[...],
                         preferred_element_type=jnp.float32)


def project_pallas(x2d, w_in_pad_bf16):
    n = x2d.shape[0]
    rows = min(PROJ_ROWS, n)
    return pl.pallas_call(
        _proj_kernel,
        grid=(n // rows,),
        in_specs=[pl.BlockSpec((rows, D_MODEL), lambda i: (i, 0)),
                  pl.BlockSpec((D_MODEL, IN_COLS_PAD), lambda i: (0, 0))],
        out_specs=pl.BlockSpec((rows, IN_COLS_PAD), lambda i: (i, 0)),
        out_shape=jax.ShapeDtypeStruct((n, IN_COLS_PAD), jnp.float32),
        compiler_params=pltpu.CompilerParams(
            dimension_semantics=("arbitrary",), vmem_limit_bytes=48 * 1024 * 1024),
        name="in_proj",
    )(x2d, w_in_pad_bf16)


N_SEL = 32
N_CMP_PAD = 128
SEL_TILE = 512
WIN_TILE = 128
ALIBI = [[2.0 ** (-8.0 * (g * HPG + r + 1) / H_NSA) for r in range(HPG)] for g in range(KV_GROUPS)]


def _dot_nt(a, b):
    return lax.dot_general(a, b, (((1,), (1,)), ((), ())), preferred_element_type=jnp.float32)


def _split_dot(a, b_exact):
    hi = a.astype(jnp.bfloat16)
    r1 = a - hi.astype(jnp.float32)
    mid = r1.astype(jnp.bfloat16)
    lo = (r1 - mid.astype(jnp.float32)).astype(jnp.bfloat16)
    d = lambda x: jnp.dot(x, b_exact, preferred_element_type=jnp.float32)
    return d(hi) + d(mid) + d(lo)


def _nsa_prompt_kernel(q_ref, gz_ref, ck_ref, cv_ref, nsa_ref, win_ref, o_ref, m_ref, l_ref, acc_ref):
    QB = Q_BLOCK
    qb = pl.program_id(1)
    q0 = qb * QB
    bf = jnp.bfloat16
    f32 = jnp.float32
    gate = jax.nn.sigmoid(gz_ref[0])
    row = lax.broadcasted_iota(jnp.int32, (QB, 1), 0)
    qpos = q0 + row

    def reset():
        m_ref[...] = jnp.full(m_ref.shape, NEG, f32)
        l_ref[...] = jnp.zeros(l_ref.shape, f32)
        acc_ref[...] = jnp.zeros(acc_ref.shape, f32)

    def flash_step(r, s, valid, v):
        s = jnp.where(valid, s, NEG)
        m_old = m_ref[r]
        m_new = jnp.maximum(m_old, s.max(-1, keepdims=True))
        p = jnp.where(valid, jnp.exp(s - m_new), 0.0)
        alpha = jnp.exp(m_old - m_new)
        l_ref[r] = alpha * l_ref[r] + p.sum(-1, keepdims=True)
        acc_ref[r] = alpha * acc_ref[r] + jnp.dot(p.astype(bf), v, preferred_element_type=f32)
        m_ref[r] = m_new
        return p

    def finish(g, branch, first):
        for r in range(HPG):
            h = g * HPG + r
            tot = l_ref[r]
            o = acc_ref[r] / jnp.where(tot > 0, tot, 1.0)
            o = gate[:, 3 * h + branch:3 * h + branch + 1] * o
            sl = (0, slice(None), slice(h * HEAD_DIM, (h + 1) * HEAD_DIM))
            if first:
                o_ref[sl] = o
            else:
                o_ref[sl] = o_ref[sl] + o

    for g in range(KV_GROUPS):
        qh = [(q_ref[0, :, (g * HPG + r) * HEAD_DIM:(g * HPG + r + 1) * HEAD_DIM] * SCALE).astype(bf)
              for r in range(HPG)]

        reset()
        n_i = lax.broadcasted_iota(jnp.int32, (1, N_CMP_PAD), 1)
        cpos = n_i * COMP_STRIDE + (COMP_BLOCK - 1)
        n_cmp = (pl.num_programs(1) * QB - COMP_BLOCK) // COMP_STRIDE + 1
        valid_c = (cpos <= qpos) & (n_i < n_cmp)
        dist_c = (qpos - cpos).astype(f32)
        ckg = ck_ref[0, g].astype(bf)
        cvg = cv_ref[0, g].astype(bf)
        psum = jnp.zeros((QB, N_CMP_PAD), f32)
        for r in range(HPG):
            s = _dot_nt(qh[r], ckg) - ALIBI[g][r] * dist_c
            p = flash_step(r, s, valid_c, cvg)
            tot = l_ref[r]
            psum = psum + p / jnp.where(tot > 0, tot, 1.0)
        finish(g, 0, True)

        nn = lax.broadcasted_iota(jnp.int32, (N_CMP_PAD, N_SEL), 0)
        jj = lax.broadcasted_iota(jnp.int32, (N_CMP_PAD, N_SEL), 1)
        sel_map = ((nn * COMP_STRIDE) // SEL_BLOCK == jj).astype(bf)
        imp = _split_dot(psum, sel_map)
        blk = lax.broadcasted_iota(jnp.int32, (QB, N_SEL), 1)
        forced = (blk == qpos // SEL_BLOCK) | (blk == 0)
        imp = jnp.where(forced, FORCE, jnp.where(blk * SEL_BLOCK <= qpos, imp, -1.0))
        chosen = jnp.zeros((QB, N_SEL), f32)
        for _ in range(TOP_N):
            mx = imp.max(-1, keepdims=True)
            first_max = jnp.where(imp == mx, blk, N_SEL).min(-1, keepdims=True)
            pick = blk == first_max
            chosen = jnp.where(pick, 1.0, chosen)
            imp = jnp.where(pick, -2.0, imp)
        chosen = chosen.astype(bf)

        reset()

        def sel_body(kt, carry):
            k0 = pl.multiple_of(kt * SEL_TILE, SEL_TILE)
            k = nsa_ref[0, pl.ds(k0, SEL_TILE), (4 + g) * HEAD_DIM:(5 + g) * HEAD_DIM].astype(bf)
            v = nsa_ref[0, pl.ds(k0, SEL_TILE), (6 + g) * HEAD_DIM:(7 + g) * HEAD_DIM].astype(bf)
            col = lax.broadcasted_iota(jnp.int32, (1, SEL_TILE), 1) + k0
            dist = qpos - col
            ej = lax.broadcasted_iota(jnp.int32, (N_SEL, SEL_TILE), 0)
            ec = lax.broadcasted_iota(jnp.int32, (N_SEL, SEL_TILE), 1) + k0
            expand = (ec // SEL_BLOCK == ej).astype(bf)
            in_block = jnp.dot(chosen, expand, preferred_element_type=f32)
            valid = (in_block > 0.5) & (dist >= 0)
            distf = dist.astype(f32)
            for r in range(HPG):
                s = _dot_nt(qh[r], k) - ALIBI[g][r] * distf
                flash_step(r, s, valid, v)
            return carry

        lax.fori_loop(0, (q0 + QB + SEL_TILE - 1) // SEL_TILE, sel_body, 0)
        finish(g, 1, False)

        reset()

        def win_body(kt, carry):
            k0 = pl.multiple_of(kt * WIN_TILE, WIN_TILE)
            k = win_ref[0, pl.ds(k0, WIN_TILE), g * HEAD_DIM:(g + 1) * HEAD_DIM].astype(bf)
            v = win_ref[0, pl.ds(k0, WIN_TILE), (2 + g) * HEAD_DIM:(3 + g) * HEAD_DIM].astype(bf)
            col = lax.broadcasted_iota(jnp.int32, (1, WIN_TILE), 1) + k0
            dist = qpos - col
            valid = (dist >= 0) & (dist < WINDOW)
            distf = dist.astype(f32)
            for r in range(HPG):
                s = _dot_nt(qh[r], k) - ALIBI[g][r] * distf
                flash_step(r, s, valid, v)
            return carry

        lax.fori_loop(jnp.maximum(qb - WINDOW // WIN_TILE, 0), qb + 1, win_body, 0)
        finish(g, 2, False)


def nsa_prompt_pallas(q_b, gate_z, ck, cv, nsa_kv, win_kv):
    B, T, _ = q_b.shape
    assert T % Q_BLOCK == 0 and T // SEL_BLOCK <= N_SEL and (T - COMP_BLOCK) // COMP_STRIDE + 1 <= N_CMP_PAD
    per_b = lambda b, i: (b, 0, 0)
    return pl.pallas_call(
        _nsa_prompt_kernel,
        grid=(B, T // Q_BLOCK),
        in_specs=[pl.BlockSpec((1, Q_BLOCK, NSA_WIDTH), lambda b, i: (b, i, 0)),
                  pl.BlockSpec((1, Q_BLOCK, 3 * H_NSA), lambda b, i: (b, i, 0)),
                  pl.BlockSpec((1, KV_GROUPS, N_CMP_PAD, HEAD_DIM), lambda b, i: (b, 0, 0, 0)),
                  pl.BlockSpec((1, KV_GROUPS, N_CMP_PAD, HEAD_DIM), lambda b, i: (b, 0, 0, 0)),
                  pl.BlockSpec((1, T, 4 * KV_GROUPS * HEAD_DIM), per_b),
                  pl.BlockSpec((1, T, 2 * KV_GROUPS * HEAD_DIM), per_b)],
        out_specs=pl.BlockSpec((1, Q_BLOCK, NSA_WIDTH), lambda b, i: (b, i, 0)),
        out_shape=jax.ShapeDtypeStruct((B, T, NSA_WIDTH), jnp.float32),
        scratch_shapes=[pltpu.VMEM((HPG, Q_BLOCK, 1), jnp.float32),
                        pltpu.VMEM((HPG, Q_BLOCK, 1), jnp.float32),
                        pltpu.VMEM((HPG, Q_BLOCK, HEAD_DIM), jnp.float32)],
        compiler_params=pltpu.CompilerParams(
            dimension_semantics=("arbitrary", "arbitrary"), vmem_limit_bytes=48 * 1024 * 1024),
        name="nsa_prompt",
    )(q_b, gate_z, ck, cv, nsa_kv, win_kv)


N_PICK = PEER_HEADS * PEER_TOPK
PEER_TOK_BLOCK = 128


def _peer_expert_kernel(e_hbm, h_ref, coef_ref, u_hbm, v_hbm, g_ref, b_ref, y_ref,
                        e_smem, ubuf, vbuf, f_ref, sem_e, sem_rows):
    TB = h_ref.shape[0]
    i = pl.program_id(0)
    ids = pltpu.make_async_copy(e_hbm.at[pl.ds(i * TB, TB), :], e_smem, sem_e)
    ids.start()
    ids.wait()

    def row_copy(tab, buf, which, slot, e, k):
        return pltpu.make_async_copy(tab.at[pl.ds(e, 1), :], buf.at[slot, pl.ds(k, 1), :], sem_rows.at[which, slot])

    def issue(t, slot):
        for k in range(N_PICK):
            e = e_smem[t, k]
            row_copy(u_hbm, ubuf, 0, slot, e, k).start()
            row_copy(v_hbm, vbuf, 1, slot, e, k).start()

    def wait_rows(slot):
        pltpu.make_async_copy(u_hbm.at[pl.ds(0, N_PICK), :], ubuf.at[slot], sem_rows.at[0, slot]).wait()
        pltpu.make_async_copy(v_hbm.at[pl.ds(0, N_PICK), :], vbuf.at[slot], sem_rows.at[1, slot]).wait()

    coef_t = coef_ref[...].T
    tok = lax.broadcasted_iota(jnp.int32, coef_t.shape, 1)
    issue(0, 0)

    def body(t, carry):
        slot = t % 2

        @pl.when(t + 1 < TB)
        def _():
            issue(t + 1, 1 - slot)

        wait_rows(slot)
        a = jnp.sum(ubuf[slot] * h_ref[pl.ds(t, 1), :], axis=-1, keepdims=True)
        c = jnp.sum(jnp.where(tok == t, coef_t, 0.0), axis=-1, keepdims=True)
        f_ref[pl.ds(t, 1), :] = jnp.sum(vbuf[slot] * (c * jax.nn.gelu(a)), axis=0, keepdims=True)
        return carry

    lax.fori_loop(0, TB, body, 0)
    x = ALPHA * h_ref[...] + f_ref[...]
    mu = x.mean(-1, keepdims=True)
    var = jnp.square(x - mu).mean(-1, keepdims=True)
    y_ref[...] = (x - mu) * lax.rsqrt(var + LN_EPS) * g_ref[...] + b_ref[...]


def peer_expert_pallas(e, h, coef, u_tab, v_tab, ln_g, ln_b):
    n = h.shape[0]
    tb = min(PEER_TOK_BLOCK, n)
    assert n % tb == 0
    row = lambda i: (i, 0)
    fixed = lambda i: (0, 0)
    return pl.pallas_call(
        _peer_expert_kernel,
        grid=(n // tb,),
        in_specs=[pl.BlockSpec(memory_space=pl.ANY),
                  pl.BlockSpec((tb, D_MODEL), row),
                  pl.BlockSpec((tb, N_PICK), row),
                  pl.BlockSpec(memory_space=pl.ANY),
                  pl.BlockSpec(memory_space=pl.ANY),
                  pl.BlockSpec((1, D_MODEL), fixed),
                  pl.BlockSpec((1, D_MODEL), fixed)],
        out_specs=pl.BlockSpec((tb, D_MODEL), row),
        out_shape=jax.ShapeDtypeStruct((n, D_MODEL), jnp.float32),
        scratch_shapes=[pltpu.SMEM((tb, N_PICK), jnp.int32),
                        pltpu.VMEM((2, N_PICK, D_MODEL), jnp.float32),
                        pltpu.VMEM((2, N_PICK, D_MODEL), jnp.float32),
                        pltpu.VMEM((tb, D_MODEL), jnp.float32),
                        pltpu.SemaphoreType.DMA,
                        pltpu.SemaphoreType.DMA((2, 2))],
        compiler_params=pltpu.CompilerParams(dimension_semantics=("arbitrary",)),
        name="peer_experts",
    )(e, h, coef, u_tab, v_tab, ln_g.reshape(1, D_MODEL), ln_b.reshape(1, D_MODEL))


def layer_norm(x, g, b):
    xf = x.astype(jnp.float32)
    mu = xf.mean(-1, keepdims=True)
    var = jnp.square(xf - mu).mean(-1, keepdims=True)
    return ((xf - mu) * lax.rsqrt(var + LN_EPS) * g + b).astype(x.dtype)


def rms_norm(x, g):
    xf = x.astype(jnp.float32)
    return (xf * lax.rsqrt(jnp.square(xf).mean(-1, keepdims=True) + LN_EPS) * g).astype(x.dtype)


def masked_softmax(s, mask):
    s = jnp.where(mask, s, NEG)
    e = jnp.exp(s - s.max(-1, keepdims=True)) * mask
    tot = e.sum(-1, keepdims=True)
    return e / jnp.where(tot > 0, tot, 1.0)


def alibi_slopes():
    i = jnp.arange(1, H_NSA + 1, dtype=jnp.float32)
    return (2.0 ** (-8.0 * i / H_NSA)).reshape(KV_GROUPS, HPG)


def project(x, w_in_pad):
    B, T, _ = x.shape
    z = project_pallas(x.reshape(B * T, D_MODEL), w_in_pad).reshape(B, T, IN_COLS_PAD)
    q_a = z[..., :SB_WIDTH].reshape(B, T, H_SB, HEAD_DIM)
    sb_kv = z[..., OFF_SBKV:OFF_QB].reshape(B, T, 2, H_SB, HEAD_DIM)
    q_b = z[..., OFF_QB:OFF_NSAKV].reshape(B, T, H_NSA, HEAD_DIM)
    nsa_kv = z[..., OFF_NSAKV:OFF_WIN].reshape(B, T, 4, KV_GROUPS, HEAD_DIM)
    win_kv = z[..., OFF_WIN:OFF_GATE].reshape(B, T, 2, KV_GROUPS, HEAD_DIM)
    gates = jax.nn.sigmoid(z[..., OFF_GATE:IN_COLS]).reshape(B, T, H_NSA, 3)
    return q_a, sb_kv, q_b, nsa_kv, win_kv, gates


def stick_breaking(q, q_pos, k, v, k_pos):
    z = jnp.einsum('bthd,bshd->bhts', q, k).astype(jnp.float32) * SCALE
    mask = k_pos[None, :] < q_pos[:, None]
    log_keep = jnp.where(mask, jax.nn.log_sigmoid(-z), 0.0)
    cum = jnp.cumsum(log_keep, axis=-1)
    after = cum[..., -1:] - cum
    w = jnp.where(mask, jnp.exp(jax.nn.log_sigmoid(z) + after), 0.0)
    return jnp.einsum('bhts,bshd->bthd', w.astype(v.dtype), v)


def compress(kc, vc, cmp_pe, cmp_w1, cmp_w2):
    L = kc.shape[1]
    n_c = (L - COMP_BLOCK) // COMP_STRIDE + 1
    starts = jnp.arange(n_c) * COMP_STRIDE
    idx = starts[:, None] + jnp.arange(COMP_BLOCK)[None, :]

    def phi(t, j):
        blk = t[:, idx] + cmp_pe[j][None, None, :, None, :]
        hdn = jax.nn.gelu(jnp.einsum('bnlgd,ldh->bngh', blk, cmp_w1[j]))
        return jnp.einsum('bngh,hd->bngd', hdn, cmp_w2[j])

    ck = phi(kc, 0)
    cv = phi(vc, 1)
    c_pos = starts + COMP_BLOCK - 1
    n_sel = -(-L // SEL_BLOCK)
    sel_map = jax.nn.one_hot(starts // SEL_BLOCK, n_sel, dtype=jnp.float32)
    return ck, cv, c_pos, sel_map


def sel_prep(k, n_sel):
    pad = n_sel * SEL_BLOCK - k.shape[1]
    return jnp.pad(k, ((0, 0), (0, pad), (0, 0), (0, 0))).transpose(0, 2, 1, 3)


def gather_rows(a, idx):
    return jax.vmap(jax.vmap(lambda r, i: r[i]))(a, idx)


def nsa_attend(q, q_pos, gates, ck, cv, c_pos, sel_map, sk, sv, wk, wv, w_pos):
    B, Tq = q.shape[:2]
    qg = q.reshape(B, Tq, KV_GROUPS, HPG, HEAD_DIM)
    slopes = alibi_slopes()[None, :, :, None, None]
    qp = q_pos.astype(jnp.float32)
    s = jnp.einsum('btgrd,bngd->bgrtn', qg, ck).astype(jnp.float32) * SCALE
    s = s - slopes * (qp[:, None] - c_pos[None, :].astype(jnp.float32))
    p_c = masked_softmax(s, c_pos[None, :] <= q_pos[:, None])
    o_c = jnp.einsum('bgrtn,bngd->btgrd', p_c.astype(cv.dtype), cv)
    imp = jnp.einsum('bgrtn,nj->bgtj', p_c, sel_map)
    n_sel = sel_map.shape[1]
    blk = jnp.arange(n_sel)
    valid = blk[None, :] * SEL_BLOCK <= q_pos[:, None]
    forced = (blk[None, :] == (q_pos // SEL_BLOCK)[:, None]) | (blk[None, :] == 0)
    imp = jnp.where(forced, FORCE, jnp.where(valid, imp, -1.0))
    n_top = min(TOP_N, n_sel)
    _, idx = lax.top_k(imp, n_top)
    kpos = (idx[..., None] * SEL_BLOCK + jnp.arange(SEL_BLOCK)).reshape(B, KV_GROUPS, Tq, n_top * SEL_BLOCK)
    flat = kpos.reshape(B, KV_GROUPS, -1)
    ksel = gather_rows(sk, flat).reshape(B, KV_GROUPS, Tq, n_top * SEL_BLOCK, HEAD_DIM)
    vsel = gather_rows(sv, flat).reshape(B, KV_GROUPS, Tq, n_top * SEL_BLOCK, HEAD_DIM)
    dist_s = (q_pos[None, None, :, None] - kpos)[:, :, None]
    s = jnp.einsum('btgrd,bgtsd->bgrts', qg, ksel).astype(jnp.float32) * SCALE
    s = s - slopes * dist_s.astype(jnp.float32)
    p_s = masked_softmax(s, dist_s >= 0)
    o_s = jnp.einsum('bgrts,bgtsd->btgrd', p_s.astype(vsel.dtype), vsel)
    dist_w = q_pos[:, None] - w_pos[None, :]
    mask_w = (dist_w >= 0) & (dist_w < WINDOW) & (w_pos[None, :] >= 0)
    s = jnp.einsum('btgrd,bsgd->bgrts', qg, wk).astype(jnp.float32) * SCALE
    s = s - slopes * dist_w.astype(jnp.float32)
    p_w = masked_softmax(s, mask_w)
    o_w = jnp.einsum('bgrts,bsgd->btgrd', p_w.astype(wv.dtype), wv)
    g = gates.reshape(B, Tq, KV_GROUPS, HPG, 3)
    o = g[..., 0:1] * o_c + g[..., 1:2] * o_s + g[..., 2:3] * o_w
    return o.reshape(B, Tq, NSA_WIDTH)


def mixer_prompt(x, w_in_pad, cmp_pe, cmp_w1, cmp_w2):
    B, T, _ = x.shape
    z = project_pallas(x.reshape(B * T, D_MODEL), w_in_pad).reshape(B, T, IN_COLS_PAD)
    q_a = z[..., :SB_WIDTH].reshape(B, T, H_SB, HEAD_DIM)
    sb_kv = z[..., OFF_SBKV:OFF_QB].reshape(B, T, 2, H_SB, HEAD_DIM)
    nsa_kv = z[..., OFF_NSAKV:OFF_WIN].reshape(B, T, 4, KV_GROUPS, HEAD_DIM)
    win_kv = z[..., OFF_WIN:OFF_GATE].reshape(B, T, 2, KV_GROUPS, HEAD_DIM)
    k_a, v_a = sb_kv[:, :, 0], sb_kv[:, :, 1]
    ck, cv, c_pos, sel_map = compress(nsa_kv[:, :, 0], nsa_kv[:, :, 1], cmp_pe, cmp_w1, cmp_w2)
    pad_c = lambda c: jnp.pad(c, ((0, 0), (0, N_CMP_PAD - c.shape[1]), (0, 0), (0, 0))).transpose(0, 2, 1, 3)
    o_b = nsa_prompt_pallas(z[..., OFF_QB:OFF_NSAKV], z[..., OFF_GATE:IN_COLS], pad_c(ck), pad_c(cv),
                            z[..., OFF_NSAKV:OFF_WIN], z[..., OFF_WIN:OFF_GATE])
    k_pos = jnp.arange(T)

    def block(qb):
        q0 = qb * Q_BLOCK
        q_pos = q0 + jnp.arange(Q_BLOCK)
        sl = lambda a: lax.dynamic_slice_in_dim(a, q0, Q_BLOCK, axis=1)
        return stick_breaking(sl(q_a), q_pos, k_a, v_a, k_pos).reshape(B, Q_BLOCK, SB_WIDTH)

    o_a = lax.map(block, jnp.arange(T // Q_BLOCK))
    o_a = jnp.moveaxis(o_a, 0, 1).reshape(B, T, SB_WIDTH)
    new_win = win_kv[:, T - min(WINDOW, T):]
    return o_a, o_b, sb_kv, nsa_kv, new_win


def mixer_sample(x, cache_sb, cache_nsa, cache_win, page_table, w_in_pad, cmp_pe, cmp_w1, cmp_w2):
    B, T, _ = x.shape
    past = page_table.shape[1] * cache_sb.shape[1]
    q_a, sb_kv, q_b, nsa_kv, win_kv, gates = project(x, w_in_pad)
    q_pos = past + jnp.arange(T)
    k_pos = jnp.arange(past + T)
    sb_all = jnp.concatenate([cache_sb[page_table].reshape(B, past, 2, H_SB, HEAD_DIM), sb_kv], axis=1)
    o_a = stick_breaking(q_a, q_pos, sb_all[:, :, 0], sb_all[:, :, 1], k_pos).reshape(B, T, SB_WIDTH)
    nsa_all = jnp.concatenate([cache_nsa[page_table].reshape(B, past, 4, KV_GROUPS, HEAD_DIM), nsa_kv], axis=1)
    ck, cv, c_pos, sel_map = compress(nsa_all[:, :, 0], nsa_all[:, :, 1], cmp_pe, cmp_w1, cmp_w2)
    n_sel = sel_map.shape[1]
    sk = sel_prep(nsa_all[:, :, 2], n_sel)
    sv = sel_prep(nsa_all[:, :, 3], n_sel)
    wb = cache_win.shape[1]
    win_all = jnp.concatenate([cache_win, win_kv], axis=1)
    w_pos = past - wb + jnp.arange(wb + T)
    o_b = nsa_attend(q_b, q_pos, gates, ck, cv, c_pos, sel_map, sk, sv,
                     win_all[:, :, 0], win_all[:, :, 1], w_pos)
    new_win = win_all[:, T:]
    return o_a, o_b, sb_kv, nsa_kv, new_win


def peer_route_chunk(h, w_q, sub_keys):
    C = h.shape[0]
    q = (h @ w_q).reshape(C, PEER_HEADS, 2, D_KEY // 2)
    s = jnp.einsum('cphd,phkd->cphk', q, sub_keys).astype(jnp.float32)
    s1, i1 = lax.top_k(s[:, :, 0], PEER_TOPK)
    s2, i2 = lax.top_k(s[:, :, 1], PEER_TOPK)
    cand = (s1[..., :, None] + s2[..., None, :]).reshape(C, PEER_HEADS, PEER_TOPK * PEER_TOPK)
    cidx = (i1[..., :, None] * N_KEYS + i2[..., None, :]).reshape(C, PEER_HEADS, PEER_TOPK * PEER_TOPK)
    top_s, top_i = lax.top_k(cand, PEER_TOPK)
    e = jnp.take_along_axis(cidx, top_i, axis=-1)
    g = jax.nn.softmax(top_s, axis=-1)
    return e.reshape(C, N_PICK).astype(jnp.int32), g.reshape(C, N_PICK)


def peer_route(h, w_q, sub_keys):
    n = h.shape[0]
    e, g = lax.map(lambda hc: peer_route_chunk(hc, w_q, sub_keys), h.reshape(n // TOK_CHUNK, TOK_CHUNK, D_MODEL))
    return e.reshape(n, N_PICK), g.reshape(n, N_PICK)


def block_out(x, o_a, o_b, mix_g, w_out, ln1_g, ln1_b, w_q, sub_keys, u_tab, v_tab, ln2_g, ln2_b):
    m = jnp.concatenate([rms_norm(o_a, mix_g[:SB_WIDTH]), rms_norm(o_b, mix_g[SB_WIDTH:])], axis=-1) @ w_out
    h = layer_norm(ALPHA * x + m, ln1_g, ln1_b).reshape(-1, D_MODEL)
    e, g = peer_route(h, w_q, sub_keys)
    return peer_expert_pallas(e, h, g, u_tab, v_tab, ln2_g, ln2_b).reshape(x.shape)


def kernel(x_prompt, x_sample, cache_sb_kv, cache_nsa_kv, cache_win_kv, page_table, w_in, cmp_pe, cmp_w1, cmp_w2, mix_norm_g, w_out, ln1_g, ln1_b, peer_w_q, peer_sub_keys, peer_u, peer_v, ln2_g, ln2_b):
    l = 0
    w_in_pad = jnp.pad(w_in[l], ((0, 0), (0, IN_COLS_PAD - IN_COLS))).astype(jnp.bfloat16)
    o_a, o_b, sb_p, nsa_p, win_p = mixer_prompt(x_prompt, w_in_pad, cmp_pe[l], cmp_w1[l], cmp_w2[l])
    h_p = block_out(x_prompt, o_a, o_b, mix_norm_g[l], w_out[l], ln1_g[l], ln1_b[l],
                    peer_w_q[l], peer_sub_keys[l], peer_u[l], peer_v[l], ln2_g[l], ln2_b[l])
    o_a, o_b, sb_s, nsa_s, win_s = mixer_sample(x_sample, cache_sb_kv[l], cache_nsa_kv[l], cache_win_kv[l],
                                                page_table, w_in_pad, cmp_pe[l], cmp_w1[l], cmp_w2[l])
    h_s = block_out(x_sample, o_a, o_b, mix_norm_g[l], w_out[l], ln1_g[l], ln1_b[l],
                    peer_w_q[l], peer_sub_keys[l], peer_u[l], peer_v[l], ln2_g[l], ln2_b[l])
    return (h_p, h_s, sb_p[None], nsa_p[None], win_p[None], sb_s[None], nsa_s[None], win_s[None])
```

```python
import math
from functools import partial

import jax, jax.numpy as jnp
from jax import lax
import numpy as np
from jax.experimental import pallas as pl
from jax.experimental.pallas import tpu as pltpu

D_MODEL = 1024
HEAD_DIM = 64
MIX_WIDTH = D_MODEL
SB_WIDTH = MIX_WIDTH // 2
NSA_WIDTH = MIX_WIDTH - SB_WIDTH
H_SB = SB_WIDTH // HEAD_DIM
H_NSA = NSA_WIDTH // HEAD_DIM
KV_GROUPS = 2
HPG = H_NSA // KV_GROUPS
COMP_BLOCK = 32
COMP_STRIDE = 16
COMP_HID = 128
SEL_BLOCK = 64
TOP_N = 8
WINDOW = 512
Q_BLOCK = 128
PEER_HEADS = 8
N_KEYS = 128
N_EXPERTS = N_KEYS * N_KEYS
PEER_TOPK = 16
D_KEY = 256
TOK_CHUNK = 256
DEPTH = 1
ALPHA = (2.0 * DEPTH) ** 0.25
LN_EPS = 1e-5
NEG = -1e30
FORCE = 1e4
SCALE = HEAD_DIM ** -0.5

OFF_SBKV = SB_WIDTH
OFF_QB = 3 * SB_WIDTH
OFF_NSAKV = OFF_QB + NSA_WIDTH
OFF_WIN = OFF_NSAKV + 4 * KV_GROUPS * HEAD_DIM
OFF_GATE = OFF_WIN + 2 * KV_GROUPS * HEAD_DIM
IN_COLS = OFF_GATE + 3 * H_NSA

LANE = 128
IN_COLS_PAD = -(-IN_COLS // LANE) * LANE
PROJ_ROWS = 512


def _proj_kernel(x_ref, w_ref, z_ref):
    z_ref[...] = jnp.dot(x_ref[...].astype(jnp.bfloat16), w_ref[...],
                         preferred_element_type=jnp.float32)


def project_pallas(x2d, w_in_pad_bf16):
    n = x2d.shape[0]
    rows = min(PROJ_ROWS, n)
    return pl.pallas_call(
        _proj_kernel,
        grid=(n // rows,),
        in_specs=[pl.BlockSpec((rows, D_MODEL), lambda i: (i, 0)),
                  pl.BlockSpec((D_MODEL, IN_COLS_PAD), lambda i: (0, 0))],
        out_specs=pl.BlockSpec((rows, IN_COLS_PAD), lambda i: (i, 0)),
        out_shape=jax.ShapeDtypeStruct((n, IN_COLS_PAD), jnp.float32),
        compiler_params=pltpu.CompilerParams(
            dimension_semantics=("arbitrary",), vmem_limit_bytes=48 * 1024 * 1024),
        name="in_proj",
    )(x2d, w_in_pad_bf16)


N_SEL = 32
N_CMP_PAD = 128
SEL_TILE = 512
WIN_TILE = 128
ALIBI = [[2.0 ** (-8.0 * (g * HPG + r + 1) / H_NSA) for r in range(HPG)] for g in range(KV_GROUPS)]


def _dot_nt(a, b):
    return lax.dot_general(a, b, (((1,), (1,)), ((), ())), preferred_element_type=jnp.float32)


def _split_dot(a, b_exact):
    hi = a.astype(jnp.bfloat16)
    r1 = a - hi.astype(jnp.float32)
    mid = r1.astype(jnp.bfloat16)
    lo = (r1 - mid.astype(jnp.float32)).astype(jnp.bfloat16)
    d = lambda x: jnp.dot(x, b_exact, preferred_element_type=jnp.float32)
    return d(hi) + d(mid) + d(lo)


def _nsa_prompt_kernel(q_ref, gz_ref, ck_ref, cv_ref, nsa_ref, win_ref, o_ref, m_ref, l_ref, acc_ref):
    QB = Q_BLOCK
    qb = pl.program_id(1)
    q0 = qb * QB
    bf = jnp.bfloat16
    f32 = jnp.float32
    gate = jax.nn.sigmoid(gz_ref[0])
    row = lax.broadcasted_iota(jnp.int32, (QB, 1), 0)
    qpos = q0 + row

    def reset():
        m_ref[...] = jnp.full(m_ref.shape, NEG, f32)
        l_ref[...] = jnp.zeros(l_ref.shape, f32)
        acc_ref[...] = jnp.zeros(acc_ref.shape, f32)

    def flash_step(r, s, valid, v):
        s = jnp.where(valid, s, NEG)
        m_old = m_ref[r]
        m_new = jnp.maximum(m_old, s.max(-1, keepdims=True))
        p = jnp.where(valid, jnp.exp(s - m_new), 0.0)
        alpha = jnp.exp(m_old - m_new)
        l_ref[r] = alpha * l_ref[r] + p.sum(-1, keepdims=True)
        acc_ref[r] = alpha * acc_ref[r] + jnp.dot(p.astype(bf), v, preferred_element_type=f32)
        m_ref[r] = m_new
        return p

    def finish(g, branch, first):
        for r in range(HPG):
            h = g * HPG + r
            tot = l_ref[r]
            o = acc_ref[r] / jnp.where(tot > 0, tot, 1.0)
            o = gate[:, 3 * h + branch:3 * h + branch + 1] * o
            sl = (0, slice(None), slice(h * HEAD_DIM, (h + 1) * HEAD_DIM))
            if first:
                o_ref[sl] = o
            else:
                o_ref[sl] = o_ref[sl] + o

    for g in range(KV_GROUPS):
        qh = [(q_ref[0, :, (g * HPG + r) * HEAD_DIM:(g * HPG + r + 1) * HEAD_DIM] * SCALE).astype(bf)
              for r in range(HPG)]

        reset()
        n_i = lax.broadcasted_iota(jnp.int32, (1, N_CMP_PAD), 1)
        cpos = n_i * COMP_STRIDE + (COMP_BLOCK - 1)
        n_cmp = (pl.num_programs(1) * QB - COMP_BLOCK) // COMP_STRIDE + 1
        valid_c = (cpos <= qpos) & (n_i < n_cmp)
        dist_c = (qpos - cpos).astype(f32)
        ckg = ck_ref[0, g].astype(bf)
        cvg = cv_ref[0, g].astype(bf)
        psum = jnp.zeros((QB, N_CMP_PAD), f32)
        for r in range(HPG):
            s = _dot_nt(qh[r], ckg) - ALIBI[g][r] * dist_c
            p = flash_step(r, s, valid_c, cvg)
            tot = l_ref[r]
            psum = psum + p / jnp.where(tot > 0, tot, 1.0)
        finish(g, 0, True)

        nn = lax.broadcasted_iota(jnp.int32, (N_CMP_PAD, N_SEL), 0)
        jj = lax.broadcasted_iota(jnp.int32, (N_CMP_PAD, N_SEL), 1)
        sel_map = ((nn * COMP_STRIDE) // SEL_BLOCK == jj).astype(bf)
        imp = _split_dot(psum, sel_map)
        blk = lax.broadcasted_iota(jnp.int32, (QB, N_SEL), 1)
        forced = (blk == qpos // SEL_BLOCK) | (blk == 0)
        imp = jnp.where(forced, FORCE, jnp.where(blk * SEL_BLOCK <= qpos, imp, -1.0))
        chosen = jnp.zeros((QB, N_SEL), f32)
        for _ in range(TOP_N):
            mx = imp.max(-1, keepdims=True)
            first_max = jnp.where(imp == mx, blk, N_SEL).min(-1, keepdims=True)
            pick = blk == first_max
            chosen = jnp.where(pick, 1.0, chosen)
            imp = jnp.where(pick, -2.0, imp)
        chosen = chosen.astype(bf)

        reset()

        def sel_body(kt, carry):
            k0 = pl.multiple_of(kt * SEL_TILE, SEL_TILE)
            k = nsa_ref[0, pl.ds(k0, SEL_TILE), (4 + g) * HEAD_DIM:(5 + g) * HEAD_DIM].astype(bf)
            v = nsa_ref[0, pl.ds(k0, SEL_TILE), (6 + g) * HEAD_DIM:(7 + g) * HEAD_DIM].astype(bf)
            col = lax.broadcasted_iota(jnp.int32, (1, SEL_TILE), 1) + k0
            dist = qpos - col
            ej = lax.broadcasted_iota(jnp.int32, (N_SEL, SEL_TILE), 0)
            ec = lax.broadcasted_iota(jnp.int32, (N_SEL, SEL_TILE), 1) + k0
            expand = (ec // SEL_BLOCK == ej).astype(bf)
            in_block = jnp.dot(chosen, expand, preferred_element_type=f32)
            valid = (in_block > 0.5) & (dist >= 0)
            distf = dist.astype(f32)
            for r in range(HPG):
                s = _dot_nt(qh[r], k) - ALIBI[g][r] * distf
                flash_step(r, s, valid, v)
            return carry

        lax.fori_loop(0, (q0 + QB + SEL_TILE - 1) // SEL_TILE, sel_body, 0)
        finish(g, 1, False)

        reset()

        def win_body(kt, carry):
            k0 = pl.multiple_of(kt * WIN_TILE, WIN_TILE)
            k = win_ref[0, pl.ds(k0, WIN_TILE), g * HEAD_DIM:(g + 1) * HEAD_DIM].astype(bf)
            v = win_ref[0, pl.ds(k0, WIN_TILE), (2 + g) * HEAD_DIM:(3 + g) * HEAD_DIM].astype(bf)
            col = lax.broadcasted_iota(jnp.int32, (1, WIN_TILE), 1) + k0
            dist = qpos - col
            valid = (dist >= 0) & (dist < WINDOW)
            distf = dist.astype(f32)
            for r in range(HPG):
                s = _dot_nt(qh[r], k) - ALIBI[g][r] * distf
                flash_step(r, s, valid, v)
            return carry

        lax.fori_loop(jnp.maximum(qb - WINDOW // WIN_TILE, 0), qb + 1, win_body, 0)
        finish(g, 2, False)


def nsa_prompt_pallas(q_b, gate_z, ck, cv, nsa_kv, win_kv):
    B, T, _ = q_b.shape
    assert T % Q_BLOCK == 0 and T // SEL_BLOCK <= N_SEL and (T - COMP_BLOCK) // COMP_STRIDE + 1 <= N_CMP_PAD
    per_b = lambda b, i: (b, 0, 0)
    return pl.pallas_call(
        _nsa_prompt_kernel,
        grid=(B, T // Q_BLOCK),
        in_specs=[pl.BlockSpec((1, Q_BLOCK, NSA_WIDTH), lambda b, i: (b, i, 0)),
                  pl.BlockSpec((1, Q_BLOCK, 3 * H_NSA), lambda b, i: (b, i, 0)),
                  pl.BlockSpec((1, KV_GROUPS, N_CMP_PAD, HEAD_DIM), lambda b, i: (b, 0, 0, 0)),
                  pl.BlockSpec((1, KV_GROUPS, N_CMP_PAD, HEAD_DIM), lambda b, i: (b, 0, 0, 0)),
                  pl.BlockSpec((1, T, 4 * KV_GROUPS * HEAD_DIM), per_b),
                  pl.BlockSpec((1, T, 2 * KV_GROUPS * HEAD_DIM), per_b)],
        out_specs=pl.BlockSpec((1, Q_BLOCK, NSA_WIDTH), lambda b, i: (b, i, 0)),
        out_shape=jax.ShapeDtypeStruct((B, T, NSA_WIDTH), jnp.float32),
        scratch_shapes=[pltpu.VMEM((HPG, Q_BLOCK, 1), jnp.float32),
                        pltpu.VMEM((HPG, Q_BLOCK, 1), jnp.float32),
                        pltpu.VMEM((HPG, Q_BLOCK, HEAD_DIM), jnp.float32)],
        compiler_params=pltpu.CompilerParams(
            dimension_semantics=("arbitrary", "arbitrary"), vmem_limit_bytes=48 * 1024 * 1024),
        name="nsa_prompt",
    )(q_b, gate_z, ck, cv, nsa_kv, win_kv)


SB_TILE = 128
N_PICK = PEER_HEADS * PEER_TOPK


def _split2_dot(a, b_exact):
    hi = a.astype(jnp.bfloat16)
    lo = (a - hi.astype(jnp.float32)).astype(jnp.bfloat16)
    return (jnp.dot(hi, b_exact, preferred_element_type=jnp.float32)
            + jnp.dot(lo, b_exact, preferred_element_type=jnp.float32))


def _sb_prompt_kernel(q_ref, kv_ref, o_ref):
    TQ = TK = SB_TILE
    bf, f32 = jnp.bfloat16, jnp.float32
    qb = pl.program_id(1)
    qpos = qb * TQ + lax.broadcasted_iota(jnp.int32, (TQ, 1), 0)
    later = (lax.broadcasted_iota(jnp.int32, (TK, TK), 0) > lax.broadcasted_iota(jnp.int32, (TK, TK), 1)).astype(bf)
    for h in range(H_SB):
        q = (q_ref[0, :, h * HEAD_DIM:(h + 1) * HEAD_DIM] * SCALE).astype(bf)

        def body(i, c):
            carry, acc = c
            k0 = pl.multiple_of((qb - i) * TK, TK)
            k = kv_ref[0, pl.ds(k0, TK), h * HEAD_DIM:(h + 1) * HEAD_DIM].astype(bf)
            v = kv_ref[0, pl.ds(k0, TK), SB_WIDTH + h * HEAD_DIM:SB_WIDTH + (h + 1) * HEAD_DIM].astype(bf)
            z = _dot_nt(q, k)
            valid = (lax.broadcasted_iota(jnp.int32, (1, TK), 1) + k0) < qpos
            soft = jnp.log1p(jnp.exp(-jnp.abs(z)))
            log_sig = jnp.minimum(z, 0.0) - soft
            log_keep = jnp.where(valid, log_sig - z, 0.0)
            after = _split2_dot(log_keep, later) + carry
            w = jnp.where(valid, jnp.exp(log_sig + after), 0.0)
            acc = acc + jnp.dot(w.astype(bf), v, preferred_element_type=f32)
            return carry + log_keep.sum(-1, keepdims=True), acc

        _, acc = lax.fori_loop(0, qb + 1, body, (jnp.zeros((TQ, 1), f32), jnp.zeros((TQ, HEAD_DIM), f32)))
        o_ref[0, :, h * HEAD_DIM:(h + 1) * HEAD_DIM] = acc


def sb_prompt_pallas(q_a, sb_kv):
    B, T, _ = q_a.shape
    assert T % SB_TILE == 0
    return pl.pallas_call(
        _sb_prompt_kernel,
        grid=(B, T // SB_TILE),
        in_specs=[pl.BlockSpec((1, SB_TILE, SB_WIDTH), lambda b, i: (b, i, 0)),
                  pl.BlockSpec((1, T, 2 * SB_WIDTH), lambda b, i: (b, 0, 0))],
        out_specs=pl.BlockSpec((1, SB_TILE, SB_WIDTH), lambda b, i: (b, i, 0)),
        out_shape=jax.ShapeDtypeStruct((B, T, SB_WIDTH), jnp.float32),
        compiler_params=pltpu.CompilerParams(
            dimension_semantics=("arbitrary", "arbitrary"), vmem_limit_bytes=48 * 1024 * 1024),
        name="sb_prompt",
    )(q_a, sb_kv)


ROUTE_TOK = 256
N_CAND = PEER_TOPK * PEER_TOPK


def _top_rows(s, n_top, vals_ref, idx_ref):
    rows = lax.broadcasted_iota(jnp.int32, s.shape, 0)
    for a in range(n_top):
        m = s.max(0, keepdims=True)
        ix = jnp.where(s == m, rows, s.shape[0]).min(0, keepdims=True)
        s = jnp.where(rows == ix, -jnp.inf, s)
        vals_ref[a:a + 1, :] = m
        idx_ref[a:a + 1, :] = ix


def _peer_route_kernel(h_ref, wq_t_ref, keys_ref, e_ref, g_ref, s1_ref, i1_ref, s2_ref, i2_ref, ts_ref, ti_ref):
    bf, f32 = jnp.bfloat16, jnp.float32
    q_t = _dot_nt(wq_t_ref[...], h_ref[...].astype(bf))
    half = D_KEY // 2
    for p in range(PEER_HEADS):
        for hf, (sv, si) in enumerate(((s1_ref, i1_ref), (s2_ref, i2_ref))):
            r0 = (2 * p + hf) * half
            s = jnp.dot(keys_ref[2 * p + hf], q_t[r0:r0 + half, :].astype(bf), preferred_element_type=f32)
            _top_rows(s, PEER_TOPK, sv, si)
        s2 = s2_ref[...]
        cand = jnp.concatenate([s1_ref[a:a + 1, :] + s2 for a in range(PEER_TOPK)], axis=0)
        _top_rows(cand, PEER_TOPK, ts_ref, ti_ref)
        ti = ti_ref[...]
        a_of, b_of = ti // PEER_TOPK, ti % PEER_TOPK
        k1 = jnp.zeros(ti.shape, jnp.int32)
        k2 = jnp.zeros(ti.shape, jnp.int32)
        for a in range(PEER_TOPK):
            k1 = jnp.where(a_of == a, i1_ref[a:a + 1, :], k1)
            k2 = jnp.where(b_of == a, i2_ref[a:a + 1, :], k2)
        ts = ts_ref[...]
        ex = jnp.exp(ts - ts.max(0, keepdims=True))
        e_ref[p * PEER_TOPK:(p + 1) * PEER_TOPK, :] = k1 * N_KEYS + k2
        g_ref[p * PEER_TOPK:(p + 1) * PEER_TOPK, :] = ex / ex.sum(0, keepdims=True)


def peer_route_pallas(h, wq_t, keys):
    n = h.shape[0]
    assert n % ROUTE_TOK == 0
    tn = ROUTE_TOK
    top = lambda dt: pltpu.VMEM((PEER_TOPK, tn), dt)
    return pl.pallas_call(
        _peer_route_kernel,
        grid=(n // tn,),
        in_specs=[pl.BlockSpec((tn, D_MODEL), lambda i: (i, 0)),
                  pl.BlockSpec((PEER_HEADS * D_KEY, D_MODEL), lambda i: (0, 0)),
                  pl.BlockSpec((2 * PEER_HEADS, N_KEYS, D_KEY // 2), lambda i: (0, 0, 0))],
        out_specs=[pl.BlockSpec((N_PICK, tn), lambda i: (0, i)),
                   pl.BlockSpec((N_PICK, tn), lambda i: (0, i))],
        out_shape=[jax.ShapeDtypeStruct((N_PICK, n), jnp.int32),
                   jax.ShapeDtypeStruct((N_PICK, n), jnp.float32)],
        scratch_shapes=[top(jnp.float32), top(jnp.int32), top(jnp.float32), top(jnp.int32),
                        top(jnp.float32), top(jnp.int32)],
        compiler_params=pltpu.CompilerParams(
            dimension_semantics=("arbitrary",), vmem_limit_bytes=48 * 1024 * 1024),
        name="peer_route",
    )(h, wq_t, keys)


PEER_TOK_BLOCK = 128


def _peer_expert_kernel(e_hbm, h_ref, coef_ref, uv_hbm, g_ref, b_ref, y_ref,
                        e_smem, uvbuf, f_ref, sem_e, sem_rows):
    TB = h_ref.shape[0]
    i = pl.program_id(0)
    ids = pltpu.make_async_copy(e_hbm.at[:, pl.ds(i * TB, TB)], e_smem, sem_e)
    ids.start()
    ids.wait()

    def issue(t, slot):
        for k in range(N_PICK):
            pltpu.make_async_copy(uv_hbm.at[pl.ds(e_smem[k, t], 1), :], uvbuf.at[slot, pl.ds(k, 1), :],
                                  sem_rows.at[slot]).start()

    def wait_rows(slot):
        pltpu.make_async_copy(uv_hbm.at[pl.ds(0, N_PICK), :], uvbuf.at[slot], sem_rows.at[slot]).wait()

    coef_t = coef_ref[...]
    tok = lax.broadcasted_iota(jnp.int32, coef_t.shape, 1)
    issue(0, 0)

    def body(t, carry):
        slot = t % 2

        @pl.when(t + 1 < TB)
        def _():
            issue(t + 1, 1 - slot)

        wait_rows(slot)
        a = jnp.sum(uvbuf[slot, :, :D_MODEL] * h_ref[pl.ds(t, 1), :], axis=-1, keepdims=True)
        c = jnp.sum(jnp.where(tok == t, coef_t, 0.0), axis=-1, keepdims=True)
        f_ref[pl.ds(t, 1), :] = jnp.sum(uvbuf[slot, :, D_MODEL:] * (c * jax.nn.gelu(a)), axis=0, keepdims=True)
        return carry

    lax.fori_loop(0, TB, body, 0)
    x = ALPHA * h_ref[...] + f_ref[...]
    mu = x.mean(-1, keepdims=True)
    var = jnp.square(x - mu).mean(-1, keepdims=True)
    y_ref[...] = (x - mu) * lax.rsqrt(var + LN_EPS) * g_ref[...] + b_ref[...]


def peer_expert_pallas(e_t, h, coef_t, uv_tab, ln_g, ln_b):
    n = h.shape[0]
    tb = min(PEER_TOK_BLOCK, n)
    assert n % tb == 0
    fixed = lambda i: (0, 0)
    return pl.pallas_call(
        _peer_expert_kernel,
        grid=(n // tb,),
        in_specs=[pl.BlockSpec(memory_space=pl.ANY),
                  pl.BlockSpec((tb, D_MODEL), lambda i: (i, 0)),
                  pl.BlockSpec((N_PICK, tb), lambda i: (0, i)),
                  pl.BlockSpec(memory_space=pl.ANY),
                  pl.BlockSpec((1, D_MODEL), fixed),
                  pl.BlockSpec((1, D_MODEL), fixed)],
        out_specs=pl.BlockSpec((tb, D_MODEL), lambda i: (i, 0)),
        out_shape=jax.ShapeDtypeStruct((n, D_MODEL), jnp.float32),
        scratch_shapes=[pltpu.SMEM((N_PICK, tb), jnp.int32),
                        pltpu.VMEM((2, N_PICK, 2 * D_MODEL), jnp.float32),
                        pltpu.VMEM((tb, D_MODEL), jnp.float32),
                        pltpu.SemaphoreType.DMA,
                        pltpu.SemaphoreType.DMA((2,))],
        compiler_params=pltpu.CompilerParams(dimension_semantics=("arbitrary",)),
        name="peer_experts",
    )(e_t, h, coef_t, uv_tab, ln_g.reshape(1, D_MODEL), ln_b.reshape(1, D_MODEL))


def layer_norm(x, g, b):
    xf = x.astype(jnp.float32)
    mu = xf.mean(-1, keepdims=True)
    var = jnp.square(xf - mu).mean(-1, keepdims=True)
    return ((xf - mu) * lax.rsqrt(var + LN_EPS) * g + b).astype(x.dtype)


def rms_norm(x, g):
    xf = x.astype(jnp.float32)
    return (xf * lax.rsqrt(jnp.square(xf).mean(-1, keepdims=True) + LN_EPS) * g).astype(x.dtype)


def masked_softmax(s, mask):
    s = jnp.where(mask, s, NEG)
    e = jnp.exp(s - s.max(-1, keepdims=True)) * mask
    tot = e.sum(-1, keepdims=True)
    return e / jnp.where(tot > 0, tot, 1.0)


def alibi_slopes():
    i = jnp.arange(1, H_NSA + 1, dtype=jnp.float32)
    return (2.0 ** (-8.0 * i / H_NSA)).reshape(KV_GROUPS, HPG)


def project(x, w_in_pad):
    B, T, _ = x.shape
    z = project_pallas(x.reshape(B * T, D_MODEL), w_in_pad).reshape(B, T, IN_COLS_PAD)
    q_a = z[..., :SB_WIDTH].reshape(B, T, H_SB, HEAD_DIM)
    sb_kv = z[..., OFF_SBKV:OFF_QB].reshape(B, T, 2, H_SB, HEAD_DIM)
    q_b = z[..., OFF_QB:OFF_NSAKV].reshape(B, T, H_NSA, HEAD_DIM)
    nsa_kv = z[..., OFF_NSAKV:OFF_WIN].reshape(B, T, 4, KV_GROUPS, HEAD_DIM)
    win_kv = z[..., OFF_WIN:OFF_GATE].reshape(B, T, 2, KV_GROUPS, HEAD_DIM)
    gates = jax.nn.sigmoid(z[..., OFF_GATE:IN_COLS]).reshape(B, T, H_NSA, 3)
    return q_a, sb_kv, q_b, nsa_kv, win_kv, gates


def stick_breaking(q, q_pos, k, v, k_pos):
    z = jnp.einsum('bthd,bshd->bhts', q, k).astype(jnp.float32) * SCALE
    mask = k_pos[None, :] < q_pos[:, None]
    log_keep = jnp.where(mask, jax.nn.log_sigmoid(-z), 0.0)
    cum = jnp.cumsum(log_keep, axis=-1)
    after = cum[..., -1:] - cum
    w = jnp.where(mask, jnp.exp(jax.nn.log_sigmoid(z) + after), 0.0)
    return jnp.einsum('bhts,bshd->bthd', w.astype(v.dtype), v)


def compress(kc, vc, cmp_pe, cmp_w1, cmp_w2):
    L = kc.shape[1]
    n_c = (L - COMP_BLOCK) // COMP_STRIDE + 1
    starts = jnp.arange(n_c) * COMP_STRIDE
    idx = starts[:, None] + jnp.arange(COMP_BLOCK)[None, :]

    def phi(t, j):
        blk = t[:, idx] + cmp_pe[j][None, None, :, None, :]
        hdn = jax.nn.gelu(jnp.einsum('bnlgd,ldh->bngh', blk, cmp_w1[j]))
        return jnp.einsum('bngh,hd->bngd', hdn, cmp_w2[j])

    ck = phi(kc, 0)
    cv = phi(vc, 1)
    c_pos = starts + COMP_BLOCK - 1
    n_sel = -(-L // SEL_BLOCK)
    sel_map = jax.nn.one_hot(starts // SEL_BLOCK, n_sel, dtype=jnp.float32)
    return ck, cv, c_pos, sel_map


def sel_prep(k, n_sel):
    pad = n_sel * SEL_BLOCK - k.shape[1]
    return jnp.pad(k, ((0, 0), (0, pad), (0, 0), (0, 0))).transpose(0, 2, 1, 3)


def gather_rows(a, idx):
    return jax.vmap(jax.vmap(lambda r, i: r[i]))(a, idx)


def nsa_attend(q, q_pos, gates, ck, cv, c_pos, sel_map, sk, sv, wk, wv, w_pos):
    B, Tq = q.shape[:2]
    qg = q.reshape(B, Tq, KV_GROUPS, HPG, HEAD_DIM)
    slopes = alibi_slopes()[None, :, :, None, None]
    qp = q_pos.astype(jnp.float32)
    s = jnp.einsum('btgrd,bngd->bgrtn', qg, ck).astype(jnp.float32) * SCALE
    s = s - slopes * (qp[:, None] - c_pos[None, :].astype(jnp.float32))
    p_c = masked_softmax(s, c_pos[None, :] <= q_pos[:, None])
    o_c = jnp.einsum('bgrtn,bngd->btgrd', p_c.astype(cv.dtype), cv)
    imp = jnp.einsum('bgrtn,nj->bgtj', p_c, sel_map)
    n_sel = sel_map.shape[1]
    blk = jnp.arange(n_sel)
    valid = blk[None, :] * SEL_BLOCK <= q_pos[:, None]
    forced = (blk[None, :] == (q_pos // SEL_BLOCK)[:, None]) | (blk[None, :] == 0)
    imp = jnp.where(forced, FORCE, jnp.where(valid, imp, -1.0))
    n_top = min(TOP_N, n_sel)
    _, idx = lax.top_k(imp, n_top)
    kpos = (idx[..., None] * SEL_BLOCK + jnp.arange(SEL_BLOCK)).reshape(B, KV_GROUPS, Tq, n_top * SEL_BLOCK)
    flat = kpos.reshape(B, KV_GROUPS, -1)
    ksel = gather_rows(sk, flat).reshape(B, KV_GROUPS, Tq, n_top * SEL_BLOCK, HEAD_DIM)
    vsel = gather_rows(sv, flat).reshape(B, KV_GROUPS, Tq, n_top * SEL_BLOCK, HEAD_DIM)
    dist_s = (q_pos[None, None, :, None] - kpos)[:, :, None]
    s = jnp.einsum('btgrd,bgtsd->bgrts', qg, ksel).astype(jnp.float32) * SCALE
    s = s - slopes * dist_s.astype(jnp.float32)
    p_s = masked_softmax(s, dist_s >= 0)
    o_s = jnp.einsum('bgrts,bgtsd->btgrd', p_s.astype(vsel.dtype), vsel)
    dist_w = q_pos[:, None] - w_pos[None, :]
    mask_w = (dist_w >= 0) & (dist_w < WINDOW) & (w_pos[None, :] >= 0)
    s = jnp.einsum('btgrd,bsgd->bgrts', qg, wk).astype(jnp.float32) * SCALE
    s = s - slopes * dist_w.astype(jnp.float32)
    p_w = masked_softmax(s, mask_w)
    o_w = jnp.einsum('bgrts,bsgd->btgrd', p_w.astype(wv.dtype), wv)
    g = gates.reshape(B, Tq, KV_GROUPS, HPG, 3)
    o = g[..., 0:1] * o_c + g[..., 1:2] * o_s + g[..., 2:3] * o_w
    return o.reshape(B, Tq, NSA_WIDTH)


def mixer_prompt(x, w_in_pad, cmp_pe, cmp_w1, cmp_w2):
    B, T, _ = x.shape
    z = project_pallas(x.reshape(B * T, D_MODEL), w_in_pad).reshape(B, T, IN_COLS_PAD)
    sb_kv = z[..., OFF_SBKV:OFF_QB].reshape(B, T, 2, H_SB, HEAD_DIM)
    nsa_kv = z[..., OFF_NSAKV:OFF_WIN].reshape(B, T, 4, KV_GROUPS, HEAD_DIM)
    win_kv = z[..., OFF_WIN:OFF_GATE].reshape(B, T, 2, KV_GROUPS, HEAD_DIM)
    ck, cv, c_pos, sel_map = compress(nsa_kv[:, :, 0], nsa_kv[:, :, 1], cmp_pe, cmp_w1, cmp_w2)
    pad_c = lambda c: jnp.pad(c, ((0, 0), (0, N_CMP_PAD - c.shape[1]), (0, 0), (0, 0))).transpose(0, 2, 1, 3)
    o_b = nsa_prompt_pallas(z[..., OFF_QB:OFF_NSAKV], z[..., OFF_GATE:IN_COLS], pad_c(ck), pad_c(cv),
                            z[..., OFF_NSAKV:OFF_WIN], z[..., OFF_WIN:OFF_GATE])
    o_a = sb_prompt_pallas(z[..., :SB_WIDTH], z[..., OFF_SBKV:OFF_QB])
    new_win = win_kv[:, T - min(WINDOW, T):]
    return o_a, o_b, sb_kv, nsa_kv, new_win


def mixer_sample(x, cache_sb, cache_nsa, cache_win, page_table, w_in_pad, cmp_pe, cmp_w1, cmp_w2):
    B, T, _ = x.shape
    past = page_table.shape[1] * cache_sb.shape[1]
    q_a, sb_kv, q_b, nsa_kv, win_kv, gates = project(x, w_in_pad)
    q_pos = past + jnp.arange(T)
    k_pos = jnp.arange(past + T)
    sb_all = jnp.concatenate([cache_sb[page_table].reshape(B, past, 2, H_SB, HEAD_DIM), sb_kv], axis=1)
    o_a = stick_breaking(q_a, q_pos, sb_all[:, :, 0], sb_all[:, :, 1], k_pos).reshape(B, T, SB_WIDTH)
    nsa_all = jnp.concatenate([cache_nsa[page_table].reshape(B, past, 4, KV_GROUPS, HEAD_DIM), nsa_kv], axis=1)
    ck, cv, c_pos, sel_map = compress(nsa_all[:, :, 0], nsa_all[:, :, 1], cmp_pe, cmp_w1, cmp_w2)
    n_sel = sel_map.shape[1]
    sk = sel_prep(nsa_all[:, :, 2], n_sel)
    sv = sel_prep(nsa_all[:, :, 3], n_sel)
    wb = cache_win.shape[1]
    win_all = jnp.concatenate([cache_win, win_kv], axis=1)
    w_pos = past - wb + jnp.arange(wb + T)
    o_b = nsa_attend(q_b, q_pos, gates, ck, cv, c_pos, sel_map, sk, sv,
                     win_all[:, :, 0], win_all[:, :, 1], w_pos)
    new_win = win_all[:, T:]
    return o_a, o_b, sb_kv, nsa_kv, new_win


def block_out(x, o_a, o_b, mix_g, w_out, ln1_g, ln1_b, wq_t, keys, uv_tab, ln2_g, ln2_b):
    m = jnp.concatenate([rms_norm(o_a, mix_g[:SB_WIDTH]), rms_norm(o_b, mix_g[SB_WIDTH:])], axis=-1) @ w_out
    h = layer_norm(ALPHA * x + m, ln1_g, ln1_b).reshape(-1, D_MODEL)
    e_t, g_t = peer_route_pallas(h, wq_t, keys)
    return peer_expert_pallas(e_t, h, g_t, uv_tab, ln2_g, ln2_b).reshape(x.shape)


def kernel(x_prompt, x_sample, cache_sb_kv, cache_nsa_kv, cache_win_kv, page_table, w_in, cmp_pe, cmp_w1, cmp_w2, mix_norm_g, w_out, ln1_g, ln1_b, peer_w_q, peer_sub_keys, peer_u, peer_v, ln2_g, ln2_b):
    l = 0
    w_in_pad = jnp.pad(w_in[l], ((0, 0), (0, IN_COLS_PAD - IN_COLS))).astype(jnp.bfloat16)
    o_a, o_b, sb_p, nsa_p, win_p = mixer_prompt(x_prompt, w_in_pad, cmp_pe[l], cmp_w1[l], cmp_w2[l])
    wq_t = peer_w_q[l].T.astype(jnp.bfloat16)
    keys = peer_sub_keys[l].reshape(2 * PEER_HEADS, N_KEYS, D_KEY // 2).astype(jnp.bfloat16)
    uv_tab = jnp.concatenate([peer_u[l], peer_v[l]], axis=1)
    h_p = block_out(x_prompt, o_a, o_b, mix_norm_g[l], w_out[l], ln1_g[l], ln1_b[l],
                    wq_t, keys, uv_tab, ln2_g[l], ln2_b[l])
    o_a, o_b, sb_s, nsa_s, win_s = mixer_sample(x_sample, cache_sb_kv[l], cache_nsa_kv[l], cache_win_kv[l],
                                                page_table, w_in_pad, cmp_pe[l], cmp_w1[l], cmp_w2[l])
    h_s = block_out(x_sample, o_a, o_b, mix_norm_g[l], w_out[l], ln1_g[l], ln1_b[l],
                    wq_t, keys, uv_tab, ln2_g[l], ln2_b[l])
    return (h_p, h_s, sb_p[None], nsa_p[None], win_p[None], sb_s[None], nsa_s[None], win_s[None])
```

```python
import math
from functools import partial

import jax, jax.numpy as jnp
from jax import lax
import numpy as np
from jax.experimental import pallas as pl
from jax.experimental.pallas import tpu as pltpu

D_MODEL = 1024
HEAD_DIM = 64
MIX_WIDTH = D_MODEL
SB_WIDTH = MIX_WIDTH // 2
NSA_WIDTH = MIX_WIDTH - SB_WIDTH
H_SB = SB_WIDTH // HEAD_DIM
H_NSA = NSA_WIDTH // HEAD_DIM
KV_GROUPS = 2
HPG = H_NSA // KV_GROUPS
COMP_BLOCK = 32
COMP_STRIDE = 16
COMP_HID = 128
SEL_BLOCK = 64
TOP_N = 8
WINDOW = 512
Q_BLOCK = 128
PEER_HEADS = 8
N_KEYS = 128
N_EXPERTS = N_KEYS * N_KEYS
PEER_TOPK = 16
D_KEY = 256
TOK_CHUNK = 256
DEPTH = 1
ALPHA = (2.0 * DEPTH) ** 0.25
LN_EPS = 1e-5
NEG = -1e30
FORCE = 1e4
SCALE = HEAD_DIM ** -0.5

OFF_SBKV = SB_WIDTH
OFF_QB = 3 * SB_WIDTH
OFF_NSAKV = OFF_QB + NSA_WIDTH
OFF_WIN = OFF_NSAKV + 4 * KV_GROUPS * HEAD_DIM
OFF_GATE = OFF_WIN + 2 * KV_GROUPS * HEAD_DIM
IN_COLS = OFF_GATE + 3 * H_NSA

LANE = 128
IN_COLS_PAD = -(-IN_COLS // LANE) * LANE
PROJ_ROWS = 512


def _proj_kernel(x_ref, w_ref, z_ref):
    z_ref[...] = jnp.dot(x_ref[...].astype(jnp.bfloat16), w_ref[...],
                         preferred_element_type=jnp.float32)


def project_pallas(x2d, w_in_pad_bf16):
    n = x2d.shape[0]
    rows = min(PROJ_ROWS, n)
    return pl.pallas_call(
        _proj_kernel,
        grid=(n // rows,),
        in_specs=[pl.BlockSpec((rows, D_MODEL), lambda i: (i, 0)),
                  pl.BlockSpec((D_MODEL, IN_COLS_PAD), lambda i: (0, 0))],
        out_specs=pl.BlockSpec((rows, IN_COLS_PAD), lambda i: (i, 0)),
        out_shape=jax.ShapeDtypeStruct((n, IN_COLS_PAD), jnp.float32),
        compiler_params=pltpu.CompilerParams(
            dimension_semantics=("arbitrary",), vmem_limit_bytes=48 * 1024 * 1024),
        name="in_proj",
    )(x2d, w_in_pad_bf16)


N_SEL = 32
N_CMP_PAD = 128
SEL_TILE = 512
WIN_TILE = 128
ALIBI = [[2.0 ** (-8.0 * (g * HPG + r + 1) / H_NSA) for r in range(HPG)] for g in range(KV_GROUPS)]


def _dot_nt(a, b):
    return lax.dot_general(a, b, (((1,), (1,)), ((), ())), preferred_element_type=jnp.float32)


def _split_dot(a, b_exact):
    hi = a.astype(jnp.bfloat16)
    r1 = a - hi.astype(jnp.float32)
    mid = r1.astype(jnp.bfloat16)
    lo = (r1 - mid.astype(jnp.float32)).astype(jnp.bfloat16)
    d = lambda x: jnp.dot(x, b_exact, preferred_element_type=jnp.float32)
    return d(hi) + d(mid) + d(lo)


def _nsa_prompt_kernel(q_ref, gz_ref, ck_ref, cv_ref, nsa_ref, win_ref, o_ref, m_ref, l_ref, acc_ref):
    QB = Q_BLOCK
    qb = pl.program_id(1)
    q0 = qb * QB
    bf = jnp.bfloat16
    f32 = jnp.float32
    gate = jax.nn.sigmoid(gz_ref[0])
    row = lax.broadcasted_iota(jnp.int32, (QB, 1), 0)
    qpos = q0 + row

    def reset():
        m_ref[...] = jnp.full(m_ref.shape, NEG, f32)
        l_ref[...] = jnp.zeros(l_ref.shape, f32)
        acc_ref[...] = jnp.zeros(acc_ref.shape, f32)

    def flash_step(r, s, valid, v):
        s = jnp.where(valid, s, NEG)
        m_old = m_ref[r]
        m_new = jnp.maximum(m_old, s.max(-1, keepdims=True))
        p = jnp.where(valid, jnp.exp(s - m_new), 0.0)
        alpha = jnp.exp(m_old - m_new)
        l_ref[r] = alpha * l_ref[r] + p.sum(-1, keepdims=True)
        acc_ref[r] = alpha * acc_ref[r] + jnp.dot(p.astype(bf), v, preferred_element_type=f32)
        m_ref[r] = m_new
        return p

    def finish(g, branch, first):
        for r in range(HPG):
            h = g * HPG + r
            tot = l_ref[r]
            o = acc_ref[r] / jnp.where(tot > 0, tot, 1.0)
            o = gate[:, 3 * h + branch:3 * h + branch + 1] * o
            sl = (0, slice(None), slice(h * HEAD_DIM, (h + 1) * HEAD_DIM))
            if first:
                o_ref[sl] = o
            else:
                o_ref[sl] = o_ref[sl] + o

    for g in range(KV_GROUPS):
        qh = [(q_ref[0, :, (g * HPG + r) * HEAD_DIM:(g * HPG + r + 1) * HEAD_DIM] * SCALE).astype(bf)
              for r in range(HPG)]

        reset()
        n_i = lax.broadcasted_iota(jnp.int32, (1, N_CMP_PAD), 1)
        cpos = n_i * COMP_STRIDE + (COMP_BLOCK - 1)
        n_cmp = (pl.num_programs(1) * QB - COMP_BLOCK) // COMP_STRIDE + 1
        valid_c = (cpos <= qpos) & (n_i < n_cmp)
        dist_c = (qpos - cpos).astype(f32)
        ckg = ck_ref[0, g].astype(bf)
        cvg = cv_ref[0, g].astype(bf)
        psum = jnp.zeros((QB, N_CMP_PAD), f32)
        for r in range(HPG):
            s = _dot_nt(qh[r], ckg) - ALIBI[g][r] * dist_c
            p = flash_step(r, s, valid_c, cvg)
            tot = l_ref[r]
            psum = psum + p / jnp.where(tot > 0, tot, 1.0)
        finish(g, 0, True)

        nn = lax.broadcasted_iota(jnp.int32, (N_CMP_PAD, N_SEL), 0)
        jj = lax.broadcasted_iota(jnp.int32, (N_CMP_PAD, N_SEL), 1)
        sel_map = ((nn * COMP_STRIDE) // SEL_BLOCK == jj).astype(bf)
        imp = _split_dot(psum, sel_map)
        blk = lax.broadcasted_iota(jnp.int32, (QB, N_SEL), 1)
        forced = (blk == qpos // SEL_BLOCK) | (blk == 0)
        imp = jnp.where(forced, FORCE, jnp.where(blk * SEL_BLOCK <= qpos, imp, -1.0))
        chosen = jnp.zeros((QB, N_SEL), f32)
        for _ in range(TOP_N):
            mx = imp.max(-1, keepdims=True)
            first_max = jnp.where(imp == mx, blk, N_SEL).min(-1, keepdims=True)
            pick = blk == first_max
            chosen = jnp.where(pick, 1.0, chosen)
            imp = jnp.where(pick, -2.0, imp)
        chosen = chosen.astype(bf)

        reset()

        def sel_body(kt, carry):
            k0 = pl.multiple_of(kt * SEL_TILE, SEL_TILE)
            k = nsa_ref[0, pl.ds(k0, SEL_TILE), (4 + g) * HEAD_DIM:(5 + g) * HEAD_DIM].astype(bf)
            v = nsa_ref[0, pl.ds(k0, SEL_TILE), (6 + g) * HEAD_DIM:(7 + g) * HEAD_DIM].astype(bf)
            col = lax.broadcasted_iota(jnp.int32, (1, SEL_TILE), 1) + k0
            dist = qpos - col
            ej = lax.broadcasted_iota(jnp.int32, (N_SEL, SEL_TILE), 0)
            ec = lax.broadcasted_iota(jnp.int32, (N_SEL, SEL_TILE), 1) + k0
            expand = (ec // SEL_BLOCK == ej).astype(bf)
            in_block = jnp.dot(chosen, expand, preferred_element_type=f32)
            valid = (in_block > 0.5) & (dist >= 0)
            distf = dist.astype(f32)
            for r in range(HPG):
                s = _dot_nt(qh[r], k) - ALIBI[g][r] * distf
                flash_step(r, s, valid, v)
            return carry

        lax.fori_loop(0, (q0 + QB + SEL_TILE - 1) // SEL_TILE, sel_body, 0)
        finish(g, 1, False)

        reset()

        def win_body(kt, carry):
            k0 = pl.multiple_of(kt * WIN_TILE, WIN_TILE)
            k = win_ref[0, pl.ds(k0, WIN_TILE), g * HEAD_DIM:(g + 1) * HEAD_DIM].astype(bf)
            v = win_ref[0, pl.ds(k0, WIN_TILE), (2 + g) * HEAD_DIM:(3 + g) * HEAD_DIM].astype(bf)
            col = lax.broadcasted_iota(jnp.int32, (1, WIN_TILE), 1) + k0
            dist = qpos - col
            valid = (dist >= 0) & (dist < WINDOW)
            distf = dist.astype(f32)
            for r in range(HPG):
                s = _dot_nt(qh[r], k) - ALIBI[g][r] * distf
                flash_step(r, s, valid, v)
            return carry

        lax.fori_loop(jnp.maximum(qb - WINDOW // WIN_TILE, 0), qb + 1, win_body, 0)
        finish(g, 2, False)


def nsa_prompt_pallas(q_b, gate_z, ck, cv, nsa_kv, win_kv):
    B, T, _ = q_b.shape
    assert T % Q_BLOCK == 0 and T // SEL_BLOCK <= N_SEL and (T - COMP_BLOCK) // COMP_STRIDE + 1 <= N_CMP_PAD
    per_b = lambda b, i: (b, 0, 0)
    return pl.pallas_call(
        _nsa_prompt_kernel,
        grid=(B, T // Q_BLOCK),
        in_specs=[pl.BlockSpec((1, Q_BLOCK, NSA_WIDTH), lambda b, i: (b, i, 0)),
                  pl.BlockSpec((1, Q_BLOCK, 3 * H_NSA), lambda b, i: (b, i, 0)),
                  pl.BlockSpec((1, KV_GROUPS, N_CMP_PAD, HEAD_DIM), lambda b, i: (b, 0, 0, 0)),
                  pl.BlockSpec((1, KV_GROUPS, N_CMP_PAD, HEAD_DIM), lambda b, i: (b, 0, 0, 0)),
                  pl.BlockSpec((1, T, 4 * KV_GROUPS * HEAD_DIM), per_b),
                  pl.BlockSpec((1, T, 2 * KV_GROUPS * HEAD_DIM), per_b)],
        out_specs=pl.BlockSpec((1, Q_BLOCK, NSA_WIDTH), lambda b, i: (b, i, 0)),
        out_shape=jax.ShapeDtypeStruct((B, T, NSA_WIDTH), jnp.float32),
        scratch_shapes=[pltpu.VMEM((HPG, Q_BLOCK, 1), jnp.float32),
                        pltpu.VMEM((HPG, Q_BLOCK, 1), jnp.float32),
                        pltpu.VMEM((HPG, Q_BLOCK, HEAD_DIM), jnp.float32)],
        compiler_params=pltpu.CompilerParams(
            dimension_semantics=("arbitrary", "arbitrary"), vmem_limit_bytes=48 * 1024 * 1024),
        name="nsa_prompt",
    )(q_b, gate_z, ck, cv, nsa_kv, win_kv)


SB_TILE = 128
N_PICK = PEER_HEADS * PEER_TOPK


def _split2_dot(a, b_exact):
    hi = a.astype(jnp.bfloat16)
    lo = (a - hi.astype(jnp.float32)).astype(jnp.bfloat16)
    return (jnp.dot(hi, b_exact, preferred_element_type=jnp.float32)
            + jnp.dot(lo, b_exact, preferred_element_type=jnp.float32))


def _sb_prompt_kernel(q_ref, kv_ref, o_ref, carry_ref, acc_ref):
    TQ = TK = SB_TILE
    bf, f32 = jnp.bfloat16, jnp.float32
    qb = pl.program_id(1)
    qpos = qb * TQ + lax.broadcasted_iota(jnp.int32, (TQ, 1), 0)
    later = (lax.broadcasted_iota(jnp.int32, (TK, TK), 0) > lax.broadcasted_iota(jnp.int32, (TK, TK), 1)).astype(bf)
    carry_ref[...] = jnp.zeros(carry_ref.shape, f32)
    acc_ref[...] = jnp.zeros(acc_ref.shape, f32)

    def body(i, c):
        k0 = pl.multiple_of((qb - i) * TK, TK)
        valid = (lax.broadcasted_iota(jnp.int32, (1, TK), 1) + k0) < qpos
        for h in range(H_SB):
            q = (q_ref[0, :, h * HEAD_DIM:(h + 1) * HEAD_DIM] * SCALE).astype(bf)
            k = kv_ref[0, pl.ds(k0, TK), h * HEAD_DIM:(h + 1) * HEAD_DIM].astype(bf)
            v = kv_ref[0, pl.ds(k0, TK), SB_WIDTH + h * HEAD_DIM:SB_WIDTH + (h + 1) * HEAD_DIM].astype(bf)
            z = _dot_nt(q, k)
            soft = jnp.log1p(jnp.exp(-jnp.abs(z)))
            log_sig = jnp.minimum(z, 0.0) - soft
            log_keep = jnp.where(valid, log_sig - z, 0.0)
            after = _split2_dot(log_keep, later) + carry_ref[h]
            w = jnp.where(valid, jnp.exp(log_sig + after), 0.0)
            acc_ref[h] = acc_ref[h] + jnp.dot(w.astype(bf), v, preferred_element_type=f32)
            carry_ref[h] = carry_ref[h] + log_keep.sum(-1, keepdims=True)
        return c

    lax.fori_loop(0, qb + 1, body, 0)
    for h in range(H_SB):
        o_ref[0, :, h * HEAD_DIM:(h + 1) * HEAD_DIM] = acc_ref[h]


def sb_prompt_pallas(q_a, sb_kv):
    B, T, _ = q_a.shape
    assert T % SB_TILE == 0
    return pl.pallas_call(
        _sb_prompt_kernel,
        grid=(B, T // SB_TILE),
        in_specs=[pl.BlockSpec((1, SB_TILE, SB_WIDTH), lambda b, i: (b, i, 0)),
                  pl.BlockSpec((1, T, 2 * SB_WIDTH), lambda b, i: (b, 0, 0))],
        out_specs=pl.BlockSpec((1, SB_TILE, SB_WIDTH), lambda b, i: (b, i, 0)),
        out_shape=jax.ShapeDtypeStruct((B, T, SB_WIDTH), jnp.float32),
        scratch_shapes=[pltpu.VMEM((H_SB, SB_TILE, 1), jnp.float32),
                        pltpu.VMEM((H_SB, SB_TILE, HEAD_DIM), jnp.float32)],
        compiler_params=pltpu.CompilerParams(
            dimension_semantics=("arbitrary", "arbitrary"), vmem_limit_bytes=48 * 1024 * 1024),
        name="sb_prompt",
    )(q_a, sb_kv)


ROUTE_TOK = 256
N_CAND = PEER_TOPK * PEER_TOPK


def _top_rows(s, n_top, vals_ref, idx_ref):
    rows = lax.broadcasted_iota(jnp.int32, s.shape, 0)
    for a in range(n_top):
        m = s.max(0, keepdims=True)
        ix = jnp.where(s == m, rows, s.shape[0]).min(0, keepdims=True)
        s = jnp.where(rows == ix, -jnp.inf, s)
        vals_ref[a:a + 1, :] = m
        idx_ref[a:a + 1, :] = ix


def _peer_route_kernel(h_ref, wq_t_ref, keys_ref, e_ref, g_ref, s1_ref, i1_ref, s2_ref, i2_ref, ts_ref, ti_ref,
                       et_ref):
    bf, f32 = jnp.bfloat16, jnp.float32
    q_t = _dot_nt(wq_t_ref[...], h_ref[...].astype(bf))
    half = D_KEY // 2
    for p in range(PEER_HEADS):
        for hf, (sv, si) in enumerate(((s1_ref, i1_ref), (s2_ref, i2_ref))):
            r0 = (2 * p + hf) * half
            s = jnp.dot(keys_ref[2 * p + hf], q_t[r0:r0 + half, :].astype(bf), preferred_element_type=f32)
            _top_rows(s, PEER_TOPK, sv, si)
        s2 = s2_ref[...]
        cand = jnp.concatenate([s1_ref[a:a + 1, :] + s2 for a in range(PEER_TOPK)], axis=0)
        _top_rows(cand, PEER_TOPK, ts_ref, ti_ref)
        ti = ti_ref[...]
        a_of, b_of = ti // PEER_TOPK, ti % PEER_TOPK
        k1 = jnp.zeros(ti.shape, jnp.int32)
        k2 = jnp.zeros(ti.shape, jnp.int32)
        for a in range(PEER_TOPK):
            k1 = jnp.where(a_of == a, i1_ref[a:a + 1, :], k1)
            k2 = jnp.where(b_of == a, i2_ref[a:a + 1, :], k2)
        ts = ts_ref[...]
        ex = jnp.exp(ts - ts.max(0, keepdims=True))
        et_ref[p * PEER_TOPK:(p + 1) * PEER_TOPK, :] = k1 * N_KEYS + k2
        g_ref[p * PEER_TOPK:(p + 1) * PEER_TOPK, :] = ex / ex.sum(0, keepdims=True)
    e_ref[...] = et_ref[...].T


def peer_route_pallas(h, wq_t, keys):
    n = h.shape[0]
    assert n % ROUTE_TOK == 0
    tn = ROUTE_TOK
    top = lambda dt: pltpu.VMEM((PEER_TOPK, tn), dt)
    return pl.pallas_call(
        _peer_route_kernel,
        grid=(n // tn,),
        in_specs=[pl.BlockSpec((tn, D_MODEL), lambda i: (i, 0)),
                  pl.BlockSpec((PEER_HEADS * D_KEY, D_MODEL), lambda i: (0, 0)),
                  pl.BlockSpec((2 * PEER_HEADS, N_KEYS, D_KEY // 2), lambda i: (0, 0, 0))],
        out_specs=[pl.BlockSpec((tn, N_PICK), lambda i: (i, 0)),
                   pl.BlockSpec((N_PICK, tn), lambda i: (0, i))],
        out_shape=[jax.ShapeDtypeStruct((n, N_PICK), jnp.int32),
                   jax.ShapeDtypeStruct((N_PICK, n), jnp.float32)],
        scratch_shapes=[top(jnp.float32), top(jnp.int32), top(jnp.float32), top(jnp.int32),
                        top(jnp.float32), top(jnp.int32), pltpu.VMEM((N_PICK, tn), jnp.int32)],
        compiler_params=pltpu.CompilerParams(
            dimension_semantics=("arbitrary",), vmem_limit_bytes=48 * 1024 * 1024),
        name="peer_route",
    )(h, wq_t, keys)


PEER_TOK_BLOCK = 128
SUBLANE = 8
VEC_ROWS = D_MODEL // LANE
SLAB = 2 * VEC_ROWS


def _peer_expert_kernel(e_hbm, h_ref, coef_ref, uv_hbm, g_ref, b_ref, y_ref,
                        e_smem, uvbuf, f_ref, sem_e, sem_rows):
    TB = coef_ref.shape[1]
    i = pl.program_id(0)
    ids = pltpu.make_async_copy(e_hbm.at[pl.ds(i * TB, TB), :], e_smem, sem_e)
    ids.start()
    ids.wait()

    def issue(t, slot):
        for k in range(N_PICK):
            row0 = pl.multiple_of(e_smem[t, k] * SLAB, SLAB)
            pltpu.make_async_copy(uv_hbm.at[pl.ds(row0, SLAB), :], uvbuf.at[slot, pl.ds(k * SLAB, SLAB), :],
                                  sem_rows.at[slot]).start()

    def wait_rows(slot):
        pltpu.make_async_copy(uv_hbm.at[pl.ds(0, N_PICK * SLAB), :], uvbuf.at[slot], sem_rows.at[slot]).wait()

    coef_t = coef_ref[...]
    tok = lax.broadcasted_iota(jnp.int32, coef_t.shape, 1)

    def compute(t, slot):
        rows = lambda r: uvbuf[slot, pl.ds(r, N_PICK, stride=SLAB), :]
        r0 = pl.multiple_of(t * VEC_ROWS, VEC_ROWS)
        h = h_ref[pl.ds(r0, VEC_ROWS), :]
        part = rows(0) * h[0:1, :]
        for r in range(1, VEC_ROWS):
            part = part + rows(r) * h[r:r + 1, :]
        a = jnp.sum(part, axis=-1, keepdims=True)
        c = jnp.sum(jnp.where(tok == t, coef_t, 0.0), axis=-1, keepdims=True)
        w = c * jax.nn.gelu(a)
        for r in range(VEC_ROWS):
            f_ref[pl.ds(r0 + r, 1), :] = jnp.sum(rows(VEC_ROWS + r) * w, axis=0, keepdims=True)

    issue(0, 0)

    def body(j, carry):
        t = 2 * j
        issue(t + 1, 1)
        wait_rows(0)
        compute(t, 0)

        @pl.when(t + 2 < TB)
        def _():
            issue(t + 2, 0)

        wait_rows(1)
        compute(t + 1, 1)
        return carry

    lax.fori_loop(0, TB // 2, body, 0)
    x = (ALPHA * h_ref[...] + f_ref[...]).reshape(TB, VEC_ROWS, LANE)
    mean = lambda v: v.sum(axis=2, keepdims=True).sum(axis=1, keepdims=True) * (1.0 / D_MODEL)
    mu = mean(x)
    var = mean(jnp.square(x - mu))
    y = (x - mu) * lax.rsqrt(var + LN_EPS) * g_ref[...][None] + b_ref[...][None]
    y_ref[...] = y.reshape(TB * VEC_ROWS, LANE)


def peer_expert_pallas(e, h, coef_t, uv_slabs, ln_g, ln_b):
    n = h.shape[0]
    tb = min(PEER_TOK_BLOCK, n)
    assert n % tb == 0 and tb % 2 == 0
    fixed = lambda i: (0, 0)
    y = pl.pallas_call(
        _peer_expert_kernel,
        grid=(n // tb,),
        in_specs=[pl.BlockSpec(memory_space=pl.ANY),
                  pl.BlockSpec((tb * VEC_ROWS, LANE), lambda i: (i, 0)),
                  pl.BlockSpec((N_PICK, tb), lambda i: (0, i)),
                  pl.BlockSpec(memory_space=pl.ANY),
                  pl.BlockSpec((VEC_ROWS, LANE), fixed),
                  pl.BlockSpec((VEC_ROWS, LANE), fixed)],
        out_specs=pl.BlockSpec((tb * VEC_ROWS, LANE), lambda i: (i, 0)),
        out_shape=jax.ShapeDtypeStruct((n * VEC_ROWS, LANE), jnp.float32),
        scratch_shapes=[pltpu.SMEM((tb, N_PICK), jnp.int32),
                        pltpu.VMEM((2, N_PICK * SLAB, LANE), jnp.float32),
                        pltpu.VMEM((tb * VEC_ROWS, LANE), jnp.float32),
                        pltpu.SemaphoreType.DMA,
                        pltpu.SemaphoreType.DMA((2,))],
        compiler_params=pltpu.CompilerParams(dimension_semantics=("arbitrary",)),
        name="peer_experts",
    )(e, h.reshape(n * VEC_ROWS, LANE), coef_t, uv_slabs,
      ln_g.reshape(VEC_ROWS, LANE), ln_b.reshape(VEC_ROWS, LANE))
    return y.reshape(n, D_MODEL)


def layer_norm(x, g, b):
    xf = x.astype(jnp.float32)
    mu = xf.mean(-1, keepdims=True)
    var = jnp.square(xf - mu).mean(-1, keepdims=True)
    return ((xf - mu) * lax.rsqrt(var + LN_EPS) * g + b).astype(x.dtype)


def rms_norm(x, g):
    xf = x.astype(jnp.float32)
    return (xf * lax.rsqrt(jnp.square(xf).mean(-1, keepdims=True) + LN_EPS) * g).astype(x.dtype)


def masked_softmax(s, mask):
    s = jnp.where(mask, s, NEG)
    e = jnp.exp(s - s.max(-1, keepdims=True)) * mask
    tot = e.sum(-1, keepdims=True)
    return e / jnp.where(tot > 0, tot, 1.0)


def alibi_slopes():
    i = jnp.arange(1, H_NSA + 1, dtype=jnp.float32)
    return (2.0 ** (-8.0 * i / H_NSA)).reshape(KV_GROUPS, HPG)


def project(x, w_in_pad):
    B, T, _ = x.shape
    z = project_pallas(x.reshape(B * T, D_MODEL), w_in_pad).reshape(B, T, IN_COLS_PAD)
    q_a = z[..., :SB_WIDTH].reshape(B, T, H_SB, HEAD_DIM)
    sb_kv = z[..., OFF_SBKV:OFF_QB].reshape(B, T, 2, H_SB, HEAD_DIM)
    q_b = z[..., OFF_QB:OFF_NSAKV].reshape(B, T, H_NSA, HEAD_DIM)
    nsa_kv = z[..., OFF_NSAKV:OFF_WIN].reshape(B, T, 4, KV_GROUPS, HEAD_DIM)
    win_kv = z[..., OFF_WIN:OFF_GATE].reshape(B, T, 2, KV_GROUPS, HEAD_DIM)
    gates = jax.nn.sigmoid(z[..., OFF_GATE:IN_COLS]).reshape(B, T, H_NSA, 3)
    return q_a, sb_kv, q_b, nsa_kv, win_kv, gates


def stick_breaking(q, q_pos, k, v, k_pos):
    z = jnp.einsum('bthd,bshd->bhts', q, k).astype(jnp.float32) * SCALE
    mask = k_pos[None, :] < q_pos[:, None]
    log_keep = jnp.where(mask, jax.nn.log_sigmoid(-z), 0.0)
    cum = jnp.cumsum(log_keep, axis=-1)
    after = cum[..., -1:] - cum
    w = jnp.where(mask, jnp.exp(jax.nn.log_sigmoid(z) + after), 0.0)
    return jnp.einsum('bhts,bshd->bthd', w.astype(v.dtype), v)


def compress(kc, vc, cmp_pe, cmp_w1, cmp_w2):
    L = kc.shape[1]
    n_c = (L - COMP_BLOCK) // COMP_STRIDE + 1
    starts = jnp.arange(n_c) * COMP_STRIDE
    idx = starts[:, None] + jnp.arange(COMP_BLOCK)[None, :]

    def phi(t, j):
        blk = t[:, idx] + cmp_pe[j][None, None, :, None, :]
        hdn = jax.nn.gelu(jnp.einsum('bnlgd,ldh->bngh', blk, cmp_w1[j]))
        return jnp.einsum('bngh,hd->bngd', hdn, cmp_w2[j])

    ck = phi(kc, 0)
    cv = phi(vc, 1)
    c_pos = starts + COMP_BLOCK - 1
    n_sel = -(-L // SEL_BLOCK)
    sel_map = jax.nn.one_hot(starts // SEL_BLOCK, n_sel, dtype=jnp.float32)
    return ck, cv, c_pos, sel_map


def sel_prep(k, n_sel):
    pad = n_sel * SEL_BLOCK - k.shape[1]
    return jnp.pad(k, ((0, 0), (0, pad), (0, 0), (0, 0))).transpose(0, 2, 1, 3)


def gather_rows(a, idx):
    return jax.vmap(jax.vmap(lambda r, i: r[i]))(a, idx)


def nsa_attend(q, q_pos, gates, ck, cv, c_pos, sel_map, sk, sv, wk, wv, w_pos):
    B, Tq = q.shape[:2]
    qg = q.reshape(B, Tq, KV_GROUPS, HPG, HEAD_DIM)
    slopes = alibi_slopes()[None, :, :, None, None]
    qp = q_pos.astype(jnp.float32)
    s = jnp.einsum('btgrd,bngd->bgrtn', qg, ck).astype(jnp.float32) * SCALE
    s = s - slopes * (qp[:, None] - c_pos[None, :].astype(jnp.float32))
    p_c = masked_softmax(s, c_pos[None, :] <= q_pos[:, None])
    o_c = jnp.einsum('bgrtn,bngd->btgrd', p_c.astype(cv.dtype), cv)
    imp = jnp.einsum('bgrtn,nj->bgtj', p_c, sel_map)
    n_sel = sel_map.shape[1]
    blk = jnp.arange(n_sel)
    valid = blk[None, :] * SEL_BLOCK <= q_pos[:, None]
    forced = (blk[None, :] == (q_pos // SEL_BLOCK)[:, None]) | (blk[None, :] == 0)
    imp = jnp.where(forced, FORCE, jnp.where(valid, imp, -1.0))
    n_top = min(TOP_N, n_sel)
    _, idx = lax.top_k(imp, n_top)
    kpos = (idx[..., None] * SEL_BLOCK + jnp.arange(SEL_BLOCK)).reshape(B, KV_GROUPS, Tq, n_top * SEL_BLOCK)
    flat = kpos.reshape(B, KV_GROUPS, -1)
    ksel = gather_rows(sk, flat).reshape(B, KV_GROUPS, Tq, n_top * SEL_BLOCK, HEAD_DIM)
    vsel = gather_rows(sv, flat).reshape(B, KV_GROUPS, Tq, n_top * SEL_BLOCK, HEAD_DIM)
    dist_s = (q_pos[None, None, :, None] - kpos)[:, :, None]
    s = jnp.einsum('btgrd,bgtsd->bgrts', qg, ksel).astype(jnp.float32) * SCALE
    s = s - slopes * dist_s.astype(jnp.float32)
    p_s = masked_softmax(s, dist_s >= 0)
    o_s = jnp.einsum('bgrts,bgtsd->btgrd', p_s.astype(vsel.dtype), vsel)
    dist_w = q_pos[:, None] - w_pos[None, :]
    mask_w = (dist_w >= 0) & (dist_w < WINDOW) & (w_pos[None, :] >= 0)
    s = jnp.einsum('btgrd,bsgd->bgrts', qg, wk).astype(jnp.float32) * SCALE
    s = s - slopes * dist_w.astype(jnp.float32)
    p_w = masked_softmax(s, mask_w)
    o_w = jnp.einsum('bgrts,bsgd->btgrd', p_w.astype(wv.dtype), wv)
    g = gates.reshape(B, Tq, KV_GROUPS, HPG, 3)
    o = g[..., 0:1] * o_c + g[..., 1:2] * o_s + g[..., 2:3] * o_w
    return o.reshape(B, Tq, NSA_WIDTH)


def mixer_prompt(x, w_in_pad, cmp_pe, cmp_w1, cmp_w2):
    B, T, _ = x.shape
    z = project_pallas(x.reshape(B * T, D_MODEL), w_in_pad).reshape(B, T, IN_COLS_PAD)
    sb_kv = z[..., OFF_SBKV:OFF_QB].reshape(B, T, 2, H_SB, HEAD_DIM)
    nsa_kv = z[..., OFF_NSAKV:OFF_WIN].reshape(B, T, 4, KV_GROUPS, HEAD_DIM)
    win_kv = z[..., OFF_WIN:OFF_GATE].reshape(B, T, 2, KV_GROUPS, HEAD_DIM)
    ck, cv, c_pos, sel_map = compress(nsa_kv[:, :, 0], nsa_kv[:, :, 1], cmp_pe, cmp_w1, cmp_w2)
    pad_c = lambda c: jnp.pad(c, ((0, 0), (0, N_CMP_PAD - c.shape[1]), (0, 0), (0, 0))).transpose(0, 2, 1, 3)
    o_b = nsa_prompt_pallas(z[..., OFF_QB:OFF_NSAKV], z[..., OFF_GATE:IN_COLS], pad_c(ck), pad_c(cv),
                            z[..., OFF_NSAKV:OFF_WIN], z[..., OFF_WIN:OFF_GATE])
    o_a = sb_prompt_pallas(z[..., :SB_WIDTH], z[..., OFF_SBKV:OFF_QB])
    new_win = win_kv[:, T - min(WINDOW, T):]
    return o_a, o_b, sb_kv, nsa_kv, new_win


def mixer_sample(x, cache_sb, cache_nsa, cache_win, page_table, w_in_pad, cmp_pe, cmp_w1, cmp_w2):
    B, T, _ = x.shape
    past = page_table.shape[1] * cache_sb.shape[1]
    q_a, sb_kv, q_b, nsa_kv, win_kv, gates = project(x, w_in_pad)
    q_pos = past + jnp.arange(T)
    k_pos = jnp.arange(past + T)
    sb_all = jnp.concatenate([cache_sb[page_table].reshape(B, past, 2, H_SB, HEAD_DIM), sb_kv], axis=1)
    o_a = stick_breaking(q_a, q_pos, sb_all[:, :, 0], sb_all[:, :, 1], k_pos).reshape(B, T, SB_WIDTH)
    nsa_all = jnp.concatenate([cache_nsa[page_table].reshape(B, past, 4, KV_GROUPS, HEAD_DIM), nsa_kv], axis=1)
    ck, cv, c_pos, sel_map = compress(nsa_all[:, :, 0], nsa_all[:, :, 1], cmp_pe, cmp_w1, cmp_w2)
    n_sel = sel_map.shape[1]
    sk = sel_prep(nsa_all[:, :, 2], n_sel)
    sv = sel_prep(nsa_all[:, :, 3], n_sel)
    wb = cache_win.shape[1]
    win_all = jnp.concatenate([cache_win, win_kv], axis=1)
    w_pos = past - wb + jnp.arange(wb + T)
    o_b = nsa_attend(q_b, q_pos, gates, ck, cv, c_pos, sel_map, sk, sv,
                     win_all[:, :, 0], win_all[:, :, 1], w_pos)
    new_win = win_all[:, T:]
    return o_a, o_b, sb_kv, nsa_kv, new_win


def block_out(x, o_a, o_b, mix_g, w_out, ln1_g, ln1_b, wq_t, keys, uv_tab, ln2_g, ln2_b):
    m = jnp.concatenate([rms_norm(o_a, mix_g[:SB_WIDTH]), rms_norm(o_b, mix_g[SB_WIDTH:])], axis=-1) @ w_out
    h = layer_norm(ALPHA * x + m, ln1_g, ln1_b).reshape(-1, D_MODEL)
    e, g_t = peer_route_pallas(h, wq_t, keys)
    return peer_expert_pallas(e, h, g_t, uv_tab, ln2_g, ln2_b).reshape(x.shape)


def kernel(x_prompt, x_sample, cache_sb_kv, cache_nsa_kv, cache_win_kv, page_table, w_in, cmp_pe, cmp_w1, cmp_w2, mix_norm_g, w_out, ln1_g, ln1_b, peer_w_q, peer_sub_keys, peer_u, peer_v, ln2_g, ln2_b):
    l = 0
    w_in_pad = jnp.pad(w_in[l], ((0, 0), (0, IN_COLS_PAD - IN_COLS))).astype(jnp.bfloat16)
    o_a, o_b, sb_p, nsa_p, win_p = mixer_prompt(x_prompt, w_in_pad, cmp_pe[l], cmp_w1[l], cmp_w2[l])
    wq_t = peer_w_q[l].T.astype(jnp.bfloat16)
    keys = peer_sub_keys[l].reshape(2 * PEER_HEADS, N_KEYS, D_KEY // 2).astype(jnp.bfloat16)
    slab = lambda tab: tab.reshape(N_EXPERTS, VEC_ROWS, LANE)
    uv_tab = jnp.concatenate([slab(peer_u[l]), slab(peer_v[l])], axis=1).reshape(N_EXPERTS * SLAB, LANE)
    h_p = block_out(x_prompt, o_a, o_b, mix_norm_g[l], w_out[l], ln1_g[l], ln1_b[l],
                    wq_t, keys, uv_tab, ln2_g[l], ln2_b[l])
    o_a, o_b, sb_s, nsa_s, win_s = mixer_sample(x_sample, cache_sb_kv[l], cache_nsa_kv[l], cache_win_kv[l],
                                                page_table, w_in_pad, cmp_pe[l], cmp_w1[l], cmp_w2[l])
    h_s = block_out(x_sample, o_a, o_b, mix_norm_g[l], w_out[l], ln1_g[l], ln1_b[l],
                    wq_t, keys, uv_tab, ln2_g[l], ln2_b[l])
    return (h_p, h_s, sb_p[None], nsa_p[None], win_p[None], sb_s[None], nsa_s[None], win_s[None])
```

```python
import math
from functools import partial

import jax, jax.numpy as jnp
from jax import lax
import numpy as np
from jax.experimental import pallas as pl
from jax.experimental.pallas import tpu as pltpu

D_MODEL = 1024
HEAD_DIM = 64
MIX_WIDTH = D_MODEL
SB_WIDTH = MIX_WIDTH // 2
NSA_WIDTH = MIX_WIDTH - SB_WIDTH
H_SB = SB_WIDTH // HEAD_DIM
H_NSA = NSA_WIDTH // HEAD_DIM
KV_GROUPS = 2
HPG = H_NSA // KV_GROUPS
COMP_BLOCK = 32
COMP_STRIDE = 16
COMP_HID = 128
SEL_BLOCK = 64
TOP_N = 8
WINDOW = 512
Q_BLOCK = 128
PEER_HEADS = 8
N_KEYS = 128
N_EXPERTS = N_KEYS * N_KEYS
PEER_TOPK = 16
D_KEY = 256
TOK_CHUNK = 256
DEPTH = 1
ALPHA = (2.0 * DEPTH) ** 0.25
LN_EPS = 1e-5
NEG = -1e30
FORCE = 1e4
SCALE = HEAD_DIM ** -0.5

OFF_SBKV = SB_WIDTH
OFF_QB = 3 * SB_WIDTH
OFF_NSAKV = OFF_QB + NSA_WIDTH
OFF_WIN = OFF_NSAKV + 4 * KV_GROUPS * HEAD_DIM
OFF_GATE = OFF_WIN + 2 * KV_GROUPS * HEAD_DIM
IN_COLS = OFF_GATE + 3 * H_NSA

LANE = 128
IN_COLS_PAD = -(-IN_COLS // LANE) * LANE
PROJ_ROWS = 512


def _proj_kernel(x_ref, w_ref, z_ref):
    z_ref[...] = jnp.dot(x_ref[...].astype(jnp.bfloat16), w_ref[...],
                         preferred_element_type=jnp.float32)


def project_pallas(x2d, w_in_pad_bf16):
    n = x2d.shape[0]
    rows = min(PROJ_ROWS, n)
    return pl.pallas_call(
        _proj_kernel,
        grid=(n // rows,),
        in_specs=[pl.BlockSpec((rows, D_MODEL), lambda i: (i, 0)),
                  pl.BlockSpec((D_MODEL, IN_COLS_PAD), lambda i: (0, 0))],
        out_specs=pl.BlockSpec((rows, IN_COLS_PAD), lambda i: (i, 0)),
        out_shape=jax.ShapeDtypeStruct((n, IN_COLS_PAD), jnp.float32),
        compiler_params=pltpu.CompilerParams(
            dimension_semantics=("arbitrary",), vmem_limit_bytes=48 * 1024 * 1024),
        name="in_proj",
    )(x2d, w_in_pad_bf16)


N_SEL = 32
N_CMP_PAD = 128
SEL_TILE = 512
WIN_TILE = 128
ALIBI = [[2.0 ** (-8.0 * (g * HPG + r + 1) / H_NSA) for r in range(HPG)] for g in range(KV_GROUPS)]


def _dot_nt(a, b):
    return lax.dot_general(a, b, (((1,), (1,)), ((), ())), preferred_element_type=jnp.float32)


def _split_dot(a, b_exact):
    hi = a.astype(jnp.bfloat16)
    r1 = a - hi.astype(jnp.float32)
    mid = r1.astype(jnp.bfloat16)
    lo = (r1 - mid.astype(jnp.float32)).astype(jnp.bfloat16)
    d = lambda x: jnp.dot(x, b_exact, preferred_element_type=jnp.float32)
    return d(hi) + d(mid) + d(lo)


def _nsa_prompt_kernel(q_ref, gz_ref, ck_ref, cv_ref, nsa_ref, win_ref, o_ref, m_ref, l_ref, acc_ref):
    QB = Q_BLOCK
    qb = pl.program_id(1)
    q0 = qb * QB
    bf = jnp.bfloat16
    f32 = jnp.float32
    gate = jax.nn.sigmoid(gz_ref[0])
    row = lax.broadcasted_iota(jnp.int32, (QB, 1), 0)
    qpos = q0 + row

    def reset():
        m_ref[...] = jnp.full(m_ref.shape, NEG, f32)
        l_ref[...] = jnp.zeros(l_ref.shape, f32)
        acc_ref[...] = jnp.zeros(acc_ref.shape, f32)

    def flash_step(r, s, valid, v):
        s = jnp.where(valid, s, NEG)
        m_old = m_ref[r]
        m_new = jnp.maximum(m_old, s.max(-1, keepdims=True))
        p = jnp.where(valid, jnp.exp(s - m_new), 0.0)
        alpha = jnp.exp(m_old - m_new)
        l_ref[r] = alpha * l_ref[r] + p.sum(-1, keepdims=True)
        acc_ref[r] = alpha * acc_ref[r] + jnp.dot(p.astype(bf), v, preferred_element_type=f32)
        m_ref[r] = m_new
        return p

    def finish(g, branch, first):
        for r in range(HPG):
            h = g * HPG + r
            tot = l_ref[r]
            o = acc_ref[r] / jnp.where(tot > 0, tot, 1.0)
            o = gate[:, 3 * h + branch:3 * h + branch + 1] * o
            sl = (0, slice(None), slice(h * HEAD_DIM, (h + 1) * HEAD_DIM))
            if first:
                o_ref[sl] = o
            else:
                o_ref[sl] = o_ref[sl] + o

    for g in range(KV_GROUPS):
        qh = [(q_ref[0, :, (g * HPG + r) * HEAD_DIM:(g * HPG + r + 1) * HEAD_DIM] * SCALE).astype(bf)
              for r in range(HPG)]

        reset()
        n_i = lax.broadcasted_iota(jnp.int32, (1, N_CMP_PAD), 1)
        cpos = n_i * COMP_STRIDE + (COMP_BLOCK - 1)
        n_cmp = (pl.num_programs(1) * QB - COMP_BLOCK) // COMP_STRIDE + 1
        valid_c = (cpos <= qpos) & (n_i < n_cmp)
        dist_c = (qpos - cpos).astype(f32)
        ckg = ck_ref[0, g].astype(bf)
        cvg = cv_ref[0, g].astype(bf)
        psum = jnp.zeros((QB, N_CMP_PAD), f32)
        for r in range(HPG):
            s = _dot_nt(qh[r], ckg) - ALIBI[g][r] * dist_c
            p = flash_step(r, s, valid_c, cvg)
            tot = l_ref[r]
            psum = psum + p / jnp.where(tot > 0, tot, 1.0)
        finish(g, 0, True)

        nn = lax.broadcasted_iota(jnp.int32, (N_CMP_PAD, N_SEL), 0)
        jj = lax.broadcasted_iota(jnp.int32, (N_CMP_PAD, N_SEL), 1)
        sel_map = ((nn * COMP_STRIDE) // SEL_BLOCK == jj).astype(bf)
        imp = _split_dot(psum, sel_map)
        blk = lax.broadcasted_iota(jnp.int32, (QB, N_SEL), 1)
        forced = (blk == qpos // SEL_BLOCK) | (blk == 0)
        imp = jnp.where(forced, FORCE, jnp.where(blk * SEL_BLOCK <= qpos, imp, -1.0))
        chosen = jnp.zeros((QB, N_SEL), f32)
        for _ in range(TOP_N):
            mx = imp.max(-1, keepdims=True)
            first_max = jnp.where(imp == mx, blk, N_SEL).min(-1, keepdims=True)
            pick = blk == first_max
            chosen = jnp.where(pick, 1.0, chosen)
            imp = jnp.where(pick, -2.0, imp)
        chosen = chosen.astype(bf)

        reset()

        def sel_body(kt, carry):
            k0 = pl.multiple_of(kt * SEL_TILE, SEL_TILE)
            k = nsa_ref[0, pl.ds(k0, SEL_TILE), (4 + g) * HEAD_DIM:(5 + g) * HEAD_DIM].astype(bf)
            v = nsa_ref[0, pl.ds(k0, SEL_TILE), (6 + g) * HEAD_DIM:(7 + g) * HEAD_DIM].astype(bf)
            col = lax.broadcasted_iota(jnp.int32, (1, SEL_TILE), 1) + k0
            dist = qpos - col
            ej = lax.broadcasted_iota(jnp.int32, (N_SEL, SEL_TILE), 0)
            ec = lax.broadcasted_iota(jnp.int32, (N_SEL, SEL_TILE), 1) + k0
            expand = (ec // SEL_BLOCK == ej).astype(bf)
            in_block = jnp.dot(chosen, expand, preferred_element_type=f32)
            valid = (in_block > 0.5) & (dist >= 0)
            distf = dist.astype(f32)
            for r in range(HPG):
                s = _dot_nt(qh[r], k) - ALIBI[g][r] * distf
                flash_step(r, s, valid, v)
            return carry

        lax.fori_loop(0, (q0 + QB + SEL_TILE - 1) // SEL_TILE, sel_body, 0)
        finish(g, 1, False)

        reset()

        def win_body(kt, carry):
            k0 = pl.multiple_of(kt * WIN_TILE, WIN_TILE)
            k = win_ref[0, pl.ds(k0, WIN_TILE), g * HEAD_DIM:(g + 1) * HEAD_DIM].astype(bf)
            v = win_ref[0, pl.ds(k0, WIN_TILE), (2 + g) * HEAD_DIM:(3 + g) * HEAD_DIM].astype(bf)
            col = lax.broadcasted_iota(jnp.int32, (1, WIN_TILE), 1) + k0
            dist = qpos - col
            valid = (dist >= 0) & (dist < WINDOW)
            distf = dist.astype(f32)
            for r in range(HPG):
                s = _dot_nt(qh[r], k) - ALIBI[g][r] * distf
                flash_step(r, s, valid, v)
            return carry

        lax.fori_loop(jnp.maximum(qb - WINDOW // WIN_TILE, 0), qb + 1, win_body, 0)
        finish(g, 2, False)


def nsa_prompt_pallas(q_b, gate_z, ck, cv, nsa_kv, win_kv):
    B, T, _ = q_b.shape
    assert T % Q_BLOCK == 0 and T // SEL_BLOCK <= N_SEL and (T - COMP_BLOCK) // COMP_STRIDE + 1 <= N_CMP_PAD
    per_b = lambda b, i: (b, 0, 0)
    return pl.pallas_call(
        _nsa_prompt_kernel,
        grid=(B, T // Q_BLOCK),
        in_specs=[pl.BlockSpec((1, Q_BLOCK, NSA_WIDTH), lambda b, i: (b, i, 0)),
                  pl.BlockSpec((1, Q_BLOCK, 3 * H_NSA), lambda b, i: (b, i, 0)),
                  pl.BlockSpec((1, KV_GROUPS, N_CMP_PAD, HEAD_DIM), lambda b, i: (b, 0, 0, 0)),
                  pl.BlockSpec((1, KV_GROUPS, N_CMP_PAD, HEAD_DIM), lambda b, i: (b, 0, 0, 0)),
                  pl.BlockSpec((1, T, 4 * KV_GROUPS * HEAD_DIM), per_b),
                  pl.BlockSpec((1, T, 2 * KV_GROUPS * HEAD_DIM), per_b)],
        out_specs=pl.BlockSpec((1, Q_BLOCK, NSA_WIDTH), lambda b, i: (b, i, 0)),
        out_shape=jax.ShapeDtypeStruct((B, T, NSA_WIDTH), jnp.float32),
        scratch_shapes=[pltpu.VMEM((HPG, Q_BLOCK, 1), jnp.float32),
                        pltpu.VMEM((HPG, Q_BLOCK, 1), jnp.float32),
                        pltpu.VMEM((HPG, Q_BLOCK, HEAD_DIM), jnp.float32)],
        compiler_params=pltpu.CompilerParams(
            dimension_semantics=("arbitrary", "arbitrary"), vmem_limit_bytes=48 * 1024 * 1024),
        name="nsa_prompt",
    )(q_b, gate_z, ck, cv, nsa_kv, win_kv)


SB_TILE = 128
N_PICK = PEER_HEADS * PEER_TOPK


def _split2_dot(a, b_exact):
    hi = a.astype(jnp.bfloat16)
    lo = (a - hi.astype(jnp.float32)).astype(jnp.bfloat16)
    return (jnp.dot(hi, b_exact, preferred_element_type=jnp.float32)
            + jnp.dot(lo, b_exact, preferred_element_type=jnp.float32))


def _sb_prompt_kernel(q_ref, kv_ref, o_ref, carry_ref, acc_ref):
    TQ = TK = SB_TILE
    bf, f32 = jnp.bfloat16, jnp.float32
    qb = pl.program_id(1)
    qpos = qb * TQ + lax.broadcasted_iota(jnp.int32, (TQ, 1), 0)
    later = (lax.broadcasted_iota(jnp.int32, (TK, TK), 0) > lax.broadcasted_iota(jnp.int32, (TK, TK), 1)).astype(bf)
    carry_ref[...] = jnp.zeros(carry_ref.shape, f32)
    acc_ref[...] = jnp.zeros(acc_ref.shape, f32)

    def body(i, c):
        k0 = pl.multiple_of((qb - i) * TK, TK)
        valid = (lax.broadcasted_iota(jnp.int32, (1, TK), 1) + k0) < qpos
        for h in range(H_SB):
            q = (q_ref[0, :, h * HEAD_DIM:(h + 1) * HEAD_DIM] * SCALE).astype(bf)
            k = kv_ref[0, pl.ds(k0, TK), h * HEAD_DIM:(h + 1) * HEAD_DIM].astype(bf)
            v = kv_ref[0, pl.ds(k0, TK), SB_WIDTH + h * HEAD_DIM:SB_WIDTH + (h + 1) * HEAD_DIM].astype(bf)
            z = _dot_nt(q, k)
            soft = jnp.log1p(jnp.exp(-jnp.abs(z)))
            log_sig = jnp.minimum(z, 0.0) - soft
            log_keep = jnp.where(valid, log_sig - z, 0.0)
            after = _split2_dot(log_keep, later) + carry_ref[h]
            w = jnp.where(valid, jnp.exp(log_sig + after), 0.0)
            acc_ref[h] = acc_ref[h] + jnp.dot(w.astype(bf), v, preferred_element_type=f32)
            carry_ref[h] = carry_ref[h] + log_keep.sum(-1, keepdims=True)
        return c

    lax.fori_loop(0, qb + 1, body, 0)
    for h in range(H_SB):
        o_ref[0, :, h * HEAD_DIM:(h + 1) * HEAD_DIM] = acc_ref[h]


def sb_prompt_pallas(q_a, sb_kv):
    B, T, _ = q_a.shape
    assert T % SB_TILE == 0
    return pl.pallas_call(
        _sb_prompt_kernel,
        grid=(B, T // SB_TILE),
        in_specs=[pl.BlockSpec((1, SB_TILE, SB_WIDTH), lambda b, i: (b, i, 0)),
                  pl.BlockSpec((1, T, 2 * SB_WIDTH), lambda b, i: (b, 0, 0))],
        out_specs=pl.BlockSpec((1, SB_TILE, SB_WIDTH), lambda b, i: (b, i, 0)),
        out_shape=jax.ShapeDtypeStruct((B, T, SB_WIDTH), jnp.float32),
        scratch_shapes=[pltpu.VMEM((H_SB, SB_TILE, 1), jnp.float32),
                        pltpu.VMEM((H_SB, SB_TILE, HEAD_DIM), jnp.float32)],
        compiler_params=pltpu.CompilerParams(
            dimension_semantics=("arbitrary", "arbitrary"), vmem_limit_bytes=48 * 1024 * 1024),
        name="sb_prompt",
    )(q_a, sb_kv)


ROUTE_TOK = 256
N_CAND = PEER_TOPK * PEER_TOPK


def _top_rows(s, n_top, vals_ref, idx_ref):
    rows = lax.broadcasted_iota(jnp.int32, s.shape, 0)
    for a in range(n_top):
        m = s.max(0, keepdims=True)
        ix = jnp.where(s == m, rows, s.shape[0]).min(0, keepdims=True)
        s = jnp.where(rows == ix, -jnp.inf, s)
        vals_ref[a:a + 1, :] = m
        idx_ref[a:a + 1, :] = ix


def _peer_route_kernel(h_ref, wq_t_ref, keys_ref, e_ref, g_ref, s1_ref, i1_ref, s2_ref, i2_ref, ts_ref, ti_ref,
                       et_ref):
    bf, f32 = jnp.bfloat16, jnp.float32
    q_t = _dot_nt(wq_t_ref[...], h_ref[...].astype(bf))
    half = D_KEY // 2
    for p in range(PEER_HEADS):
        for hf, (sv, si) in enumerate(((s1_ref, i1_ref), (s2_ref, i2_ref))):
            r0 = (2 * p + hf) * half
            s = jnp.dot(keys_ref[2 * p + hf], q_t[r0:r0 + half, :].astype(bf), preferred_element_type=f32)
            _top_rows(s, PEER_TOPK, sv, si)
        s2 = s2_ref[...]
        cand = jnp.concatenate([s1_ref[a:a + 1, :] + s2 for a in range(PEER_TOPK)], axis=0)
        _top_rows(cand, PEER_TOPK, ts_ref, ti_ref)
        ti = ti_ref[...]
        a_of, b_of = ti // PEER_TOPK, ti % PEER_TOPK
        k1 = jnp.zeros(ti.shape, jnp.int32)
        k2 = jnp.zeros(ti.shape, jnp.int32)
        for a in range(PEER_TOPK):
            k1 = jnp.where(a_of == a, i1_ref[a:a + 1, :], k1)
            k2 = jnp.where(b_of == a, i2_ref[a:a + 1, :], k2)
        ts = ts_ref[...]
        ex = jnp.exp(ts - ts.max(0, keepdims=True))
        et_ref[p * PEER_TOPK:(p + 1) * PEER_TOPK, :] = k1 * N_KEYS + k2
        g_ref[p * PEER_TOPK:(p + 1) * PEER_TOPK, :] = ex / ex.sum(0, keepdims=True)
    e_ref[...] = et_ref[...].T


def peer_route_pallas(h, wq_t, keys):
    n = h.shape[0]
    assert n % ROUTE_TOK == 0
    tn = ROUTE_TOK
    top = lambda dt: pltpu.VMEM((PEER_TOPK, tn), dt)
    return pl.pallas_call(
        _peer_route_kernel,
        grid=(n // tn,),
        in_specs=[pl.BlockSpec((tn, D_MODEL), lambda i: (i, 0)),
                  pl.BlockSpec((PEER_HEADS * D_KEY, D_MODEL), lambda i: (0, 0)),
                  pl.BlockSpec((2 * PEER_HEADS, N_KEYS, D_KEY // 2), lambda i: (0, 0, 0))],
        out_specs=[pl.BlockSpec((tn, N_PICK), lambda i: (i, 0)),
                   pl.BlockSpec((N_PICK, tn), lambda i: (0, i))],
        out_shape=[jax.ShapeDtypeStruct((n, N_PICK), jnp.int32),
                   jax.ShapeDtypeStruct((N_PICK, n), jnp.float32)],
        scratch_shapes=[top(jnp.float32), top(jnp.int32), top(jnp.float32), top(jnp.int32),
                        top(jnp.float32), top(jnp.int32), pltpu.VMEM((N_PICK, tn), jnp.int32)],
        compiler_params=pltpu.CompilerParams(
            dimension_semantics=("arbitrary",), vmem_limit_bytes=48 * 1024 * 1024),
        name="peer_route",
    )(h, wq_t, keys)


PEER_TOK_BLOCK = 128
PEER_SLOTS = 4
VEC_ROWS = D_MODEL // LANE
HALF_ROWS = VEC_ROWS // 2
SLAB = 2 * HALF_ROWS


def pack_expert_slabs(u_tab, v_tab):
    def pack(tab):
        bits = lax.bitcast_convert_type(tab.astype(jnp.bfloat16), jnp.uint16).astype(jnp.uint32)
        bits = bits.reshape(N_EXPERTS, 2, HALF_ROWS, LANE)
        return bits[:, 0] | (bits[:, 1] << 16)
    return jnp.concatenate([pack(u_tab), pack(v_tab)], axis=1).reshape(N_EXPERTS * SLAB, LANE)


def _peer_expert_kernel(e_hbm, h_ref, coef_ref, uv_hbm, g_ref, b_ref, y_ref,
                        e_smem, uvbuf, f_ref, sem_e, sem_rows):
    TB = coef_ref.shape[1]
    i = pl.program_id(0)
    ids = pltpu.make_async_copy(e_hbm.at[pl.ds(i * TB, TB), :], e_smem, sem_e)
    ids.start()
    ids.wait()

    def issue(t, slot):
        for k in range(N_PICK):
            row0 = pl.multiple_of(e_smem[t, k] * SLAB, SLAB)
            pltpu.make_async_copy(uv_hbm.at[pl.ds(row0, SLAB), :], uvbuf.at[slot, pl.ds(k * SLAB, SLAB), :],
                                  sem_rows.at[slot]).start()

    def wait_rows(slot):
        pltpu.make_async_copy(uv_hbm.at[pl.ds(0, N_PICK * SLAB), :], uvbuf.at[slot], sem_rows.at[slot]).wait()

    coef_t = coef_ref[...]
    tok = lax.broadcasted_iota(jnp.int32, coef_t.shape, 1)
    low = lambda w: lax.bitcast_convert_type(w << 16, jnp.float32)
    high = lambda w: lax.bitcast_convert_type(w & jnp.uint32(0xFFFF0000), jnp.float32)

    def compute(t, slot):
        rows = lambda r: uvbuf[slot, pl.ds(r, N_PICK, stride=SLAB), :]
        r0 = pl.multiple_of(t * VEC_ROWS, VEC_ROWS)
        h = h_ref[pl.ds(r0, VEC_ROWS), :]
        part = None
        for r in range(HALF_ROWS):
            w = rows(r)
            term = low(w) * h[r:r + 1, :] + high(w) * h[HALF_ROWS + r:HALF_ROWS + r + 1, :]
            part = term if part is None else part + term
        a = jnp.sum(part, axis=-1, keepdims=True)
        c = jnp.sum(jnp.where(tok == t, coef_t, 0.0), axis=-1, keepdims=True)
        wgt = c * jax.nn.gelu(a)
        for r in range(HALF_ROWS):
            w = rows(HALF_ROWS + r)
            f_ref[pl.ds(r0 + r, 1), :] = jnp.sum(low(w) * wgt, axis=0, keepdims=True)
            f_ref[pl.ds(r0 + HALF_ROWS + r, 1), :] = jnp.sum(high(w) * wgt, axis=0, keepdims=True)

    ahead = PEER_SLOTS - 1
    for t in range(ahead):
        issue(t, t)

    def body(j, carry):
        for s in range(PEER_SLOTS):
            t = PEER_SLOTS * j + s

            @pl.when(t + ahead < TB)
            def _():
                issue(t + ahead, (s + ahead) % PEER_SLOTS)

            wait_rows(s)
            compute(t, s)
        return carry

    lax.fori_loop(0, TB // PEER_SLOTS, body, 0)
    x = (ALPHA * h_ref[...] + f_ref[...]).reshape(TB, VEC_ROWS, LANE)
    mean = lambda v: v.sum(axis=2, keepdims=True).sum(axis=1, keepdims=True) * (1.0 / D_MODEL)
    mu = mean(x)
    var = mean(jnp.square(x - mu))
    y = (x - mu) * lax.rsqrt(var + LN_EPS) * g_ref[...][None] + b_ref[...][None]
    y_ref[...] = y.reshape(TB * VEC_ROWS, LANE)


def peer_expert_pallas(e, h, coef_t, uv_slabs, ln_g, ln_b):
    n = h.shape[0]
    tb = min(PEER_TOK_BLOCK, n)
    assert n % tb == 0 and tb % PEER_SLOTS == 0
    fixed = lambda i: (0, 0)
    y = pl.pallas_call(
        _peer_expert_kernel,
        grid=(n // tb,),
        in_specs=[pl.BlockSpec(memory_space=pl.ANY),
                  pl.BlockSpec((tb * VEC_ROWS, LANE), lambda i: (i, 0)),
                  pl.BlockSpec((N_PICK, tb), lambda i: (0, i)),
                  pl.BlockSpec(memory_space=pl.ANY),
                  pl.BlockSpec((VEC_ROWS, LANE), fixed),
                  pl.BlockSpec((VEC_ROWS, LANE), fixed)],
        out_specs=pl.BlockSpec((tb * VEC_ROWS, LANE), lambda i: (i, 0)),
        out_shape=jax.ShapeDtypeStruct((n * VEC_ROWS, LANE), jnp.float32),
        scratch_shapes=[pltpu.SMEM((tb, N_PICK), jnp.int32),
                        pltpu.VMEM((PEER_SLOTS, N_PICK * SLAB, LANE), jnp.uint32),
                        pltpu.VMEM((tb * VEC_ROWS, LANE), jnp.float32),
                        pltpu.SemaphoreType.DMA,
                        pltpu.SemaphoreType.DMA((PEER_SLOTS,))],
        compiler_params=pltpu.CompilerParams(dimension_semantics=("arbitrary",)),
        name="peer_experts",
    )(e, h.reshape(n * VEC_ROWS, LANE), coef_t, uv_slabs,
      ln_g.reshape(VEC_ROWS, LANE), ln_b.reshape(VEC_ROWS, LANE))
    return y.reshape(n, D_MODEL)


MID_ROWS = 512


def _mid_kernel(x_ref, oa_ref, ob_ref, mg_ref, w_ref, g_ref, b_ref, h_ref):
    bf, f32 = jnp.bfloat16, jnp.float32

    def normed(o, g):
        return (o * lax.rsqrt(jnp.mean(jnp.square(o), -1, keepdims=True) + LN_EPS) * g).astype(bf)

    m = (jnp.dot(normed(oa_ref[...], mg_ref[:, :SB_WIDTH]), w_ref[:SB_WIDTH, :], preferred_element_type=f32)
         + jnp.dot(normed(ob_ref[...], mg_ref[:, SB_WIDTH:]), w_ref[SB_WIDTH:, :], preferred_element_type=f32))
    x = ALPHA * x_ref[...] + m
    mu = x.mean(-1, keepdims=True)
    var = jnp.square(x - mu).mean(-1, keepdims=True)
    h_ref[...] = (x - mu) * lax.rsqrt(var + LN_EPS) * g_ref[...] + b_ref[...]


def mixer_out_pallas(x, o_a, o_b, mix_g, w_out_bf16, ln_g, ln_b):
    n = x.shape[0]
    rows = min(MID_ROWS, n)
    assert n % rows == 0
    row = lambda i: (i, 0)
    fixed = lambda i: (0, 0)
    return pl.pallas_call(
        _mid_kernel,
        grid=(n // rows,),
        in_specs=[pl.BlockSpec((rows, D_MODEL), row),
                  pl.BlockSpec((rows, SB_WIDTH), row),
                  pl.BlockSpec((rows, NSA_WIDTH), row),
                  pl.BlockSpec((1, MIX_WIDTH), fixed),
                  pl.BlockSpec((MIX_WIDTH, D_MODEL), fixed),
                  pl.BlockSpec((1, D_MODEL), fixed),
                  pl.BlockSpec((1, D_MODEL), fixed)],
        out_specs=pl.BlockSpec((rows, D_MODEL), row),
        out_shape=jax.ShapeDtypeStruct((n, D_MODEL), jnp.float32),
        compiler_params=pltpu.CompilerParams(
            dimension_semantics=("arbitrary",), vmem_limit_bytes=48 * 1024 * 1024),
        name="mixer_out",
    )(x, o_a, o_b, mix_g.reshape(1, MIX_WIDTH), w_out_bf16, ln_g.reshape(1, D_MODEL), ln_b.reshape(1, D_MODEL))


def layer_norm(x, g, b):
    xf = x.astype(jnp.float32)
    mu = xf.mean(-1, keepdims=True)
    var = jnp.square(xf - mu).mean(-1, keepdims=True)
    return ((xf - mu) * lax.rsqrt(var + LN_EPS) * g + b).astype(x.dtype)


def rms_norm(x, g):
    xf = x.astype(jnp.float32)
    return (xf * lax.rsqrt(jnp.square(xf).mean(-1, keepdims=True) + LN_EPS) * g).astype(x.dtype)


def masked_softmax(s, mask):
    s = jnp.where(mask, s, NEG)
    e = jnp.exp(s - s.max(-1, keepdims=True)) * mask
    tot = e.sum(-1, keepdims=True)
    return e / jnp.where(tot > 0, tot, 1.0)


def alibi_slopes():
    i = jnp.arange(1, H_NSA + 1, dtype=jnp.float32)
    return (2.0 ** (-8.0 * i / H_NSA)).reshape(KV_GROUPS, HPG)


def project(x, w_in_pad):
    B, T, _ = x.shape
    z = project_pallas(x.reshape(B * T, D_MODEL), w_in_pad).reshape(B, T, IN_COLS_PAD)
    q_a = z[..., :SB_WIDTH].reshape(B, T, H_SB, HEAD_DIM)
    sb_kv = z[..., OFF_SBKV:OFF_QB].reshape(B, T, 2, H_SB, HEAD_DIM)
    q_b = z[..., OFF_QB:OFF_NSAKV].reshape(B, T, H_NSA, HEAD_DIM)
    nsa_kv = z[..., OFF_NSAKV:OFF_WIN].reshape(B, T, 4, KV_GROUPS, HEAD_DIM)
    win_kv = z[..., OFF_WIN:OFF_GATE].reshape(B, T, 2, KV_GROUPS, HEAD_DIM)
    gates = jax.nn.sigmoid(z[..., OFF_GATE:IN_COLS]).reshape(B, T, H_NSA, 3)
    return q_a, sb_kv, q_b, nsa_kv, win_kv, gates


def stick_breaking(q, q_pos, k, v, k_pos):
    z = jnp.einsum('bthd,bshd->bhts', q, k).astype(jnp.float32) * SCALE
    mask = k_pos[None, :] < q_pos[:, None]
    log_keep = jnp.where(mask, jax.nn.log_sigmoid(-z), 0.0)
    cum = jnp.cumsum(log_keep, axis=-1)
    after = cum[..., -1:] - cum
    w = jnp.where(mask, jnp.exp(jax.nn.log_sigmoid(z) + after), 0.0)
    return jnp.einsum('bhts,bshd->bthd', w.astype(v.dtype), v)


def compress(kc, vc, cmp_pe, cmp_w1, cmp_w2):
    L = kc.shape[1]
    n_c = (L - COMP_BLOCK) // COMP_STRIDE + 1
    starts = jnp.arange(n_c) * COMP_STRIDE
    idx = starts[:, None] + jnp.arange(COMP_BLOCK)[None, :]

    def phi(t, j):
        blk = t[:, idx] + cmp_pe[j][None, None, :, None, :]
        hdn = jax.nn.gelu(jnp.einsum('bnlgd,ldh->bngh', blk, cmp_w1[j]))
        return jnp.einsum('bngh,hd->bngd', hdn, cmp_w2[j])

    ck = phi(kc, 0)
    cv = phi(vc, 1)
    c_pos = starts + COMP_BLOCK - 1
    n_sel = -(-L // SEL_BLOCK)
    sel_map = jax.nn.one_hot(starts // SEL_BLOCK, n_sel, dtype=jnp.float32)
    return ck, cv, c_pos, sel_map


def sel_prep(k, n_sel):
    pad = n_sel * SEL_BLOCK - k.shape[1]
    return jnp.pad(k, ((0, 0), (0, pad), (0, 0), (0, 0))).transpose(0, 2, 1, 3)


def gather_rows(a, idx):
    return jax.vmap(jax.vmap(lambda r, i: r[i]))(a, idx)


def nsa_attend(q, q_pos, gates, ck, cv, c_pos, sel_map, sk, sv, wk, wv, w_pos):
    B, Tq = q.shape[:2]
    qg = q.reshape(B, Tq, KV_GROUPS, HPG, HEAD_DIM)
    slopes = alibi_slopes()[None, :, :, None, None]
    qp = q_pos.astype(jnp.float32)
    s = jnp.einsum('btgrd,bngd->bgrtn', qg, ck).astype(jnp.float32) * SCALE
    s = s - slopes * (qp[:, None] - c_pos[None, :].astype(jnp.float32))
    p_c = masked_softmax(s, c_pos[None, :] <= q_pos[:, None])
    o_c = jnp.einsum('bgrtn,bngd->btgrd', p_c.astype(cv.dtype), cv)
    imp = jnp.einsum('bgrtn,nj->bgtj', p_c, sel_map)
    n_sel = sel_map.shape[1]
    blk = jnp.arange(n_sel)
    valid = blk[None, :] * SEL_BLOCK <= q_pos[:, None]
    forced = (blk[None, :] == (q_pos // SEL_BLOCK)[:, None]) | (blk[None, :] == 0)
    imp = jnp.where(forced, FORCE, jnp.where(valid, imp, -1.0))
    n_top = min(TOP_N, n_sel)
    _, idx = lax.top_k(imp, n_top)
    kpos = (idx[..., None] * SEL_BLOCK + jnp.arange(SEL_BLOCK)).reshape(B, KV_GROUPS, Tq, n_top * SEL_BLOCK)
    flat = kpos.reshape(B, KV_GROUPS, -1)
    ksel = gather_rows(sk, flat).reshape(B, KV_GROUPS, Tq, n_top * SEL_BLOCK, HEAD_DIM)
    vsel = gather_rows(sv, flat).reshape(B, KV_GROUPS, Tq, n_top * SEL_BLOCK, HEAD_DIM)
    dist_s = (q_pos[None, None, :, None] - kpos)[:, :, None]
    s = jnp.einsum('btgrd,bgtsd->bgrts', qg, ksel).astype(jnp.float32) * SCALE
    s = s - slopes * dist_s.astype(jnp.float32)
    p_s = masked_softmax(s, dist_s >= 0)
    o_s = jnp.einsum('bgrts,bgtsd->btgrd', p_s.astype(vsel.dtype), vsel)
    dist_w = q_pos[:, None] - w_pos[None, :]
    mask_w = (dist_w >= 0) & (dist_w < WINDOW) & (w_pos[None, :] >= 0)
    s = jnp.einsum('btgrd,bsgd->bgrts', qg, wk).astype(jnp.float32) * SCALE
    s = s - slopes * dist_w.astype(jnp.float32)
    p_w = masked_softmax(s, mask_w)
    o_w = jnp.einsum('bgrts,bsgd->btgrd', p_w.astype(wv.dtype), wv)
    g = gates.reshape(B, Tq, KV_GROUPS, HPG, 3)
    o = g[..., 0:1] * o_c + g[..., 1:2] * o_s + g[..., 2:3] * o_w
    return o.reshape(B, Tq, NSA_WIDTH)


def mixer_prompt(x, w_in_pad, cmp_pe, cmp_w1, cmp_w2):
    B, T, _ = x.shape
    z = project_pallas(x.reshape(B * T, D_MODEL), w_in_pad).reshape(B, T, IN_COLS_PAD)
    sb_kv = z[..., OFF_SBKV:OFF_QB].reshape(B, T, 2, H_SB, HEAD_DIM)
    nsa_kv = z[..., OFF_NSAKV:OFF_WIN].reshape(B, T, 4, KV_GROUPS, HEAD_DIM)
    win_kv = z[..., OFF_WIN:OFF_GATE].reshape(B, T, 2, KV_GROUPS, HEAD_DIM)
    ck, cv, c_pos, sel_map = compress(nsa_kv[:, :, 0], nsa_kv[:, :, 1], cmp_pe, cmp_w1, cmp_w2)
    pad_c = lambda c: jnp.pad(c, ((0, 0), (0, N_CMP_PAD - c.shape[1]), (0, 0), (0, 0))).transpose(0, 2, 1, 3)
    o_b = nsa_prompt_pallas(z[..., OFF_QB:OFF_NSAKV], z[..., OFF_GATE:IN_COLS], pad_c(ck), pad_c(cv),
                            z[..., OFF_NSAKV:OFF_WIN], z[..., OFF_WIN:OFF_GATE])
    o_a = sb_prompt_pallas(z[..., :SB_WIDTH], z[..., OFF_SBKV:OFF_QB])
    new_win = win_kv[:, T - min(WINDOW, T):]
    return o_a, o_b, sb_kv, nsa_kv, new_win


def mixer_sample(x, cache_sb, cache_nsa, cache_win, page_table, w_in_pad, cmp_pe, cmp_w1, cmp_w2):
    B, T, _ = x.shape
    past = page_table.shape[1] * cache_sb.shape[1]
    q_a, sb_kv, q_b, nsa_kv, win_kv, gates = project(x, w_in_pad)
    q_pos = past + jnp.arange(T)
    k_pos = jnp.arange(past + T)
    sb_all = jnp.concatenate([cache_sb[page_table].reshape(B, past, 2, H_SB, HEAD_DIM), sb_kv], axis=1)
    o_a = stick_breaking(q_a, q_pos, sb_all[:, :, 0], sb_all[:, :, 1], k_pos).reshape(B, T, SB_WIDTH)
    nsa_all = jnp.concatenate([cache_nsa[page_table].reshape(B, past, 4, KV_GROUPS, HEAD_DIM), nsa_kv], axis=1)
    ck, cv, c_pos, sel_map = compress(nsa_all[:, :, 0], nsa_all[:, :, 1], cmp_pe, cmp_w1, cmp_w2)
    n_sel = sel_map.shape[1]
    sk = sel_prep(nsa_all[:, :, 2], n_sel)
    sv = sel_prep(nsa_all[:, :, 3], n_sel)
    wb = cache_win.shape[1]
    win_all = jnp.concatenate([cache_win, win_kv], axis=1)
    w_pos = past - wb + jnp.arange(wb + T)
    o_b = nsa_attend(q_b, q_pos, gates, ck, cv, c_pos, sel_map, sk, sv,
                     win_all[:, :, 0], win_all[:, :, 1], w_pos)
    new_win = win_all[:, T:]
    return o_a, o_b, sb_kv, nsa_kv, new_win


def block_out(x, o_a, o_b, mix_g, w_out, ln1_g, ln1_b, wq_t, keys, uv_tab, ln2_g, ln2_b):
    h = mixer_out_pallas(x.reshape(-1, D_MODEL), o_a.reshape(-1, SB_WIDTH), o_b.reshape(-1, NSA_WIDTH),
                         mix_g, w_out, ln1_g, ln1_b)
    e, g_t = peer_route_pallas(h, wq_t, keys)
    return peer_expert_pallas(e, h, g_t, uv_tab, ln2_g, ln2_b).reshape(x.shape)


def kernel(x_prompt, x_sample, cache_sb_kv, cache_nsa_kv, cache_win_kv, page_table, w_in, cmp_pe, cmp_w1, cmp_w2, mix_norm_g, w_out, ln1_g, ln1_b, peer_w_q, peer_sub_keys, peer_u, peer_v, ln2_g, ln2_b):
    l = 0
    w_in_pad = jnp.pad(w_in[l], ((0, 0), (0, IN_COLS_PAD - IN_COLS))).astype(jnp.bfloat16)
    o_a, o_b, sb_p, nsa_p, win_p = mixer_prompt(x_prompt, w_in_pad, cmp_pe[l], cmp_w1[l], cmp_w2[l])
    wq_t = peer_w_q[l].T.astype(jnp.bfloat16)
    keys = peer_sub_keys[l].reshape(2 * PEER_HEADS, N_KEYS, D_KEY // 2).astype(jnp.bfloat16)
    uv_tab = pack_expert_slabs(peer_u[l], peer_v[l])
    w_out_bf = w_out[l].astype(jnp.bfloat16)
    h_p = block_out(x_prompt, o_a, o_b, mix_norm_g[l], w_out_bf, ln1_g[l], ln1_b[l],
                    wq_t, keys, uv_tab, ln2_g[l], ln2_b[l])
    o_a, o_b, sb_s, nsa_s, win_s = mixer_sample(x_sample, cache_sb_kv[l], cache_nsa_kv[l], cache_win_kv[l],
                                                page_table, w_in_pad, cmp_pe[l], cmp_w1[l], cmp_w2[l])
    h_s = block_out(x_sample, o_a, o_b, mix_norm_g[l], w_out_bf, ln1_g[l], ln1_b[l],
                    wq_t, keys, uv_tab, ln2_g[l], ln2_b[l])
    return (h_p, h_s, sb_p[None], nsa_p[None], win_p[None], sb_s[None], nsa_s[None], win_s[None])
```

```python
import jax, jax.numpy as jnp
from jax import lax
from jax.experimental import pallas as pl
from jax.experimental.pallas import tpu as pltpu

D_MODEL = 1024
HEAD_DIM = 64
MIX_WIDTH = D_MODEL
SB_WIDTH = MIX_WIDTH // 2
NSA_WIDTH = MIX_WIDTH - SB_WIDTH
H_SB = SB_WIDTH // HEAD_DIM
H_NSA = NSA_WIDTH // HEAD_DIM
KV_GROUPS = 2
HPG = H_NSA // KV_GROUPS
COMP_BLOCK = 32
COMP_STRIDE = 16
COMP_HID = 128
SEL_BLOCK = 64
TOP_N = 8
WINDOW = 512
Q_BLOCK = 128
PEER_HEADS = 8
N_KEYS = 128
N_EXPERTS = N_KEYS * N_KEYS
PEER_TOPK = 16
D_KEY = 256
TOK_CHUNK = 256
DEPTH = 1
ALPHA = (2.0 * DEPTH) ** 0.25
LN_EPS = 1e-5
NEG = -1e30
FORCE = 1e4
SCALE = HEAD_DIM ** -0.5

OFF_SBKV = SB_WIDTH
OFF_QB = 3 * SB_WIDTH
OFF_NSAKV = OFF_QB + NSA_WIDTH
OFF_WIN = OFF_NSAKV + 4 * KV_GROUPS * HEAD_DIM
OFF_GATE = OFF_WIN + 2 * KV_GROUPS * HEAD_DIM
IN_COLS = OFF_GATE + 3 * H_NSA

LANE = 128
IN_COLS_PAD = -(-IN_COLS // LANE) * LANE
PROJ_ROWS = 512


def _proj_kernel(x_ref, w_ref, z_ref):
    z_ref[...] = jnp.dot(x_ref[...].astype(jnp.bfloat16), w_ref[...],
                         preferred_element_type=jnp.float32)


def project_pallas(x2d, w_in_pad_bf16):
    n = x2d.shape[0]
    rows = min(PROJ_ROWS, n)
    return pl.pallas_call(
        _proj_kernel,
        grid=(n // rows,),
        in_specs=[pl.BlockSpec((rows, D_MODEL), lambda i: (i, 0)),
                  pl.BlockSpec((D_MODEL, IN_COLS_PAD), lambda i: (0, 0))],
        out_specs=pl.BlockSpec((rows, IN_COLS_PAD), lambda i: (i, 0)),
        out_shape=jax.ShapeDtypeStruct((n, IN_COLS_PAD), jnp.float32),
        compiler_params=pltpu.CompilerParams(
            dimension_semantics=("arbitrary",), vmem_limit_bytes=48 * 1024 * 1024),
        name="in_proj",
    )(x2d, w_in_pad_bf16)


N_SEL = 32
N_CMP_PAD = 128
SEL_TILE = 512
WIN_TILE = 128
ALIBI = [[2.0 ** (-8.0 * (g * HPG + r + 1) / H_NSA) for r in range(HPG)] for g in range(KV_GROUPS)]


def _dot_nt(a, b):
    return lax.dot_general(a, b, (((1,), (1,)), ((), ())), preferred_element_type=jnp.float32)


def _split_dot(a, b_exact):
    hi = a.astype(jnp.bfloat16)
    r1 = a - hi.astype(jnp.float32)
    mid = r1.astype(jnp.bfloat16)
    lo = (r1 - mid.astype(jnp.float32)).astype(jnp.bfloat16)
    d = lambda x: jnp.dot(x, b_exact, preferred_element_type=jnp.float32)
    return d(hi) + d(mid) + d(lo)


def _nsa_prompt_kernel(q_ref, gz_ref, ck_ref, cv_ref, nsa_ref, win_ref, o_ref, m_ref, l_ref, acc_ref):
    QB = Q_BLOCK
    qb = pl.program_id(1)
    q0 = qb * QB
    bf = jnp.bfloat16
    f32 = jnp.float32
    gate = jax.nn.sigmoid(gz_ref[0])
    qpos = q0 + lax.broadcasted_iota(jnp.int32, (QB, 1), 0)
    qh = [(q_ref[0, :, h * HEAD_DIM:(h + 1) * HEAD_DIM] * SCALE).astype(bf) for h in range(H_NSA)]

    def attend(qs, k, v, biases, valid):
        n = range(len(qs))
        s = [jnp.where(valid, _dot_nt(qs[i], k) + biases[i], NEG) for i in n]
        e = [jnp.where(valid, jnp.exp(s[i] - s[i].max(-1, keepdims=True)), 0.0) for i in n]
        tot = [e[i].sum(-1, keepdims=True) for i in n]
        inv = [1.0 / jnp.where(tot[i] > 0, tot[i], 1.0) for i in n]
        o = [jnp.dot(e[i].astype(bf), v, preferred_element_type=f32) * inv[i] for i in n]
        return o, [e[i] * inv[i] for i in n]

    def emit(h, branch, o, first):
        o = gate[:, 3 * h + branch:3 * h + branch + 1] * o
        sl = (0, slice(None), slice(h * HEAD_DIM, (h + 1) * HEAD_DIM))
        o_ref[sl] = o if first else o_ref[sl] + o

    chosen = []
    for g in range(KV_GROUPS):
        n_i = lax.broadcasted_iota(jnp.int32, (1, N_CMP_PAD), 1)
        cpos = n_i * COMP_STRIDE + (COMP_BLOCK - 1)
        n_cmp = (pl.num_programs(1) * QB - COMP_BLOCK) // COMP_STRIDE + 1
        valid_c = (cpos <= qpos) & (n_i < n_cmp)
        dist_c = (qpos - cpos).astype(f32)
        ckg = ck_ref[0, g].astype(bf)
        cvg = cv_ref[0, g].astype(bf)
        o_c, p_c = attend(qh[g * HPG:(g + 1) * HPG], ckg, cvg, [-ALIBI[g][r] * dist_c for r in range(HPG)], valid_c)
        psum = p_c[0]
        for r in range(HPG):
            emit(g * HPG + r, 0, o_c[r], True)
            if r:
                psum = psum + p_c[r]

        nn = lax.broadcasted_iota(jnp.int32, (N_CMP_PAD, N_SEL), 0)
        jj = lax.broadcasted_iota(jnp.int32, (N_CMP_PAD, N_SEL), 1)
        sel_map = ((nn * COMP_STRIDE) // SEL_BLOCK == jj).astype(bf)
        imp = _split_dot(psum, sel_map)
        blk = lax.broadcasted_iota(jnp.int32, (QB, N_SEL), 1)
        forced = (blk == qpos // SEL_BLOCK) | (blk == 0)
        imp = jnp.where(forced, FORCE, jnp.where(blk * SEL_BLOCK <= qpos, imp, -1.0))
        picked = jnp.zeros((QB, N_SEL), f32)
        for _ in range(TOP_N):
            mx = imp.max(-1, keepdims=True)
            first_max = jnp.where(imp == mx, blk, N_SEL).min(-1, keepdims=True)
            pick = blk == first_max
            picked = jnp.where(pick, 1.0, picked)
            imp = jnp.where(pick, -2.0, imp)
        chosen.append(picked.astype(bf))

        k0 = pl.multiple_of(jnp.maximum(q0 - WINDOW, 0), QB)
        k = win_ref[0, pl.ds(k0, WINDOW + QB), g * HEAD_DIM:(g + 1) * HEAD_DIM].astype(bf)
        v = win_ref[0, pl.ds(k0, WINDOW + QB), (2 + g) * HEAD_DIM:(3 + g) * HEAD_DIM].astype(bf)
        dist = qpos - (lax.broadcasted_iota(jnp.int32, (1, WINDOW + QB), 1) + k0)
        valid_w = (dist >= 0) & (dist < WINDOW)
        dist_w = dist.astype(f32)
        o_w, _ = attend(qh[g * HPG:(g + 1) * HPG], k, v, [-ALIBI[g][r] * dist_w for r in range(HPG)], valid_w)
        for r in range(HPG):
            emit(g * HPG + r, 2, o_w[r], False)

    m_ref[...] = jnp.full(m_ref.shape, NEG, f32)
    l_ref[...] = jnp.zeros(l_ref.shape, f32)
    acc_ref[...] = jnp.zeros(acc_ref.shape, f32)

    def sel_body(kt, carry):
        k0 = pl.multiple_of(kt * SEL_TILE, SEL_TILE)
        col = lax.broadcasted_iota(jnp.int32, (1, SEL_TILE), 1) + k0
        dist = qpos - col
        distf = dist.astype(f32)
        ej = lax.broadcasted_iota(jnp.int32, (N_SEL, SEL_TILE), 0)
        ec = lax.broadcasted_iota(jnp.int32, (N_SEL, SEL_TILE), 1) + k0
        expand = (ec // SEL_BLOCK == ej).astype(bf)
        heads = range(H_NSA)
        k = [nsa_ref[0, pl.ds(k0, SEL_TILE), (4 + g) * HEAD_DIM:(5 + g) * HEAD_DIM].astype(bf)
             for g in range(KV_GROUPS)]
        v = [nsa_ref[0, pl.ds(k0, SEL_TILE), (6 + g) * HEAD_DIM:(7 + g) * HEAD_DIM].astype(bf)
             for g in range(KV_GROUPS)]
        valid = [(jnp.dot(chosen[g], expand, preferred_element_type=f32) > 0.5) & (dist >= 0)
                 for g in range(KV_GROUPS)]
        s = [jnp.where(valid[h // HPG], _dot_nt(qh[h], k[h // HPG]) - ALIBI[h // HPG][h % HPG] * distf, NEG)
             for h in heads]
        m_old = [m_ref[h] for h in heads]
        m_new = [jnp.maximum(m_old[h], s[h].max(-1, keepdims=True)) for h in heads]
        p = [jnp.where(valid[h // HPG], jnp.exp(s[h] - m_new[h]), 0.0) for h in heads]
        for h in heads:
            alpha = jnp.exp(m_old[h] - m_new[h])
            l_ref[h] = alpha * l_ref[h] + p[h].sum(-1, keepdims=True)
            acc_ref[h] = alpha * acc_ref[h] + jnp.dot(p[h].astype(bf), v[h // HPG], preferred_element_type=f32)
            m_ref[h] = m_new[h]
        return carry

    lax.fori_loop(0, (q0 + QB + SEL_TILE - 1) // SEL_TILE, sel_body, 0)
    for h in range(H_NSA):
        tot = l_ref[h]
        emit(h, 1, acc_ref[h] / jnp.where(tot > 0, tot, 1.0), False)


def nsa_prompt_pallas(q_b, gate_z, ck, cv, nsa_kv, win_kv):
    B, T, _ = q_b.shape
    assert T % SEL_TILE == 0 and T >= WINDOW + Q_BLOCK and T // SEL_BLOCK <= N_SEL
    assert (T - COMP_BLOCK) // COMP_STRIDE + 1 <= N_CMP_PAD
    per_b = lambda b, i: (b, 0, 0)
    return pl.pallas_call(
        _nsa_prompt_kernel,
        grid=(B, T // Q_BLOCK),
        in_specs=[pl.BlockSpec((1, Q_BLOCK, NSA_WIDTH), lambda b, i: (b, i, 0)),
                  pl.BlockSpec((1, Q_BLOCK, 3 * H_NSA), lambda b, i: (b, i, 0)),
                  pl.BlockSpec((1, KV_GROUPS, N_CMP_PAD, HEAD_DIM), lambda b, i: (b, 0, 0, 0)),
                  pl.BlockSpec((1, KV_GROUPS, N_CMP_PAD, HEAD_DIM), lambda b, i: (b, 0, 0, 0)),
                  pl.BlockSpec((1, T, 4 * KV_GROUPS * HEAD_DIM), per_b),
                  pl.BlockSpec((1, T, 2 * KV_GROUPS * HEAD_DIM), per_b)],
        out_specs=pl.BlockSpec((1, Q_BLOCK, NSA_WIDTH), lambda b, i: (b, i, 0)),
        out_shape=jax.ShapeDtypeStruct((B, T, NSA_WIDTH), jnp.float32),
        scratch_shapes=[pltpu.VMEM((H_NSA, Q_BLOCK, 1), jnp.float32),
                        pltpu.VMEM((H_NSA, Q_BLOCK, 1), jnp.float32),
                        pltpu.VMEM((H_NSA, Q_BLOCK, HEAD_DIM), jnp.float32)],
        compiler_params=pltpu.CompilerParams(
            dimension_semantics=("arbitrary", "arbitrary"), vmem_limit_bytes=48 * 1024 * 1024),
        name="nsa_prompt",
    )(q_b, gate_z, ck, cv, nsa_kv, win_kv)


SB_TILE = 128
N_PICK = PEER_HEADS * PEER_TOPK


def _split2_dot(a, b_exact):
    hi = a.astype(jnp.bfloat16)
    lo = (a - hi.astype(jnp.float32)).astype(jnp.bfloat16)
    return (jnp.dot(hi, b_exact, preferred_element_type=jnp.float32)
            + jnp.dot(lo, b_exact, preferred_element_type=jnp.float32))


def _sb_prompt_kernel(q_ref, kv_ref, o_ref, carry_ref, acc_ref):
    TQ = TK = SB_TILE
    bf, f32 = jnp.bfloat16, jnp.float32
    qb = pl.program_id(1)
    qpos = qb * TQ + lax.broadcasted_iota(jnp.int32, (TQ, 1), 0)
    later = (lax.broadcasted_iota(jnp.int32, (TK, TK), 0) > lax.broadcasted_iota(jnp.int32, (TK, TK), 1)).astype(bf)
    carry_ref[...] = jnp.zeros(carry_ref.shape, f32)
    acc_ref[...] = jnp.zeros(acc_ref.shape, f32)

    def body(i, c):
        k0 = pl.multiple_of((qb - i) * TK, TK)
        valid = (lax.broadcasted_iota(jnp.int32, (1, TK), 1) + k0) < qpos
        heads = range(H_SB)
        z = [_dot_nt((q_ref[0, :, h * HEAD_DIM:(h + 1) * HEAD_DIM] * SCALE).astype(bf),
                     kv_ref[0, pl.ds(k0, TK), h * HEAD_DIM:(h + 1) * HEAD_DIM].astype(bf)) for h in heads]
        log_sig = [jnp.minimum(z[h], 0.0) - jnp.log1p(jnp.exp(-jnp.abs(z[h]))) for h in heads]
        log_keep = [jnp.where(valid, log_sig[h] - z[h], 0.0) for h in heads]
        after = [_split2_dot(log_keep[h], later) for h in heads]
        w = [jnp.where(valid, jnp.exp(log_sig[h] + after[h] + carry_ref[h]), 0.0).astype(bf) for h in heads]
        for h in heads:
            v = kv_ref[0, pl.ds(k0, TK), SB_WIDTH + h * HEAD_DIM:SB_WIDTH + (h + 1) * HEAD_DIM].astype(bf)
            acc_ref[h] = acc_ref[h] + jnp.dot(w[h], v, preferred_element_type=f32)
            carry_ref[h] = carry_ref[h] + log_keep[h].sum(-1, keepdims=True)
        return c

    lax.fori_loop(0, qb + 1, body, 0)
    for h in range(H_SB):
        o_ref[0, :, h * HEAD_DIM:(h + 1) * HEAD_DIM] = acc_ref[h]


def sb_prompt_pallas(q_a, sb_kv):
    B, T, _ = q_a.shape
    assert T % SB_TILE == 0
    return pl.pallas_call(
        _sb_prompt_kernel,
        grid=(B, T // SB_TILE),
        in_specs=[pl.BlockSpec((1, SB_TILE, SB_WIDTH), lambda b, i: (b, i, 0)),
                  pl.BlockSpec((1, T, 2 * SB_WIDTH), lambda b, i: (b, 0, 0))],
        out_specs=pl.BlockSpec((1, SB_TILE, SB_WIDTH), lambda b, i: (b, i, 0)),
        out_shape=jax.ShapeDtypeStruct((B, T, SB_WIDTH), jnp.float32),
        scratch_shapes=[pltpu.VMEM((H_SB, SB_TILE, 1), jnp.float32),
                        pltpu.VMEM((H_SB, SB_TILE, HEAD_DIM), jnp.float32)],
        compiler_params=pltpu.CompilerParams(
            dimension_semantics=("arbitrary", "arbitrary"), vmem_limit_bytes=48 * 1024 * 1024),
        name="sb_prompt",
    )(q_a, sb_kv)


ROUTE_TOK = 256


def _top_rows(arrays, ids, n_top, outs):
    arrays = list(arrays)
    if ids is None:
        ids = lax.broadcasted_iota(jnp.int32, arrays[0].shape, 0)
    big = jnp.int32(1 << 30)
    for a in range(n_top):
        m = [s.max(0, keepdims=True) for s in arrays]
        ix = [jnp.where(s == mi, ids, big).min(0, keepdims=True) for s, mi in zip(arrays, m)]
        arrays = [jnp.where(ids == i, -jnp.inf, s) for s, i in zip(arrays, ix)]
        for (vals_ref, idx_ref), mi, i in zip(outs, m, ix):
            vals_ref[a:a + 1, :] = mi
            idx_ref[a:a + 1, :] = i


CAND_HEAD = PEER_TOPK
CAND_SIDE = 8
N_CAND_ROWS = CAND_HEAD + CAND_SIDE * (PEER_TOPK - 1)


def _peer_route_kernel(h_ref, wq_t_ref, keys_ref, e_ref, g_ref, s1_ref, i1_ref, s2_ref, i2_ref, ts_ref, ti_ref,
                       et_ref):
    bf, f32 = jnp.bfloat16, jnp.float32
    q_t = _dot_nt(wq_t_ref[...], h_ref[...].astype(bf))
    half = D_KEY // 2
    tn = q_t.shape[1]
    ci = lax.broadcasted_iota(jnp.int32, (N_CAND_ROWS, tn), 0)
    side = jnp.maximum(ci - CAND_HEAD, 0)
    cand_id = jnp.where(ci < CAND_HEAD, ci * PEER_TOPK, (side % CAND_SIDE) * PEER_TOPK + 1 + side // CAND_SIDE)
    for p in range(PEER_HEADS):
        scores = [jnp.dot(keys_ref[2 * p + hf], q_t[(2 * p + hf) * half:(2 * p + hf + 1) * half, :].astype(bf),
                          preferred_element_type=f32) for hf in range(2)]
        _top_rows(scores, None, PEER_TOPK, ((s1_ref, i1_ref), (s2_ref, i2_ref)))
        s1 = s1_ref[...]
        cand = jnp.concatenate([s1 + s2_ref[0:1, :]]
                               + [s1[:CAND_SIDE] + s2_ref[b:b + 1, :] for b in range(1, PEER_TOPK)], axis=0)
        _top_rows([cand], cand_id, PEER_TOPK, ((ts_ref, ti_ref),))
        ti = ti_ref[...]
        a_of, b_of = ti // PEER_TOPK, ti % PEER_TOPK
        k1 = jnp.zeros(ti.shape, jnp.int32)
        k2 = jnp.zeros(ti.shape, jnp.int32)
        for a in range(PEER_TOPK):
            k1 = jnp.where(a_of == a, i1_ref[a:a + 1, :], k1)
            k2 = jnp.where(b_of == a, i2_ref[a:a + 1, :], k2)
        ts = ts_ref[...]
        ex = jnp.exp(ts - ts.max(0, keepdims=True))
        et_ref[p * PEER_TOPK:(p + 1) * PEER_TOPK, :] = k1 * N_KEYS + k2
        g_ref[p * PEER_TOPK:(p + 1) * PEER_TOPK, :] = ex / ex.sum(0, keepdims=True)
    e_ref[...] = et_ref[...].T


def peer_route_pallas(h, wq_t, keys):
    n = h.shape[0]
    assert n % ROUTE_TOK == 0
    tn = ROUTE_TOK
    top = lambda dt: pltpu.VMEM((PEER_TOPK, tn), dt)
    return pl.pallas_call(
        _peer_route_kernel,
        grid=(n // tn,),
        in_specs=[pl.BlockSpec((tn, D_MODEL), lambda i: (i, 0)),
                  pl.BlockSpec((PEER_HEADS * D_KEY, D_MODEL), lambda i: (0, 0)),
                  pl.BlockSpec((2 * PEER_HEADS, N_KEYS, D_KEY // 2), lambda i: (0, 0, 0))],
        out_specs=[pl.BlockSpec((tn, N_PICK), lambda i: (i, 0)),
                   pl.BlockSpec((N_PICK, tn), lambda i: (0, i))],
        out_shape=[jax.ShapeDtypeStruct((n, N_PICK), jnp.int32),
                   jax.ShapeDtypeStruct((N_PICK, n), jnp.float32)],
        scratch_shapes=[top(jnp.float32), top(jnp.int32), top(jnp.float32), top(jnp.int32),
                        top(jnp.float32), top(jnp.int32), pltpu.VMEM((N_PICK, tn), jnp.int32)],
        compiler_params=pltpu.CompilerParams(
            dimension_semantics=("arbitrary",), vmem_limit_bytes=48 * 1024 * 1024),
        name="peer_route",
    )(h, wq_t, keys)


PEER_TOK_BLOCK = 128
PEER_SLOTS = 4
ID_PAD = 8
VEC_ROWS = D_MODEL // LANE
SLAB = 2 * VEC_ROWS


def pack_expert_slabs(u_tab, v_tab):
    rows = lambda tab: tab.reshape(N_EXPERTS, VEC_ROWS, LANE)
    return jnp.concatenate([rows(u_tab), rows(v_tab)], axis=1).reshape(N_EXPERTS * SLAB, LANE).astype(jnp.bfloat16)


def _peer_expert_kernel(e_hbm, h_ref, coef_ref, uv_hbm, g_ref, b_ref, y_ref,
                        e_smem, uvbuf, a_ref, w_ref, sem_e, sem_rows):
    TB = coef_ref.shape[1]
    i = pl.program_id(0)
    ahead = PEER_SLOTS - 1
    ids = pltpu.make_async_copy(e_hbm.at[pl.ds(i * TB, TB + ID_PAD), :], e_smem, sem_e)
    ids.start()
    ids.wait()

    def issue(t, slot):
        for k in range(N_PICK):
            row0 = pl.multiple_of(e_smem[t, k] * SLAB, SLAB)
            pltpu.make_async_copy(uv_hbm.at[pl.ds(row0, SLAB), :], uvbuf.at[slot, pl.ds(k * SLAB, SLAB), :],
                                  sem_rows.at[slot]).start()

    def wait_rows(slot):
        pltpu.make_async_copy(uv_hbm.at[pl.ds(0, N_PICK * SLAB), :], uvbuf.at[slot], sem_rows.at[slot]).wait()

    coef_t = coef_ref[...]
    tok = lax.broadcasted_iota(jnp.int32, coef_t.shape, 1)
    g_ln, b_ln = g_ref[...], b_ref[...]

    def compute(t, slot):
        slab = lambda k: uvbuf[slot, k * SLAB:(k + 1) * SLAB, :].astype(jnp.float32)
        r0 = pl.multiple_of(t * VEC_ROWS, VEC_ROWS)
        h = h_ref[pl.ds(r0, VEC_ROWS), :]
        for k in range(N_PICK):
            a_ref[k:k + 1, :] = jnp.sum(slab(k)[:VEC_ROWS] * h, axis=0, keepdims=True)
        a = jnp.sum(a_ref[...], axis=-1, keepdims=True)
        c = jnp.sum(jnp.where(tok == t, coef_t, 0.0), axis=-1, keepdims=True)
        w_ref[...] = jnp.broadcast_to(c * jax.nn.gelu(a), (N_PICK, LANE))
        parts = [w_ref[k:k + 1, :] * slab(k)[VEC_ROWS:] for k in range(N_PICK)]
        while len(parts) > 1:
            parts = [parts[j] + parts[j + 1] for j in range(0, len(parts), 2)]
        x = ALPHA * h + parts[0]
        mu = jnp.mean(x, keepdims=True)
        var = jnp.mean(jnp.square(x - mu), keepdims=True)
        y_ref[pl.ds(r0, VEC_ROWS), :] = (x - mu) * lax.rsqrt(var + LN_EPS) * g_ln + b_ln

    @pl.when(i == 0)
    def _():
        for t in range(ahead):
            issue(t, t)

    def body(j, carry):
        for s in range(PEER_SLOTS):
            t = PEER_SLOTS * j + s
            issue(t + ahead, (s + ahead) % PEER_SLOTS)
            wait_rows(s)
            compute(t, s)
        return carry

    lax.fori_loop(0, TB // PEER_SLOTS, body, 0)

    @pl.when(i == pl.num_programs(0) - 1)
    def _():
        for s in range(ahead):
            wait_rows(s)


def peer_expert_pallas(e, h, coef_t, uv_slabs, ln_g, ln_b):
    n = h.shape[0]
    tb = min(PEER_TOK_BLOCK, n)
    assert n % tb == 0 and tb % PEER_SLOTS == 0 and PEER_SLOTS - 1 <= ID_PAD
    fixed = lambda i: (0, 0)
    y = pl.pallas_call(
        _peer_expert_kernel,
        grid=(n // tb,),
        in_specs=[pl.BlockSpec(memory_space=pl.ANY),
                  pl.BlockSpec((tb * VEC_ROWS, LANE), lambda i: (i, 0)),
                  pl.BlockSpec((N_PICK, tb), lambda i: (0, i)),
                  pl.BlockSpec(memory_space=pl.ANY),
                  pl.BlockSpec((VEC_ROWS, LANE), fixed),
                  pl.BlockSpec((VEC_ROWS, LANE), fixed)],
        out_specs=pl.BlockSpec((tb * VEC_ROWS, LANE), lambda i: (i, 0)),
        out_shape=jax.ShapeDtypeStruct((n * VEC_ROWS, LANE), jnp.float32),
        scratch_shapes=[pltpu.SMEM((tb + ID_PAD, N_PICK), jnp.int32),
                        pltpu.VMEM((PEER_SLOTS, N_PICK * SLAB, LANE), jnp.bfloat16),
                        pltpu.VMEM((N_PICK, LANE), jnp.float32),
                        pltpu.VMEM((N_PICK, LANE), jnp.float32),
                        pltpu.SemaphoreType.DMA,
                        pltpu.SemaphoreType.DMA((PEER_SLOTS,))],
        compiler_params=pltpu.CompilerParams(dimension_semantics=("arbitrary",)),
        name="peer_experts",
    )(jnp.pad(e, ((0, ID_PAD), (0, 0))), h.reshape(n * VEC_ROWS, LANE), coef_t, uv_slabs,
      ln_g.reshape(VEC_ROWS, LANE), ln_b.reshape(VEC_ROWS, LANE))
    return y.reshape(n, D_MODEL)


MID_ROWS = 512


def _mid_kernel(x_ref, oa_ref, ob_ref, mg_ref, w_ref, g_ref, b_ref, h_ref):
    bf, f32 = jnp.bfloat16, jnp.float32

    def normed(o, g):
        return (o * lax.rsqrt(jnp.mean(jnp.square(o), -1, keepdims=True) + LN_EPS) * g).astype(bf)

    m = (jnp.dot(normed(oa_ref[...], mg_ref[:, :SB_WIDTH]), w_ref[:SB_WIDTH, :], preferred_element_type=f32)
         + jnp.dot(normed(ob_ref[...], mg_ref[:, SB_WIDTH:]), w_ref[SB_WIDTH:, :], preferred_element_type=f32))
    x = ALPHA * x_ref[...] + m
    mu = x.mean(-1, keepdims=True)
    var = jnp.square(x - mu).mean(-1, keepdims=True)
    h_ref[...] = (x - mu) * lax.rsqrt(var + LN_EPS) * g_ref[...] + b_ref[...]


def mixer_out_pallas(x, o_a, o_b, mix_g, w_out_bf16, ln_g, ln_b):
    n = x.shape[0]
    rows = min(MID_ROWS, n)
    assert n % rows == 0
    row = lambda i: (i, 0)
    fixed = lambda i: (0, 0)
    return pl.pallas_call(
        _mid_kernel,
        grid=(n // rows,),
        in_specs=[pl.BlockSpec((rows, D_MODEL), row),
                  pl.BlockSpec((rows, SB_WIDTH), row),
                  pl.BlockSpec((rows, NSA_WIDTH), row),
                  pl.BlockSpec((1, MIX_WIDTH), fixed),
                  pl.BlockSpec((MIX_WIDTH, D_MODEL), fixed),
                  pl.BlockSpec((1, D_MODEL), fixed),
                  pl.BlockSpec((1, D_MODEL), fixed)],
        out_specs=pl.BlockSpec((rows, D_MODEL), row),
        out_shape=jax.ShapeDtypeStruct((n, D_MODEL), jnp.float32),
        compiler_params=pltpu.CompilerParams(
            dimension_semantics=("arbitrary",), vmem_limit_bytes=48 * 1024 * 1024),
        name="mixer_out",
    )(x, o_a, o_b, mix_g.reshape(1, MIX_WIDTH), w_out_bf16, ln_g.reshape(1, D_MODEL), ln_b.reshape(1, D_MODEL))


PAGE_ROWS = 128
ROW_PAD = 8


def _pad_rows(x, rows):
    return jnp.concatenate([x, jnp.zeros((rows - x.shape[0], x.shape[1]), x.dtype)], axis=0)


def _sb_sample_kernel(pt_ref, q_ref, new_ref, *rest):
    n_pages = len(rest) - 1
    pages, o_ref = rest[:n_pages], rest[n_pages]
    bf, f32 = jnp.bfloat16, jnp.float32
    R = 4 * H_SB
    q_t = q_ref[0]
    rowi = lax.broadcasted_iota(jnp.int32, (R, 1), 0)
    later = (lax.broadcasted_iota(jnp.int32, (PAGE_ROWS, PAGE_ROWS), 0)
             > lax.broadcasted_iota(jnp.int32, (PAGE_ROWS, PAGE_ROWS), 1)).astype(bf)

    key_i = lax.broadcasted_iota(jnp.int32, (1, PAGE_ROWS), 1)
    newest = key_i < rowi // H_SB
    tiles = [_pad_rows(new_ref[0], PAGE_ROWS)] + [pages[p][0] for p in reversed(range(n_pages))]
    n = range(len(tiles))
    z = [_dot_nt(q_t, tiles[i][:, :SB_WIDTH].astype(bf)) for i in n]
    log_sig = [jnp.minimum(z[i], 0.0) - jnp.log1p(jnp.exp(-jnp.abs(z[i]))) for i in n]
    log_keep = [log_sig[i] - z[i] for i in n]
    log_keep[0] = jnp.where(newest, log_keep[0], 0.0)
    after = [_split2_dot(log_keep[i], later) for i in n]
    carry = jnp.zeros((R, 1), f32)
    acc = jnp.zeros((R, SB_WIDTH), f32)
    for i in n:
        w = jnp.exp(log_sig[i] + after[i] + carry)
        if i == 0:
            w = jnp.where(newest, w, 0.0)
        acc = acc + jnp.dot(w.astype(bf), tiles[i][:, SB_WIDTH:].astype(bf), preferred_element_type=f32)
        carry = carry + log_keep[i].sum(-1, keepdims=True)
    head_of_col = lax.broadcasted_iota(jnp.int32, (1, SB_WIDTH), 1) // HEAD_DIM
    own = jnp.where(head_of_col == rowi % H_SB, acc, 0.0)
    for t in range(4):
        o_ref[0, t:t + 1, :] = own[t * H_SB:(t + 1) * H_SB, :].sum(0, keepdims=True)


def sb_sample_pallas(page_table, q_a, sb_new, cache_sb):
    B, T, _ = q_a.shape
    n_pages = page_table.shape[1]
    assert T == 4
    eye = jnp.eye(H_SB, dtype=jnp.float32)
    q_t = (q_a.reshape(B, T, 1, H_SB, HEAD_DIM) * eye[None, None, :, :, None] * SCALE)
    q_t = q_t.reshape(B, T * H_SB, SB_WIDTH).astype(jnp.bfloat16)
    new = jnp.pad(sb_new, ((0, 0), (0, ROW_PAD - T), (0, 0)))
    page_spec = lambda p: pl.BlockSpec((1, PAGE_ROWS, 2 * SB_WIDTH), lambda b, pt, p=p: (pt[b, p], 0, 0))
    return pl.pallas_call(
        _sb_sample_kernel,
        grid_spec=pltpu.PrefetchScalarGridSpec(
            num_scalar_prefetch=1, grid=(B,),
            in_specs=[pl.BlockSpec((1, T * H_SB, SB_WIDTH), lambda b, pt: (b, 0, 0)),
                      pl.BlockSpec((1, ROW_PAD, 2 * SB_WIDTH), lambda b, pt: (b, 0, 0))]
                     + [page_spec(p) for p in range(n_pages)],
            out_specs=pl.BlockSpec((1, T, SB_WIDTH), lambda b, pt: (b, 0, 0))),
        out_shape=jax.ShapeDtypeStruct((B, T, SB_WIDTH), jnp.float32),
        compiler_params=pltpu.CompilerParams(
            dimension_semantics=("arbitrary",), vmem_limit_bytes=48 * 1024 * 1024),
        name="sb_sample",
    )(page_table, q_t, new, *([cache_sb] * n_pages))


def _compress_tokens(chunk_rows, pe_ref, w1_ref, w2_ref):
    bf, f32 = jnp.bfloat16, jnp.float32
    out = []
    for j in range(2):
        h_a = h_b = None
        for l in range(COMP_STRIDE):
            x = chunk_rows(j, l)
            a = jnp.dot((x + pe_ref[j, l:l + 1, :]).astype(bf), w1_ref[j, l], preferred_element_type=f32)
            b = jnp.dot((x + pe_ref[j, COMP_STRIDE + l:COMP_STRIDE + l + 1, :]).astype(bf),
                        w1_ref[j, COMP_STRIDE + l], preferred_element_type=f32)
            h_a = a if h_a is None else h_a + a
            h_b = b if h_b is None else h_b + b
        hdn = jax.nn.gelu(h_a + pltpu.roll(h_b, h_b.shape[0] - 1, 0)).astype(bf)
        out.append([jnp.dot(hdn[:, g * COMP_HID:(g + 1) * COMP_HID], w2_ref[j], preferred_element_type=f32)
                    for g in range(KV_GROUPS)])
    return out


def _compress_prompt_kernel(xk_ref, xv_ref, w1_ref, w2_ref, pe_ref, ck_ref, cv_ref):
    x_refs = (xk_ref, xv_ref)
    chunks = xk_ref.shape[1] // COMP_STRIDE
    cmp_kv = _compress_tokens(lambda j, l: x_refs[j][0, pl.ds(l, chunks, stride=COMP_STRIDE), :],
                              pe_ref, w1_ref, w2_ref)
    for g in range(KV_GROUPS):
        ck_ref[0, g] = cmp_kv[0][g]
        cv_ref[0, g] = cmp_kv[1][g]


def compress_prompt_pallas(xk, xv, w1, w2, pe):
    B, T, _ = xk.shape
    chunks = T // COMP_STRIDE
    assert chunks == N_CMP_PAD
    per_b = lambda b: (b, 0, 0)
    out = jax.ShapeDtypeStruct((B, KV_GROUPS, chunks, HEAD_DIM), jnp.float32)
    return pl.pallas_call(
        _compress_prompt_kernel,
        grid=(B,),
        in_specs=[pl.BlockSpec((1, T, LANE), per_b), pl.BlockSpec((1, T, LANE), per_b),
                  pl.BlockSpec(w1.shape, lambda b: (0, 0, 0, 0)),
                  pl.BlockSpec(w2.shape, lambda b: (0, 0, 0)),
                  pl.BlockSpec(pe.shape, lambda b: (0, 0, 0))],
        out_specs=[pl.BlockSpec((1, KV_GROUPS, chunks, HEAD_DIM), lambda b: (b, 0, 0, 0))] * 2,
        out_shape=[out, out],
        compiler_params=pltpu.CompilerParams(
            dimension_semantics=("arbitrary",), vmem_limit_bytes=48 * 1024 * 1024),
        name="compress_prompt",
    )(xk, xv, w1, w2, pe)


def compress_params(cmp_pe, cmp_w1, cmp_w2):
    zero = jnp.zeros_like(cmp_w1)
    w1 = jnp.concatenate([jnp.concatenate([cmp_w1, zero], axis=-1),
                          jnp.concatenate([zero, cmp_w1], axis=-1)], axis=-2).astype(jnp.bfloat16)
    return w1, cmp_w2.astype(jnp.bfloat16), jnp.concatenate([cmp_pe, cmp_pe], axis=-1)


def _nsa_sample_kernel(pt_ref, q_ref, gz_ref, new_ref, neww_ref, win_ref, w1_ref, w2_ref, pe_ref, *rest):
    n_pages = len(rest) - 2
    pages, o_ref, x_ref = rest[:n_pages], rest[n_pages], rest[n_pages + 1]
    bf, f32 = jnp.bfloat16, jnp.float32
    past = n_pages * PAGE_ROWS
    n_cmp = (past + 4 - COMP_BLOCK) // COMP_STRIDE + 1
    n_sel = -(-(past + 4) // SEL_BLOCK)
    R = HPG * ROW_PAD

    for i, pg in enumerate(pages):
        for j in range(2):
            x_ref[j, i * PAGE_ROWS:(i + 1) * PAGE_ROWS, :] = pg[0, :, j * LANE:(j + 1) * LANE]
    cmp_kv = _compress_tokens(lambda j, l: x_ref[j, pl.ds(l, past // COMP_STRIDE, stride=COMP_STRIDE), :],
                              pe_ref, w1_ref, w2_ref)

    rowi = lax.broadcasted_iota(jnp.int32, (R, 1), 0)
    qpos = past + rowi % ROW_PAD
    gate = jax.nn.sigmoid(gz_ref[0])

    def softmax_rows(s, valid):
        s = jnp.where(valid, s, NEG)
        e = jnp.where(valid, jnp.exp(s - s.max(-1, keepdims=True)), 0.0)
        tot = e.sum(-1, keepdims=True)
        return e / jnp.where(tot > 0, tot, 1.0)

    G = range(KV_GROUPS)
    q = [q_ref[0, g] for g in G]
    slope = []
    for g in G:
        sl = jnp.full((R, 1), ALIBI[g][HPG - 1], f32)
        for r in range(HPG - 1):
            sl = jnp.where(rowi // ROW_PAD == r, ALIBI[g][r], sl)
        slope.append(sl)

    n_i = lax.broadcasted_iota(jnp.int32, (1, PAGE_ROWS), 1)
    cpos = n_i * COMP_STRIDE + (COMP_BLOCK - 1)
    valid_c = (cpos <= qpos) & (n_i < n_cmp)
    dist_c = (qpos - cpos).astype(f32)
    p_c = [softmax_rows(_dot_nt(q[g], cmp_kv[0][g].astype(bf)) - slope[g] * dist_c, valid_c) for g in G]
    o_c = [jnp.dot(p_c[g].astype(bf), cmp_kv[1][g].astype(bf), preferred_element_type=f32) for g in G]

    nn = lax.broadcasted_iota(jnp.int32, (PAGE_ROWS, LANE), 0)
    jj = lax.broadcasted_iota(jnp.int32, (PAGE_ROWS, LANE), 1)
    sel_map = ((nn * COMP_STRIDE) // SEL_BLOCK == jj).astype(bf)
    blk = lax.broadcasted_iota(jnp.int32, (ROW_PAD, LANE), 1)
    qp8 = past + lax.broadcasted_iota(jnp.int32, (ROW_PAD, 1), 0)
    forced = (blk == qp8 // SEL_BLOCK) | (blk == 0)
    imp = []
    for g in G:
        psum = p_c[g][0:ROW_PAD]
        for r in range(1, HPG):
            psum = psum + p_c[g][r * ROW_PAD:(r + 1) * ROW_PAD]
        im = jnp.where(forced, FORCE, jnp.where(blk * SEL_BLOCK <= qp8, _split_dot(psum, sel_map), -1.0))
        imp.append(jnp.where(blk < n_sel, im, -3.0))
    picked = [jnp.zeros((ROW_PAD, LANE), f32) for g in G]
    for _ in range(min(TOP_N, n_sel)):
        for g in G:
            mx = imp[g].max(-1, keepdims=True)
            first_max = jnp.where(imp[g] == mx, blk, LANE).min(-1, keepdims=True)
            pick = blk == first_max
            picked[g] = jnp.where(pick, 1.0, picked[g])
            imp[g] = jnp.where(pick, -4.0, imp[g])
    chosen = [jnp.concatenate([picked[g].astype(bf)] * HPG, axis=0) for g in G]

    new_kv = _pad_rows(new_ref[0], PAGE_ROWS)
    kcol = [slice((4 + g) * HEAD_DIM, (5 + g) * HEAD_DIM) for g in G]
    vcol = [slice((6 + g) * HEAD_DIM, (7 + g) * HEAD_DIM) for g in G]
    n_key = past + PAGE_ROWS
    col = lax.broadcasted_iota(jnp.int32, (1, n_key), 1)
    ej = lax.broadcasted_iota(jnp.int32, (LANE, n_key), 0)
    ec = lax.broadcasted_iota(jnp.int32, (LANE, n_key), 1)
    expand = (ec // SEL_BLOCK == ej).astype(bf)
    dist = qpos - col
    dist_s = dist.astype(f32)
    s = [jnp.concatenate([_dot_nt(q[g], pg[0, :, kcol[g]].astype(bf)) for pg in pages]
                         + [_dot_nt(q[g], new_kv[:, kcol[g]].astype(bf))], axis=1) for g in G]
    valid_s = [(jnp.dot(chosen[g], expand, preferred_element_type=f32) > 0.5) & (dist >= 0) for g in G]
    p_s = [softmax_rows(s[g] - slope[g] * dist_s, valid_s[g]).astype(bf) for g in G]
    o_s = [jnp.dot(p_s[g][:, past:], new_kv[:, vcol[g]].astype(bf), preferred_element_type=f32) for g in G]
    for i, pg in enumerate(pages):
        for g in G:
            o_s[g] = o_s[g] + jnp.dot(p_s[g][:, i * PAGE_ROWS:(i + 1) * PAGE_ROWS], pg[0, :, vcol[g]].astype(bf),
                                      preferred_element_type=f32)

    wb = win_ref.shape[1]
    new_w = _pad_rows(neww_ref[0], PAGE_ROWS)
    wk = [slice(g * HEAD_DIM, (g + 1) * HEAD_DIM) for g in G]
    wv = [slice((2 + g) * HEAD_DIM, (3 + g) * HEAD_DIM) for g in G]
    wpos = past - wb + lax.broadcasted_iota(jnp.int32, (1, wb + PAGE_ROWS), 1)
    dist = qpos - wpos
    valid_w = (dist >= 0) & (dist < WINDOW) & (wpos >= 0)
    dist_w = dist.astype(f32)
    s = [jnp.concatenate([_dot_nt(q[g], win_ref[0, :, wk[g]].astype(bf)),
                          _dot_nt(q[g], new_w[:, wk[g]].astype(bf))], axis=1) for g in G]
    p_w = [softmax_rows(s[g] - slope[g] * dist_w, valid_w).astype(bf) for g in G]
    o_w = [jnp.dot(p_w[g][:, :wb], win_ref[0, :, wv[g]].astype(bf), preferred_element_type=f32)
           + jnp.dot(p_w[g][:, wb:], new_w[:, wv[g]].astype(bf), preferred_element_type=f32) for g in G]

    for h in range(H_NSA):
        g, r = divmod(h, HPG)
        rs = slice(r * ROW_PAD, (r + 1) * ROW_PAD)
        o_ref[0, :, h * HEAD_DIM:(h + 1) * HEAD_DIM] = (gate[:, 3 * h:3 * h + 1] * o_c[g][rs]
                                                        + gate[:, 3 * h + 1:3 * h + 2] * o_s[g][rs]
                                                        + gate[:, 3 * h + 2:3 * h + 3] * o_w[g][rs])


def nsa_sample_pallas(page_table, q_b, gate_z, nsa_new, win_new, cache_nsa, cache_win, w1, w2, pe):
    B, T, _ = q_b.shape
    n_pages = page_table.shape[1]
    assert T == 4 and cache_win.shape[1] == min(WINDOW, n_pages * PAGE_ROWS)
    bf = jnp.bfloat16
    pad_t = lambda x: jnp.pad(x, ((0, 0), (0, ROW_PAD - T), (0, 0)))
    q = pad_t(q_b * SCALE).reshape(B, ROW_PAD, KV_GROUPS, HPG, HEAD_DIM).transpose(0, 2, 3, 1, 4)
    q = q.reshape(B, KV_GROUPS, HPG * ROW_PAD, HEAD_DIM).astype(bf)
    per_b = lambda b, pt: (b, 0, 0)
    page_spec = lambda p: pl.BlockSpec((1, PAGE_ROWS, 4 * KV_GROUPS * HEAD_DIM), lambda b, pt, p=p: (pt[b, p], 0, 0))
    out = pl.pallas_call(
        _nsa_sample_kernel,
        grid_spec=pltpu.PrefetchScalarGridSpec(
            num_scalar_prefetch=1, grid=(B,),
            in_specs=[pl.BlockSpec((1, KV_GROUPS, HPG * ROW_PAD, HEAD_DIM), lambda b, pt: (b, 0, 0, 0)),
                      pl.BlockSpec((1, ROW_PAD, 3 * H_NSA), per_b),
                      pl.BlockSpec((1, ROW_PAD, 4 * KV_GROUPS * HEAD_DIM), per_b),
                      pl.BlockSpec((1, ROW_PAD, 2 * KV_GROUPS * HEAD_DIM), per_b),
                      pl.BlockSpec((1, cache_win.shape[1], 2 * KV_GROUPS * HEAD_DIM), per_b),
                      pl.BlockSpec(w1.shape, lambda b, pt: (0, 0, 0, 0)),
                      pl.BlockSpec(w2.shape, lambda b, pt: (0, 0, 0)),
                      pl.BlockSpec(pe.shape, lambda b, pt: (0, 0, 0))]
                     + [page_spec(p) for p in range(n_pages)],
            out_specs=pl.BlockSpec((1, ROW_PAD, NSA_WIDTH), per_b),
            scratch_shapes=[pltpu.VMEM((2, n_pages * PAGE_ROWS, LANE), jnp.float32)]),
        out_shape=jax.ShapeDtypeStruct((B, ROW_PAD, NSA_WIDTH), jnp.float32),
        compiler_params=pltpu.CompilerParams(
            dimension_semantics=("arbitrary",), vmem_limit_bytes=48 * 1024 * 1024),
        name="nsa_sample",
    )(page_table, q, pad_t(gate_z), pad_t(nsa_new), pad_t(win_new), cache_win, w1, w2, pe,
      *([cache_nsa] * n_pages))
    return out[:, :T]


def mixer_prompt(x, w_in_pad, cmp_params):
    B, T, _ = x.shape
    z = project_pallas(x.reshape(B * T, D_MODEL), w_in_pad).reshape(B, T, IN_COLS_PAD)
    sb_kv = z[..., OFF_SBKV:OFF_QB].reshape(B, T, 2, H_SB, HEAD_DIM)
    nsa_kv = z[..., OFF_NSAKV:OFF_WIN].reshape(B, T, 4, KV_GROUPS, HEAD_DIM)
    win_kv = z[..., OFF_WIN:OFF_GATE].reshape(B, T, 2, KV_GROUPS, HEAD_DIM)
    ck, cv = compress_prompt_pallas(z[..., OFF_NSAKV:OFF_NSAKV + LANE], z[..., OFF_NSAKV + LANE:OFF_NSAKV + 2 * LANE],
                                    *cmp_params)
    o_b = nsa_prompt_pallas(z[..., OFF_QB:OFF_NSAKV], z[..., OFF_GATE:IN_COLS], ck, cv,
                            z[..., OFF_NSAKV:OFF_WIN], z[..., OFF_WIN:OFF_GATE])
    o_a = sb_prompt_pallas(z[..., :SB_WIDTH], z[..., OFF_SBKV:OFF_QB])
    new_win = win_kv[:, T - min(WINDOW, T):]
    return o_a, o_b, sb_kv, nsa_kv, new_win


def mixer_sample(x, cache_sb, cache_nsa, cache_win, page_table, w_in_pad, cmp_params):
    B, T, _ = x.shape
    z = project_pallas(x.reshape(B * T, D_MODEL), w_in_pad).reshape(B, T, IN_COLS_PAD)
    sb_new = z[..., OFF_SBKV:OFF_QB]
    nsa_new = z[..., OFF_NSAKV:OFF_WIN]
    win_new = z[..., OFF_WIN:OFF_GATE]
    n_pool, page = cache_sb.shape[:2]
    o_a = sb_sample_pallas(page_table, z[..., :SB_WIDTH], sb_new, cache_sb.reshape(n_pool, page, 2 * SB_WIDTH))
    win_flat = cache_win.reshape(B, cache_win.shape[1], 2 * KV_GROUPS * HEAD_DIM)
    o_b = nsa_sample_pallas(page_table, z[..., OFF_QB:OFF_NSAKV], z[..., OFF_GATE:IN_COLS], nsa_new, win_new,
                            cache_nsa.reshape(n_pool, page, 4 * KV_GROUPS * HEAD_DIM), win_flat, *cmp_params)
    new_win = jnp.concatenate([win_flat, win_new], axis=1)[:, T:].reshape(B, -1, 2, KV_GROUPS, HEAD_DIM)
    return (o_a, o_b, sb_new.reshape(B, T, 2, H_SB, HEAD_DIM), nsa_new.reshape(B, T, 4, KV_GROUPS, HEAD_DIM), new_win)


def block_out(x, o_a, o_b, mix_g, w_out, ln1_g, ln1_b, wq_t, keys, uv_tab, ln2_g, ln2_b):
    h = mixer_out_pallas(x.reshape(-1, D_MODEL), o_a.reshape(-1, SB_WIDTH), o_b.reshape(-1, NSA_WIDTH),
                         mix_g, w_out, ln1_g, ln1_b)
    e, g_t = peer_route_pallas(h, wq_t, keys)
    return peer_expert_pallas(e, h, g_t, uv_tab, ln2_g, ln2_b).reshape(x.shape)


def kernel(x_prompt, x_sample, cache_sb_kv, cache_nsa_kv, cache_win_kv, page_table, w_in, cmp_pe, cmp_w1, cmp_w2, mix_norm_g, w_out, ln1_g, ln1_b, peer_w_q, peer_sub_keys, peer_u, peer_v, ln2_g, ln2_b):
    l = 0
    w_in_pad = jnp.pad(w_in[l], ((0, 0), (0, IN_COLS_PAD - IN_COLS))).astype(jnp.bfloat16)
    cmp_params = compress_params(cmp_pe[l], cmp_w1[l], cmp_w2[l])
    o_a, o_b, sb_p, nsa_p, win_p = mixer_prompt(x_prompt, w_in_pad, cmp_params)
    wq_t = peer_w_q[l].T.astype(jnp.bfloat16)
    keys = peer_sub_keys[l].reshape(2 * PEER_HEADS, N_KEYS, D_KEY // 2).astype(jnp.bfloat16)
    uv_tab = pack_expert_slabs(peer_u[l], peer_v[l])
    w_out_bf = w_out[l].astype(jnp.bfloat16)
    h_p = block_out(x_prompt, o_a, o_b, mix_norm_g[l], w_out_bf, ln1_g[l], ln1_b[l],
                    wq_t, keys, uv_tab, ln2_g[l], ln2_b[l])
    o_a, o_b, sb_s, nsa_s, win_s = mixer_sample(x_sample, cache_sb_kv[l], cache_nsa_kv[l], cache_win_kv[l],
                                                page_table, w_in_pad, cmp_params)
    h_s = block_out(x_sample, o_a, o_b, mix_norm_g[l], w_out_bf, ln1_g[l], ln1_b[l],
                    wq_t, keys, uv_tab, ln2_g[l], ln2_b[l])
    return (h_p, h_s, sb_p[None], nsa_p[None], win_p[None], sb_s[None], nsa_s[None], win_s[None])
```

```python
import jax, jax.numpy as jnp
from jax import lax
from jax.experimental import pallas as pl
from jax.experimental.pallas import tpu as pltpu

D_MODEL = 1024
HEAD_DIM = 64
MIX_WIDTH = D_MODEL
SB_WIDTH = MIX_WIDTH // 2
NSA_WIDTH = MIX_WIDTH - SB_WIDTH
H_SB = SB_WIDTH // HEAD_DIM
H_NSA = NSA_WIDTH // HEAD_DIM
KV_GROUPS = 2
HPG = H_NSA // KV_GROUPS
COMP_BLOCK = 32
COMP_STRIDE = 16
COMP_HID = 128
SEL_BLOCK = 64
TOP_N = 8
WINDOW = 512
Q_BLOCK = 128
PEER_HEADS = 8
N_KEYS = 128
N_EXPERTS = N_KEYS * N_KEYS
PEER_TOPK = 16
D_KEY = 256
DEPTH = 1
ALPHA = (2.0 * DEPTH) ** 0.25
LN_EPS = 1e-5
NEG = -1e30
FORCE = 1e4
SCALE = HEAD_DIM ** -0.5

OFF_SBKV = SB_WIDTH
OFF_QB = 3 * SB_WIDTH
OFF_NSAKV = OFF_QB + NSA_WIDTH
OFF_WIN = OFF_NSAKV + 4 * KV_GROUPS * HEAD_DIM
OFF_GATE = OFF_WIN + 2 * KV_GROUPS * HEAD_DIM
IN_COLS = OFF_GATE + 3 * H_NSA

LANE = 128
V7X_VMEM_BYTES = 64 * 1024 * 1024
VMEM_LIMIT_BYTES = V7X_VMEM_BYTES * 3 // 4
STEP_TOKENS = 4
IN_COLS_PAD = -(-IN_COLS // LANE) * LANE
PROJ_ROWS = 512


def _proj_kernel(x_ref, w_ref, z_ref):
    z_ref[...] = jnp.dot(x_ref[...].astype(jnp.bfloat16), w_ref[...],
                         preferred_element_type=jnp.float32)


def project_pallas(x2d, w_in_pad_bf16):
    n = x2d.shape[0]
    rows = min(PROJ_ROWS, n)
    return pl.pallas_call(
        _proj_kernel,
        grid=(n // rows,),
        in_specs=[pl.BlockSpec((rows, D_MODEL), lambda i: (i, 0)),
                  pl.BlockSpec((D_MODEL, IN_COLS_PAD), lambda i: (0, 0))],
        out_specs=pl.BlockSpec((rows, IN_COLS_PAD), lambda i: (i, 0)),
        out_shape=jax.ShapeDtypeStruct((n, IN_COLS_PAD), jnp.float32),
        compiler_params=pltpu.CompilerParams(
            dimension_semantics=("arbitrary",), vmem_limit_bytes=VMEM_LIMIT_BYTES),
        name="in_proj",
    )(x2d, w_in_pad_bf16)


N_SEL = 32
N_CMP_PAD = 128
SEL_TILE = 512
ALIBI = [[2.0 ** (-8.0 * (g * HPG + r + 1) / H_NSA) for r in range(HPG)] for g in range(KV_GROUPS)]


def _dot_nt(a, b):
    return lax.dot_general(a, b, (((1,), (1,)), ((), ())), preferred_element_type=jnp.float32)


def _split_dot(a, b_exact):
    hi = a.astype(jnp.bfloat16)
    r1 = a - hi.astype(jnp.float32)
    mid = r1.astype(jnp.bfloat16)
    lo = (r1 - mid.astype(jnp.float32)).astype(jnp.bfloat16)
    d = lambda x: jnp.dot(x, b_exact, preferred_element_type=jnp.float32)
    return d(hi) + d(mid) + d(lo)


def _nsa_prompt_kernel(q_ref, gz_ref, ck_ref, cv_ref, nsa_ref, win_ref, o_ref, m_ref, l_ref, acc_ref):
    QB = Q_BLOCK
    qb = pl.program_id(1)
    q0 = qb * QB
    bf = jnp.bfloat16
    f32 = jnp.float32
    gate = jax.nn.sigmoid(gz_ref[0])
    qpos = q0 + lax.broadcasted_iota(jnp.int32, (QB, 1), 0)
    qh = [(q_ref[0, :, h * HEAD_DIM:(h + 1) * HEAD_DIM] * SCALE).astype(bf) for h in range(H_NSA)]

    def attend(qs, k, v, biases, valid):
        n = range(len(qs))
        s = [jnp.where(valid, _dot_nt(qs[i], k) + biases[i], NEG) for i in n]
        e = [jnp.where(valid, jnp.exp(s[i] - s[i].max(-1, keepdims=True)), 0.0) for i in n]
        tot = [e[i].sum(-1, keepdims=True) for i in n]
        inv = [1.0 / jnp.where(tot[i] > 0, tot[i], 1.0) for i in n]
        o = [jnp.dot(e[i].astype(bf), v, preferred_element_type=f32) * inv[i] for i in n]
        return o, [e[i] * inv[i] for i in n]

    def emit(h, branch, o, first):
        o = gate[:, 3 * h + branch:3 * h + branch + 1] * o
        sl = (0, slice(None), slice(h * HEAD_DIM, (h + 1) * HEAD_DIM))
        o_ref[sl] = o if first else o_ref[sl] + o

    chosen = []
    for g in range(KV_GROUPS):
        n_i = lax.broadcasted_iota(jnp.int32, (1, N_CMP_PAD), 1)
        cpos = n_i * COMP_STRIDE + (COMP_BLOCK - 1)
        n_cmp = (pl.num_programs(1) * QB - COMP_BLOCK) // COMP_STRIDE + 1
        valid_c = (cpos <= qpos) & (n_i < n_cmp)
        dist_c = (qpos - cpos).astype(f32)
        ckg = ck_ref[0, g].astype(bf)
        cvg = cv_ref[0, g].astype(bf)
        o_c, p_c = attend(qh[g * HPG:(g + 1) * HPG], ckg, cvg, [-ALIBI[g][r] * dist_c for r in range(HPG)], valid_c)
        psum = p_c[0]
        for r in range(HPG):
            emit(g * HPG + r, 0, o_c[r], True)
            if r:
                psum = psum + p_c[r]

        nn = lax.broadcasted_iota(jnp.int32, (N_CMP_PAD, N_SEL), 0)
        jj = lax.broadcasted_iota(jnp.int32, (N_CMP_PAD, N_SEL), 1)
        sel_map = ((nn * COMP_STRIDE) // SEL_BLOCK == jj).astype(bf)
        imp = _split_dot(psum, sel_map)
        blk = lax.broadcasted_iota(jnp.int32, (QB, N_SEL), 1)
        forced = (blk == qpos // SEL_BLOCK) | (blk == 0)
        imp = jnp.where(forced, FORCE, jnp.where(blk * SEL_BLOCK <= qpos, imp, -1.0))
        picked = jnp.zeros((QB, N_SEL), f32)
        for _ in range(TOP_N):
            mx = imp.max(-1, keepdims=True)
            first_max = jnp.where(imp == mx, blk, N_SEL).min(-1, keepdims=True)
            pick = blk == first_max
            picked = jnp.where(pick, 1.0, picked)
            imp = jnp.where(pick, -2.0, imp)
        chosen.append(picked.astype(bf))

        k0 = pl.multiple_of(jnp.maximum(q0 - WINDOW, 0), QB)
        k = win_ref[0, pl.ds(k0, WINDOW + QB), g * HEAD_DIM:(g + 1) * HEAD_DIM].astype(bf)
        v = win_ref[0, pl.ds(k0, WINDOW + QB), (2 + g) * HEAD_DIM:(3 + g) * HEAD_DIM].astype(bf)
        dist = qpos - (lax.broadcasted_iota(jnp.int32, (1, WINDOW + QB), 1) + k0)
        valid_w = (dist >= 0) & (dist < WINDOW)
        dist_w = dist.astype(f32)
        o_w, _ = attend(qh[g * HPG:(g + 1) * HPG], k, v, [-ALIBI[g][r] * dist_w for r in range(HPG)], valid_w)
        for r in range(HPG):
            emit(g * HPG + r, 2, o_w[r], False)

    m_ref[...] = jnp.full(m_ref.shape, NEG, f32)
    l_ref[...] = jnp.zeros(l_ref.shape, f32)
    acc_ref[...] = jnp.zeros(acc_ref.shape, f32)

    def sel_body(kt, carry):
        k0 = pl.multiple_of(kt * SEL_TILE, SEL_TILE)
        col = lax.broadcasted_iota(jnp.int32, (1, SEL_TILE), 1) + k0
        dist = qpos - col
        distf = dist.astype(f32)
        ej = lax.broadcasted_iota(jnp.int32, (N_SEL, SEL_TILE), 0)
        ec = lax.broadcasted_iota(jnp.int32, (N_SEL, SEL_TILE), 1) + k0
        expand = (ec // SEL_BLOCK == ej).astype(bf)
        heads = range(H_NSA)
        k = [nsa_ref[0, pl.ds(k0, SEL_TILE), (4 + g) * HEAD_DIM:(5 + g) * HEAD_DIM].astype(bf)
             for g in range(KV_GROUPS)]
        v = [nsa_ref[0, pl.ds(k0, SEL_TILE), (6 + g) * HEAD_DIM:(7 + g) * HEAD_DIM].astype(bf)
             for g in range(KV_GROUPS)]
        valid = [(jnp.dot(chosen[g], expand, preferred_element_type=f32) > 0.5) & (dist >= 0)
                 for g in range(KV_GROUPS)]
        s = [jnp.where(valid[h // HPG], _dot_nt(qh[h], k[h // HPG]) - ALIBI[h // HPG][h % HPG] * distf, NEG)
             for h in heads]
        m_old = [m_ref[h] for h in heads]
        m_new = [jnp.maximum(m_old[h], s[h].max(-1, keepdims=True)) for h in heads]
        p = [jnp.where(valid[h // HPG], jnp.exp(s[h] - m_new[h]), 0.0) for h in heads]
        for h in heads:
            alpha = jnp.exp(m_old[h] - m_new[h])
            l_ref[h] = alpha * l_ref[h] + p[h].sum(-1, keepdims=True)
            acc_ref[h] = alpha * acc_ref[h] + jnp.dot(p[h].astype(bf), v[h // HPG], preferred_element_type=f32)
            m_ref[h] = m_new[h]
        return carry

    lax.fori_loop(0, (q0 + QB + SEL_TILE - 1) // SEL_TILE, sel_body, 0)
    for h in range(H_NSA):
        tot = l_ref[h]
        emit(h, 1, acc_ref[h] / jnp.where(tot > 0, tot, 1.0), False)


def nsa_prompt_pallas(q_b, gate_z, ck, cv, nsa_kv, win_kv):
    B, T, _ = q_b.shape
    assert T % SEL_TILE == 0 and T >= WINDOW + Q_BLOCK and T // SEL_BLOCK <= N_SEL
    assert (T - COMP_BLOCK) // COMP_STRIDE + 1 <= N_CMP_PAD
    per_b = lambda b, i: (b, 0, 0)
    return pl.pallas_call(
        _nsa_prompt_kernel,
        grid=(B, T // Q_BLOCK),
        in_specs=[pl.BlockSpec((1, Q_BLOCK, NSA_WIDTH), lambda b, i: (b, i, 0)),
                  pl.BlockSpec((1, Q_BLOCK, 3 * H_NSA), lambda b, i: (b, i, 0)),
                  pl.BlockSpec((1, KV_GROUPS, N_CMP_PAD, HEAD_DIM), lambda b, i: (b, 0, 0, 0)),
                  pl.BlockSpec((1, KV_GROUPS, N_CMP_PAD, HEAD_DIM), lambda b, i: (b, 0, 0, 0)),
                  pl.BlockSpec((1, T, 4 * KV_GROUPS * HEAD_DIM), per_b),
                  pl.BlockSpec((1, T, 2 * KV_GROUPS * HEAD_DIM), per_b)],
        out_specs=pl.BlockSpec((1, Q_BLOCK, NSA_WIDTH), lambda b, i: (b, i, 0)),
        out_shape=jax.ShapeDtypeStruct((B, T, NSA_WIDTH), jnp.float32),
        scratch_shapes=[pltpu.VMEM((H_NSA, Q_BLOCK, 1), jnp.float32),
                        pltpu.VMEM((H_NSA, Q_BLOCK, 1), jnp.float32),
                        pltpu.VMEM((H_NSA, Q_BLOCK, HEAD_DIM), jnp.float32)],
        compiler_params=pltpu.CompilerParams(
            dimension_semantics=("arbitrary", "arbitrary"), vmem_limit_bytes=VMEM_LIMIT_BYTES),
        name="nsa_prompt",
    )(q_b, gate_z, ck, cv, nsa_kv, win_kv)


SB_TILE = 128
N_PICK = PEER_HEADS * PEER_TOPK


def _split2_dot(a, b_exact):
    hi = a.astype(jnp.bfloat16)
    lo = (a - hi.astype(jnp.float32)).astype(jnp.bfloat16)
    return (jnp.dot(hi, b_exact, preferred_element_type=jnp.float32)
            + jnp.dot(lo, b_exact, preferred_element_type=jnp.float32))


def _sb_prompt_kernel(q_ref, kv_ref, o_ref, carry_ref, acc_ref):
    TQ = TK = SB_TILE
    bf, f32 = jnp.bfloat16, jnp.float32
    qb = pl.program_id(1)
    qpos = qb * TQ + lax.broadcasted_iota(jnp.int32, (TQ, 1), 0)
    later = (lax.broadcasted_iota(jnp.int32, (TK, TK), 0) > lax.broadcasted_iota(jnp.int32, (TK, TK), 1)).astype(bf)
    carry_ref[...] = jnp.zeros(carry_ref.shape, f32)
    acc_ref[...] = jnp.zeros(acc_ref.shape, f32)

    def body(i, c):
        k0 = pl.multiple_of((qb - i) * TK, TK)
        valid = (lax.broadcasted_iota(jnp.int32, (1, TK), 1) + k0) < qpos
        heads = range(H_SB)
        z = [_dot_nt((q_ref[0, :, h * HEAD_DIM:(h + 1) * HEAD_DIM] * SCALE).astype(bf),
                     kv_ref[0, pl.ds(k0, TK), h * HEAD_DIM:(h + 1) * HEAD_DIM].astype(bf)) for h in heads]
        log_sig = [jnp.minimum(z[h], 0.0) - jnp.log(1.0 + jnp.exp(-jnp.abs(z[h]))) for h in heads]
        log_keep = [jnp.where(valid, log_sig[h] - z[h], 0.0) for h in heads]
        after = [_split2_dot(log_keep[h], later) for h in heads]
        w = [jnp.where(valid, jnp.exp(log_sig[h] + after[h] + carry_ref[h]), 0.0).astype(bf) for h in heads]
        for h in heads:
            v = kv_ref[0, pl.ds(k0, TK), SB_WIDTH + h * HEAD_DIM:SB_WIDTH + (h + 1) * HEAD_DIM].astype(bf)
            acc_ref[h] = acc_ref[h] + jnp.dot(w[h], v, preferred_element_type=f32)
            carry_ref[h] = carry_ref[h] + log_keep[h].sum(-1, keepdims=True)
        return c

    lax.fori_loop(0, qb + 1, body, 0)
    for h in range(H_SB):
        o_ref[0, :, h * HEAD_DIM:(h + 1) * HEAD_DIM] = acc_ref[h]


def sb_prompt_pallas(q_a, sb_kv):
    B, T, _ = q_a.shape
    assert T % SB_TILE == 0
    return pl.pallas_call(
        _sb_prompt_kernel,
        grid=(B, T // SB_TILE),
        in_specs=[pl.BlockSpec((1, SB_TILE, SB_WIDTH), lambda b, i: (b, i, 0)),
                  pl.BlockSpec((1, T, 2 * SB_WIDTH), lambda b, i: (b, 0, 0))],
        out_specs=pl.BlockSpec((1, SB_TILE, SB_WIDTH), lambda b, i: (b, i, 0)),
        out_shape=jax.ShapeDtypeStruct((B, T, SB_WIDTH), jnp.float32),
        scratch_shapes=[pltpu.VMEM((H_SB, SB_TILE, 1), jnp.float32),
                        pltpu.VMEM((H_SB, SB_TILE, HEAD_DIM), jnp.float32)],
        compiler_params=pltpu.CompilerParams(
            dimension_semantics=("arbitrary", "arbitrary"), vmem_limit_bytes=VMEM_LIMIT_BYTES),
        name="sb_prompt",
    )(q_a, sb_kv)


ROUTE_TOK = 256


def _top_rows(arrays, ids, n_top, outs):
    arrays = list(arrays)
    if ids is None:
        ids = lax.broadcasted_iota(jnp.int32, arrays[0].shape, 0)
    big = jnp.int32(1 << 30)
    for a in range(n_top):
        m = [s.max(0, keepdims=True) for s in arrays]
        ix = [jnp.where(s == mi, ids, big).min(0, keepdims=True) for s, mi in zip(arrays, m)]
        arrays = [jnp.where(ids == i, -jnp.inf, s) for s, i in zip(arrays, ix)]
        for (vals_ref, idx_ref), mi, i in zip(outs, m, ix):
            vals_ref[a:a + 1, :] = mi
            idx_ref[a:a + 1, :] = i


CAND_HEAD = PEER_TOPK
CAND_SIDE = 8
N_CAND_ROWS = CAND_HEAD + CAND_SIDE * (PEER_TOPK - 1)


def _peer_route_kernel(h_ref, wq_t_ref, keys_ref, e_ref, g_ref, s1_ref, i1_ref, s2_ref, i2_ref, ts_ref, ti_ref,
                       et_ref):
    bf, f32 = jnp.bfloat16, jnp.float32
    q_t = _dot_nt(wq_t_ref[...], h_ref[...].astype(bf))
    half = D_KEY // 2
    tn = q_t.shape[1]
    ci = lax.broadcasted_iota(jnp.int32, (N_CAND_ROWS, tn), 0)
    side = jnp.maximum(ci - CAND_HEAD, 0)
    cand_id = jnp.where(ci < CAND_HEAD, ci * PEER_TOPK, (side % CAND_SIDE) * PEER_TOPK + 1 + side // CAND_SIDE)
    for p in range(PEER_HEADS):
        scores = [jnp.dot(keys_ref[2 * p + hf], q_t[(2 * p + hf) * half:(2 * p + hf + 1) * half, :].astype(bf),
                          preferred_element_type=f32) for hf in range(2)]
        _top_rows(scores, None, PEER_TOPK, ((s1_ref, i1_ref), (s2_ref, i2_ref)))
        s1 = s1_ref[...]
        cand = jnp.concatenate([s1 + s2_ref[0:1, :]]
                               + [s1[:CAND_SIDE] + s2_ref[b:b + 1, :] for b in range(1, PEER_TOPK)], axis=0)
        _top_rows([cand], cand_id, PEER_TOPK, ((ts_ref, ti_ref),))
        ti = ti_ref[...]
        a_of, b_of = ti // PEER_TOPK, ti % PEER_TOPK
        k1 = jnp.zeros(ti.shape, jnp.int32)
        k2 = jnp.zeros(ti.shape, jnp.int32)
        for a in range(PEER_TOPK):
            k1 = jnp.where(a_of == a, i1_ref[a:a + 1, :], k1)
            k2 = jnp.where(b_of == a, i2_ref[a:a + 1, :], k2)
        ts = ts_ref[...]
        ex = jnp.exp(ts - ts.max(0, keepdims=True))
        et_ref[p * PEER_TOPK:(p + 1) * PEER_TOPK, :] = k1 * N_KEYS + k2
        g_ref[p * PEER_TOPK:(p + 1) * PEER_TOPK, :] = ex / ex.sum(0, keepdims=True)
    e_ref[...] = et_ref[...].T


def peer_route_pallas(h, wq_t, keys):
    n = h.shape[0]
    assert n % ROUTE_TOK == 0
    tn = ROUTE_TOK
    top = lambda dt: pltpu.VMEM((PEER_TOPK, tn), dt)
    return pl.pallas_call(
        _peer_route_kernel,
        grid=(n // tn,),
        in_specs=[pl.BlockSpec((tn, D_MODEL), lambda i: (i, 0)),
                  pl.BlockSpec((PEER_HEADS * D_KEY, D_MODEL), lambda i: (0, 0)),
                  pl.BlockSpec((2 * PEER_HEADS, N_KEYS, D_KEY // 2), lambda i: (0, 0, 0))],
        out_specs=[pl.BlockSpec((tn, N_PICK), lambda i: (i, 0)),
                   pl.BlockSpec((N_PICK, tn), lambda i: (0, i))],
        out_shape=[jax.ShapeDtypeStruct((n, N_PICK), jnp.int32),
                   jax.ShapeDtypeStruct((N_PICK, n), jnp.float32)],
        scratch_shapes=[top(jnp.float32), top(jnp.int32), top(jnp.float32), top(jnp.int32),
                        top(jnp.float32), top(jnp.int32), pltpu.VMEM((N_PICK, tn), jnp.int32)],
        compiler_params=pltpu.CompilerParams(
            dimension_semantics=("arbitrary",), vmem_limit_bytes=VMEM_LIMIT_BYTES),
        name="peer_route",
    )(h, wq_t, keys)


PEER_TOK_BLOCK = 256
PEER_SLOTS = 4
ID_PAD = 8
SUBLANE_PICKS = 8
VEC_ROWS = D_MODEL // LANE
SLAB = 2 * VEC_ROWS


def pack_expert_slabs(u_tab, v_tab):
    rows = lambda tab: tab.reshape(N_EXPERTS, VEC_ROWS, LANE)
    return jnp.concatenate([rows(u_tab), rows(v_tab)], axis=1).astype(jnp.bfloat16)


def _peer_expert_kernel(e_hbm, h_ref, coef_ref, uv_hbm, g_ref, b_ref, y_ref,
                        e_smem, uvbuf, a_ref, w_ref, sem_e, sem_rows):
    TB = coef_ref.shape[1]
    i = pl.program_id(0)
    ahead = PEER_SLOTS - 2
    ids = pltpu.make_async_copy(e_hbm.at[pl.ds(i * TB, TB + ID_PAD), :], e_smem, sem_e)
    ids.start()
    ids.wait()

    def issue(t, slot):
        for k in range(N_PICK):
            pltpu.make_async_copy(uv_hbm.at[e_smem[t, k]], uvbuf.at[slot, k],
                                  sem_rows.at[slot]).start()

    def wait_rows(slot):
        pltpu.make_async_copy(uv_hbm.at[pl.ds(0, N_PICK)], uvbuf.at[slot], sem_rows.at[slot]).wait()

    coef_t = coef_ref[...]
    tok = lax.broadcasted_iota(jnp.int32, coef_t.shape, 1)
    g_ln, b_ln = g_ref[...], b_ref[...]

    sub = lax.broadcasted_iota(jnp.int32, (VEC_ROWS, LANE), 0)

    def fold(x, y, d):
        return jnp.where(sub % (2 * d) < d, x + pltpu.roll(x, VEC_ROWS - d, 0), y + pltpu.roll(y, d, 0))

    def slab(t, k):
        return uvbuf[t % PEER_SLOTS, k].astype(jnp.float32)

    def h_rows(t):
        return h_ref[pl.ds(pl.multiple_of(t * VEC_ROWS, VEC_ROWS), VEC_ROWS), :]

    def stage_a(t):
        h = h_rows(t)
        for k0 in range(0, N_PICK, SUBLANE_PICKS):
            q = [slab(t, k)[:VEC_ROWS] * h for k in range(k0, k0 + SUBLANE_PICKS)]
            for d in (4, 2, 1):
                q = [fold(q[j], q[j + d], d) for j in range(d)]
            a_ref[t % 2, k0:k0 + SUBLANE_PICKS, :] = q[0]

    def stage_b_weights(t):
        a = jnp.sum(a_ref[t % 2], axis=-1, keepdims=True)
        c = jnp.sum(jnp.where(tok == t, coef_t, 0.0), axis=-1, keepdims=True)
        w_ref[...] = jnp.broadcast_to(c * jax.nn.gelu(a), (N_PICK, LANE))

    def stage_b_apply(t):
        parts = [w_ref[k:k + 1, :] * slab(t, k)[VEC_ROWS:] for k in range(SUBLANE_PICKS)]
        for k in range(SUBLANE_PICKS, N_PICK):
            parts[k % SUBLANE_PICKS] = parts[k % SUBLANE_PICKS] + w_ref[k:k + 1, :] * slab(t, k)[VEC_ROWS:]
        while len(parts) > 1:
            parts = [parts[j] + parts[j + 1] for j in range(0, len(parts), 2)]
        x = ALPHA * h_rows(t) + parts[0]
        mu = jnp.mean(x, keepdims=True)
        var = jnp.mean(jnp.square(x - mu), keepdims=True)
        y_ref[pl.ds(pl.multiple_of(t * VEC_ROWS, VEC_ROWS), VEC_ROWS), :] = (
            (x - mu) * lax.rsqrt(var + LN_EPS) * g_ln + b_ln)

    @pl.when(i == 0)
    def _():
        for t in range(ahead):
            issue(t, t)

    issue(ahead, ahead)
    wait_rows(0)
    stage_a(0)

    def body(t, carry):
        issue(t + ahead, (t + ahead) % PEER_SLOTS)
        wait_rows(t % PEER_SLOTS)
        stage_b_weights(t - 1)
        stage_a(t)
        stage_b_apply(t - 1)
        return carry

    lax.fori_loop(1, TB, body, 0)
    stage_b_weights(TB - 1)
    stage_b_apply(TB - 1)

    @pl.when(i == pl.num_programs(0) - 1)
    def _():
        for s in range(ahead):
            wait_rows(s)


def peer_expert_pallas(e, h, coef_t, uv_slabs, ln_g, ln_b):
    n = h.shape[0]
    tb = min(PEER_TOK_BLOCK, n)
    assert n % tb == 0 and tb % PEER_SLOTS == 0 and PEER_SLOTS - 1 <= ID_PAD
    fixed = lambda i: (0, 0)
    y = pl.pallas_call(
        _peer_expert_kernel,
        grid=(n // tb,),
        in_specs=[pl.BlockSpec(memory_space=pl.ANY),
                  pl.BlockSpec((tb * VEC_ROWS, LANE), lambda i: (i, 0)),
                  pl.BlockSpec((N_PICK, tb), lambda i: (0, i)),
                  pl.BlockSpec(memory_space=pl.ANY),
                  pl.BlockSpec((VEC_ROWS, LANE), fixed),
                  pl.BlockSpec((VEC_ROWS, LANE), fixed)],
        out_specs=pl.BlockSpec((tb * VEC_ROWS, LANE), lambda i: (i, 0)),
        out_shape=jax.ShapeDtypeStruct((n * VEC_ROWS, LANE), jnp.float32),
        scratch_shapes=[pltpu.SMEM((tb + ID_PAD, N_PICK), jnp.int32),
                        pltpu.VMEM((PEER_SLOTS, N_PICK, SLAB, LANE), jnp.bfloat16),
                        pltpu.VMEM((2, N_PICK, LANE), jnp.float32),
                        pltpu.VMEM((N_PICK, LANE), jnp.float32),
                        pltpu.SemaphoreType.DMA,
                        pltpu.SemaphoreType.DMA((PEER_SLOTS,))],
        compiler_params=pltpu.CompilerParams(dimension_semantics=("arbitrary",)),
        name="peer_experts",
    )(jnp.pad(e, ((0, ID_PAD), (0, 0))), h.reshape(n * VEC_ROWS, LANE), coef_t, uv_slabs,
      ln_g.reshape(VEC_ROWS, LANE), ln_b.reshape(VEC_ROWS, LANE))
    return y.reshape(n, D_MODEL)


MID_ROWS = 512


def _mid_kernel(x_ref, oa_ref, ob_ref, mg_ref, w_ref, g_ref, b_ref, h_ref):
    bf, f32 = jnp.bfloat16, jnp.float32

    def normed(o, g):
        return (o * lax.rsqrt(jnp.mean(jnp.square(o), -1, keepdims=True) + LN_EPS) * g).astype(bf)

    m = (jnp.dot(normed(oa_ref[...], mg_ref[:, :SB_WIDTH]), w_ref[:SB_WIDTH, :], preferred_element_type=f32)
         + jnp.dot(normed(ob_ref[...], mg_ref[:, SB_WIDTH:]), w_ref[SB_WIDTH:, :], preferred_element_type=f32))
    x = ALPHA * x_ref[...] + m
    mu = x.mean(-1, keepdims=True)
    var = jnp.square(x - mu).mean(-1, keepdims=True)
    h_ref[...] = (x - mu) * lax.rsqrt(var + LN_EPS) * g_ref[...] + b_ref[...]


def mixer_out_pallas(x, o_a, o_b, mix_g, w_out_bf16, ln_g, ln_b):
    n = x.shape[0]
    rows = min(MID_ROWS, n)
    assert n % rows == 0
    row = lambda i: (i, 0)
    fixed = lambda i: (0, 0)
    return pl.pallas_call(
        _mid_kernel,
        grid=(n // rows,),
        in_specs=[pl.BlockSpec((rows, D_MODEL), row),
                  pl.BlockSpec((rows, SB_WIDTH), row),
                  pl.BlockSpec((rows, NSA_WIDTH), row),
                  pl.BlockSpec((1, MIX_WIDTH), fixed),
                  pl.BlockSpec((MIX_WIDTH, D_MODEL), fixed),
                  pl.BlockSpec((1, D_MODEL), fixed),
                  pl.BlockSpec((1, D_MODEL), fixed)],
        out_specs=pl.BlockSpec((rows, D_MODEL), row),
        out_shape=jax.ShapeDtypeStruct((n, D_MODEL), jnp.float32),
        compiler_params=pltpu.CompilerParams(
            dimension_semantics=("arbitrary",), vmem_limit_bytes=VMEM_LIMIT_BYTES),
        name="mixer_out",
    )(x, o_a, o_b, mix_g.reshape(1, MIX_WIDTH), w_out_bf16, ln_g.reshape(1, D_MODEL), ln_b.reshape(1, D_MODEL))


PAGE_ROWS = 128
ROW_PAD = 8


def _pad_rows(x, rows):
    return jnp.concatenate([x, jnp.zeros((rows - x.shape[0], x.shape[1]), x.dtype)], axis=0)


def _sb_sample_kernel(pt_ref, q_ref, new_ref, *rest):
    n_pages = len(rest) - 1
    pages, o_ref = rest[:n_pages], rest[n_pages]
    bf, f32 = jnp.bfloat16, jnp.float32
    R = STEP_TOKENS * H_SB
    q_t = q_ref[0]
    rowi = lax.broadcasted_iota(jnp.int32, (R, 1), 0)
    later = (lax.broadcasted_iota(jnp.int32, (PAGE_ROWS, PAGE_ROWS), 0)
             > lax.broadcasted_iota(jnp.int32, (PAGE_ROWS, PAGE_ROWS), 1)).astype(bf)

    key_i = lax.broadcasted_iota(jnp.int32, (1, PAGE_ROWS), 1)
    newest = key_i < rowi // H_SB
    tiles = [_pad_rows(new_ref[0], PAGE_ROWS)] + [pages[p][0] for p in reversed(range(n_pages))]
    n = range(len(tiles))
    z = [_dot_nt(q_t, tiles[i][:, :SB_WIDTH].astype(bf)) for i in n]
    log_sig = [jnp.minimum(z[i], 0.0) - jnp.log(1.0 + jnp.exp(-jnp.abs(z[i]))) for i in n]
    log_keep = [log_sig[i] - z[i] for i in n]
    log_keep[0] = jnp.where(newest, log_keep[0], 0.0)
    after = [_split2_dot(log_keep[i], later) for i in n]
    carry = jnp.zeros((R, 1), f32)
    acc = jnp.zeros((R, SB_WIDTH), f32)
    for i in n:
        w = jnp.exp(log_sig[i] + after[i] + carry)
        if i == 0:
            w = jnp.where(newest, w, 0.0)
        acc = acc + jnp.dot(w.astype(bf), tiles[i][:, SB_WIDTH:].astype(bf), preferred_element_type=f32)
        carry = carry + log_keep[i].sum(-1, keepdims=True)
    head_of_col = lax.broadcasted_iota(jnp.int32, (1, SB_WIDTH), 1) // HEAD_DIM
    own = jnp.where(head_of_col == rowi % H_SB, acc, 0.0)
    for t in range(STEP_TOKENS):
        o_ref[0, t:t + 1, :] = own[t * H_SB:(t + 1) * H_SB, :].sum(0, keepdims=True)


def sb_sample_pallas(page_table, q_a, sb_new, cache_sb):
    B, T, _ = q_a.shape
    n_pages = page_table.shape[1]
    assert T == STEP_TOKENS
    eye = jnp.eye(H_SB, dtype=jnp.float32)
    q_t = (q_a.reshape(B, T, 1, H_SB, HEAD_DIM) * eye[None, None, :, :, None] * SCALE)
    q_t = q_t.reshape(B, T * H_SB, SB_WIDTH).astype(jnp.bfloat16)
    new = jnp.pad(sb_new, ((0, 0), (0, ROW_PAD - T), (0, 0)))
    page_spec = lambda p: pl.BlockSpec((1, PAGE_ROWS, 2 * SB_WIDTH), lambda b, pt, p=p: (pt[b, p], 0, 0))
    return pl.pallas_call(
        _sb_sample_kernel,
        grid_spec=pltpu.PrefetchScalarGridSpec(
            num_scalar_prefetch=1, grid=(B,),
            in_specs=[pl.BlockSpec((1, T * H_SB, SB_WIDTH), lambda b, pt: (b, 0, 0)),
                      pl.BlockSpec((1, ROW_PAD, 2 * SB_WIDTH), lambda b, pt: (b, 0, 0))]
                     + [page_spec(p) for p in range(n_pages)],
            out_specs=pl.BlockSpec((1, T, SB_WIDTH), lambda b, pt: (b, 0, 0))),
        out_shape=jax.ShapeDtypeStruct((B, T, SB_WIDTH), jnp.float32),
        compiler_params=pltpu.CompilerParams(
            dimension_semantics=("arbitrary",), vmem_limit_bytes=VMEM_LIMIT_BYTES),
        name="sb_sample",
    )(page_table, q_t, new, *([cache_sb] * n_pages))


def _compress_tokens(chunk_rows, pe_ref, w1_ref, w2_ref):
    bf, f32 = jnp.bfloat16, jnp.float32
    out = []
    for j in range(2):
        h_a = h_b = None
        for l in range(COMP_STRIDE):
            x = chunk_rows(j, l)
            a = jnp.dot((x + pe_ref[j, l:l + 1, :]).astype(bf), w1_ref[j, l], preferred_element_type=f32)
            b = jnp.dot((x + pe_ref[j, COMP_STRIDE + l:COMP_STRIDE + l + 1, :]).astype(bf),
                        w1_ref[j, COMP_STRIDE + l], preferred_element_type=f32)
            h_a = a if h_a is None else h_a + a
            h_b = b if h_b is None else h_b + b
        hdn = jax.nn.gelu(h_a + pltpu.roll(h_b, h_b.shape[0] - 1, 0)).astype(bf)
        out.append([jnp.dot(hdn[:, g * COMP_HID:(g + 1) * COMP_HID], w2_ref[j], preferred_element_type=f32)
                    for g in range(KV_GROUPS)])
    return out


def _compress_prompt_kernel(xk_ref, xv_ref, w1_ref, w2_ref, pe_ref, ck_ref, cv_ref):
    x_refs = (xk_ref, xv_ref)
    chunks = xk_ref.shape[1] // COMP_STRIDE
    cmp_kv = _compress_tokens(lambda j, l: x_refs[j][0, pl.ds(l, chunks, stride=COMP_STRIDE), :],
                              pe_ref, w1_ref, w2_ref)
    for g in range(KV_GROUPS):
        ck_ref[0, g] = cmp_kv[0][g]
        cv_ref[0, g] = cmp_kv[1][g]


def compress_prompt_pallas(xk, xv, w1, w2, pe):
    B, T, _ = xk.shape
    chunks = T // COMP_STRIDE
    assert chunks == N_CMP_PAD
    per_b = lambda b: (b, 0, 0)
    out = jax.ShapeDtypeStruct((B, KV_GROUPS, chunks, HEAD_DIM), jnp.float32)
    return pl.pallas_call(
        _compress_prompt_kernel,
        grid=(B,),
        in_specs=[pl.BlockSpec((1, T, LANE), per_b), pl.BlockSpec((1, T, LANE), per_b),
                  pl.BlockSpec(w1.shape, lambda b: (0, 0, 0, 0)),
                  pl.BlockSpec(w2.shape, lambda b: (0, 0, 0)),
                  pl.BlockSpec(pe.shape, lambda b: (0, 0, 0))],
        out_specs=[pl.BlockSpec((1, KV_GROUPS, chunks, HEAD_DIM), lambda b: (b, 0, 0, 0))] * 2,
        out_shape=[out, out],
        compiler_params=pltpu.CompilerParams(
            dimension_semantics=("arbitrary",), vmem_limit_bytes=VMEM_LIMIT_BYTES),
        name="compress_prompt",
    )(xk, xv, w1, w2, pe)


def compress_params(cmp_pe, cmp_w1, cmp_w2):
    zero = jnp.zeros_like(cmp_w1)
    w1 = jnp.concatenate([jnp.concatenate([cmp_w1, zero], axis=-1),
                          jnp.concatenate([zero, cmp_w1], axis=-1)], axis=-2).astype(jnp.bfloat16)
    return w1, cmp_w2.astype(jnp.bfloat16), jnp.concatenate([cmp_pe, cmp_pe], axis=-1)


def _nsa_sample_kernel(pt_ref, q_ref, gz_ref, new_ref, neww_ref, win_ref, w1_ref, w2_ref, pe_ref, *rest):
    n_pages = len(rest) - 2
    pages, o_ref, x_ref = rest[:n_pages], rest[n_pages], rest[n_pages + 1]
    bf, f32 = jnp.bfloat16, jnp.float32
    past = n_pages * PAGE_ROWS
    n_cmp = (past + STEP_TOKENS - COMP_BLOCK) // COMP_STRIDE + 1
    n_sel = -(-(past + STEP_TOKENS) // SEL_BLOCK)
    R = HPG * ROW_PAD

    for i, pg in enumerate(pages):
        for j in range(2):
            x_ref[j, i * PAGE_ROWS:(i + 1) * PAGE_ROWS, :] = pg[0, :, j * LANE:(j + 1) * LANE]
    cmp_kv = _compress_tokens(lambda j, l: x_ref[j, pl.ds(l, past // COMP_STRIDE, stride=COMP_STRIDE), :],
                              pe_ref, w1_ref, w2_ref)

    rowi = lax.broadcasted_iota(jnp.int32, (R, 1), 0)
    qpos = past + rowi % ROW_PAD
    gate = jax.nn.sigmoid(gz_ref[0])

    def softmax_rows(s, valid):
        s = jnp.where(valid, s, NEG)
        e = jnp.where(valid, jnp.exp(s - s.max(-1, keepdims=True)), 0.0)
        tot = e.sum(-1, keepdims=True)
        return e / jnp.where(tot > 0, tot, 1.0)

    G = range(KV_GROUPS)
    q = [q_ref[0, g] for g in G]
    slope = []
    for g in G:
        sl = jnp.full((R, 1), ALIBI[g][HPG - 1], f32)
        for r in range(HPG - 1):
            sl = jnp.where(rowi // ROW_PAD == r, ALIBI[g][r], sl)
        slope.append(sl)

    n_i = lax.broadcasted_iota(jnp.int32, (1, PAGE_ROWS), 1)
    cpos = n_i * COMP_STRIDE + (COMP_BLOCK - 1)
    valid_c = (cpos <= qpos) & (n_i < n_cmp)
    dist_c = (qpos - cpos).astype(f32)
    p_c = [softmax_rows(_dot_nt(q[g], cmp_kv[0][g].astype(bf)) - slope[g] * dist_c, valid_c) for g in G]
    o_c = [jnp.dot(p_c[g].astype(bf), cmp_kv[1][g].astype(bf), preferred_element_type=f32) for g in G]

    nn = lax.broadcasted_iota(jnp.int32, (PAGE_ROWS, LANE), 0)
    jj = lax.broadcasted_iota(jnp.int32, (PAGE_ROWS, LANE), 1)
    sel_map = ((nn * COMP_STRIDE) // SEL_BLOCK == jj).astype(bf)
    blk = lax.broadcasted_iota(jnp.int32, (ROW_PAD, LANE), 1)
    qp8 = past + lax.broadcasted_iota(jnp.int32, (ROW_PAD, 1), 0)
    forced = (blk == qp8 // SEL_BLOCK) | (blk == 0)
    imp = []
    for g in G:
        psum = p_c[g][0:ROW_PAD]
        for r in range(1, HPG):
            psum = psum + p_c[g][r * ROW_PAD:(r + 1) * ROW_PAD]
        im = jnp.where(forced, FORCE, jnp.where(blk * SEL_BLOCK <= qp8, _split_dot(psum, sel_map), -1.0))
        imp.append(jnp.where(blk < n_sel, im, -3.0))
    picked = [jnp.zeros((ROW_PAD, LANE), f32) for g in G]
    for _ in range(min(TOP_N, n_sel)):
        for g in G:
            mx = imp[g].max(-1, keepdims=True)
            first_max = jnp.where(imp[g] == mx, blk, LANE).min(-1, keepdims=True)
            pick = blk == first_max
            picked[g] = jnp.where(pick, 1.0, picked[g])
            imp[g] = jnp.where(pick, -4.0, imp[g])
    chosen = [jnp.concatenate([picked[g].astype(bf)] * HPG, axis=0) for g in G]

    new_kv = _pad_rows(new_ref[0], PAGE_ROWS)
    kcol = [slice((4 + g) * HEAD_DIM, (5 + g) * HEAD_DIM) for g in G]
    vcol = [slice((6 + g) * HEAD_DIM, (7 + g) * HEAD_DIM) for g in G]
    n_key = past + PAGE_ROWS
    col = lax.broadcasted_iota(jnp.int32, (1, n_key), 1)
    ej = lax.broadcasted_iota(jnp.int32, (LANE, n_key), 0)
    ec = lax.broadcasted_iota(jnp.int32, (LANE, n_key), 1)
    expand = (ec // SEL_BLOCK == ej).astype(bf)
    dist = qpos - col
    dist_s = dist.astype(f32)
    s = [jnp.concatenate([_dot_nt(q[g], pg[0, :, kcol[g]].astype(bf)) for pg in pages]
                         + [_dot_nt(q[g], new_kv[:, kcol[g]].astype(bf))], axis=1) for g in G]
    valid_s = [(jnp.dot(chosen[g], expand, preferred_element_type=f32) > 0.5) & (dist >= 0) for g in G]
    p_s = [softmax_rows(s[g] - slope[g] * dist_s, valid_s[g]).astype(bf) for g in G]
    o_s = [jnp.dot(p_s[g][:, past:], new_kv[:, vcol[g]].astype(bf), preferred_element_type=f32) for g in G]
    for i, pg in enumerate(pages):
        for g in G:
            o_s[g] = o_s[g] + jnp.dot(p_s[g][:, i * PAGE_ROWS:(i + 1) * PAGE_ROWS], pg[0, :, vcol[g]].astype(bf),
                                      preferred_element_type=f32)

    wb = win_ref.shape[1]
    new_w = _pad_rows(neww_ref[0], PAGE_ROWS)
    wk = [slice(g * HEAD_DIM, (g + 1) * HEAD_DIM) for g in G]
    wv = [slice((2 + g) * HEAD_DIM, (3 + g) * HEAD_DIM) for g in G]
    wpos = past - wb + lax.broadcasted_iota(jnp.int32, (1, wb + PAGE_ROWS), 1)
    dist = qpos - wpos
    valid_w = (dist >= 0) & (dist < WINDOW) & (wpos >= 0)
    dist_w = dist.astype(f32)
    s = [jnp.concatenate([_dot_nt(q[g], win_ref[0, :, wk[g]].astype(bf)),
                          _dot_nt(q[g], new_w[:, wk[g]].astype(bf))], axis=1) for g in G]
    p_w = [softmax_rows(s[g] - slope[g] * dist_w, valid_w).astype(bf) for g in G]
    o_w = [jnp.dot(p_w[g][:, :wb], win_ref[0, :, wv[g]].astype(bf), preferred_element_type=f32)
           + jnp.dot(p_w[g][:, wb:], new_w[:, wv[g]].astype(bf), preferred_element_type=f32) for g in G]

    for h in range(H_NSA):
        g, r = divmod(h, HPG)
        rs = slice(r * ROW_PAD, (r + 1) * ROW_PAD)
        o_ref[0, :, h * HEAD_DIM:(h + 1) * HEAD_DIM] = (gate[:, 3 * h:3 * h + 1] * o_c[g][rs]
                                                        + gate[:, 3 * h + 1:3 * h + 2] * o_s[g][rs]
                                                        + gate[:, 3 * h + 2:3 * h + 3] * o_w[g][rs])


def nsa_sample_pallas(page_table, q_b, gate_z, nsa_new, win_new, cache_nsa, cache_win, w1, w2, pe):
    B, T, _ = q_b.shape
    n_pages = page_table.shape[1]
    assert T == STEP_TOKENS and cache_win.shape[1] == min(WINDOW, n_pages * PAGE_ROWS)
    bf = jnp.bfloat16
    pad_t = lambda x: jnp.pad(x, ((0, 0), (0, ROW_PAD - T), (0, 0)))
    q = pad_t(q_b * SCALE).reshape(B, ROW_PAD, KV_GROUPS, HPG, HEAD_DIM).transpose(0, 2, 3, 1, 4)
    q = q.reshape(B, KV_GROUPS, HPG * ROW_PAD, HEAD_DIM).astype(bf)
    per_b = lambda b, pt: (b, 0, 0)
    page_spec = lambda p: pl.BlockSpec((1, PAGE_ROWS, 4 * KV_GROUPS * HEAD_DIM), lambda b, pt, p=p: (pt[b, p], 0, 0))
    out = pl.pallas_call(
        _nsa_sample_kernel,
        grid_spec=pltpu.PrefetchScalarGridSpec(
            num_scalar_prefetch=1, grid=(B,),
            in_specs=[pl.BlockSpec((1, KV_GROUPS, HPG * ROW_PAD, HEAD_DIM), lambda b, pt: (b, 0, 0, 0)),
                      pl.BlockSpec((1, ROW_PAD, 3 * H_NSA), per_b),
                      pl.BlockSpec((1, ROW_PAD, 4 * KV_GROUPS * HEAD_DIM), per_b),
                      pl.BlockSpec((1, ROW_PAD, 2 * KV_GROUPS * HEAD_DIM), per_b),
                      pl.BlockSpec((1, cache_win.shape[1], 2 * KV_GROUPS * HEAD_DIM), per_b),
                      pl.BlockSpec(w1.shape, lambda b, pt: (0, 0, 0, 0)),
                      pl.BlockSpec(w2.shape, lambda b, pt: (0, 0, 0)),
                      pl.BlockSpec(pe.shape, lambda b, pt: (0, 0, 0))]
                     + [page_spec(p) for p in range(n_pages)],
            out_specs=pl.BlockSpec((1, ROW_PAD, NSA_WIDTH), per_b),
            scratch_shapes=[pltpu.VMEM((2, n_pages * PAGE_ROWS, LANE), jnp.float32)]),
        out_shape=jax.ShapeDtypeStruct((B, ROW_PAD, NSA_WIDTH), jnp.float32),
        compiler_params=pltpu.CompilerParams(
            dimension_semantics=("arbitrary",), vmem_limit_bytes=VMEM_LIMIT_BYTES),
        name="nsa_sample",
    )(page_table, q, pad_t(gate_z), pad_t(nsa_new), pad_t(win_new), cache_win, w1, w2, pe,
      *([cache_nsa] * n_pages))
    return out[:, :T]


def mixer_prompt(x, w_in_pad, cmp_params):
    B, T, _ = x.shape
    z = project_pallas(x.reshape(B * T, D_MODEL), w_in_pad).reshape(B, T, IN_COLS_PAD)
    sb_kv = z[..., OFF_SBKV:OFF_QB].reshape(B, T, 2, H_SB, HEAD_DIM)
    nsa_kv = z[..., OFF_NSAKV:OFF_WIN].reshape(B, T, 4, KV_GROUPS, HEAD_DIM)
    win_kv = z[..., OFF_WIN:OFF_GATE].reshape(B, T, 2, KV_GROUPS, HEAD_DIM)
    ck, cv = compress_prompt_pallas(z[..., OFF_NSAKV:OFF_NSAKV + LANE], z[..., OFF_NSAKV + LANE:OFF_NSAKV + 2 * LANE],
                                    *cmp_params)
    o_b = nsa_prompt_pallas(z[..., OFF_QB:OFF_NSAKV], z[..., OFF_GATE:IN_COLS], ck, cv,
                            z[..., OFF_NSAKV:OFF_WIN], z[..., OFF_WIN:OFF_GATE])
    o_a = sb_prompt_pallas(z[..., :SB_WIDTH], z[..., OFF_SBKV:OFF_QB])
    new_win = win_kv[:, T - min(WINDOW, T):]
    return o_a, o_b, sb_kv, nsa_kv, new_win


def mixer_sample(x, cache_sb, cache_nsa, cache_win, page_table, w_in_pad, cmp_params):
    B, T, _ = x.shape
    z = project_pallas(x.reshape(B * T, D_MODEL), w_in_pad).reshape(B, T, IN_COLS_PAD)
    sb_new = z[..., OFF_SBKV:OFF_QB]
    nsa_new = z[..., OFF_NSAKV:OFF_WIN]
    win_new = z[..., OFF_WIN:OFF_GATE]
    n_pool, page = cache_sb.shape[:2]
    o_a = sb_sample_pallas(page_table, z[..., :SB_WIDTH], sb_new, cache_sb.reshape(n_pool, page, 2 * SB_WIDTH))
    win_flat = cache_win.reshape(B, cache_win.shape[1], 2 * KV_GROUPS * HEAD_DIM)
    o_b = nsa_sample_pallas(page_table, z[..., OFF_QB:OFF_NSAKV], z[..., OFF_GATE:IN_COLS], nsa_new, win_new,
                            cache_nsa.reshape(n_pool, page, 4 * KV_GROUPS * HEAD_DIM), win_flat, *cmp_params)
    new_win = jnp.concatenate([win_flat, win_new], axis=1)[:, T:].reshape(B, -1, 2, KV_GROUPS, HEAD_DIM)
    return (o_a, o_b, sb_new.reshape(B, T, 2, H_SB, HEAD_DIM), nsa_new.reshape(B, T, 4, KV_GROUPS, HEAD_DIM), new_win)


def block_out(x, o_a, o_b, mix_g, w_out, ln1_g, ln1_b, wq_t, keys, uv_tab, ln2_g, ln2_b):
    h = mixer_out_pallas(x.reshape(-1, D_MODEL), o_a.reshape(-1, SB_WIDTH), o_b.reshape(-1, NSA_WIDTH),
                         mix_g, w_out, ln1_g, ln1_b)
    e, g_t = peer_route_pallas(h, wq_t, keys)
    return peer_expert_pallas(e, h, g_t, uv_tab, ln2_g, ln2_b).reshape(x.shape)


def kernel(x_prompt, x_sample, cache_sb_kv, cache_nsa_kv, cache_win_kv, page_table, w_in, cmp_pe, cmp_w1, cmp_w2, mix_norm_g, w_out, ln1_g, ln1_b, peer_w_q, peer_sub_keys, peer_u, peer_v, ln2_g, ln2_b):
    l = 0
    w_in_pad = jnp.pad(w_in[l], ((0, 0), (0, IN_COLS_PAD - IN_COLS))).astype(jnp.bfloat16)
    cmp_params = compress_params(cmp_pe[l], cmp_w1[l], cmp_w2[l])
    o_a, o_b, sb_p, nsa_p, win_p = mixer_prompt(x_prompt, w_in_pad, cmp_params)
    wq_t = peer_w_q[l].T.astype(jnp.bfloat16)
    keys = peer_sub_keys[l].reshape(2 * PEER_HEADS, N_KEYS, D_KEY // 2).astype(jnp.bfloat16)
    uv_tab = pack_expert_slabs(peer_u[l], peer_v[l])
    w_out_bf = w_out[l].astype(jnp.bfloat16)
    h_p = block_out(x_prompt, o_a, o_b, mix_norm_g[l], w_out_bf, ln1_g[l], ln1_b[l],
                    wq_t, keys, uv_tab, ln2_g[l], ln2_b[l])
    o_a, o_b, sb_s, nsa_s, win_s = mixer_sample(x_sample, cache_sb_kv[l], cache_nsa_kv[l], cache_win_kv[l],
                                                page_table, w_in_pad, cmp_params)
    h_s = block_out(x_sample, o_a, o_b, mix_norm_g[l], w_out_bf, ln1_g[l], ln1_b[l],
                    wq_t, keys, uv_tab, ln2_g[l], ln2_b[l])
    return (h_p, h_s, sb_p[None], nsa_p[None], win_p[None], sb_s[None], nsa_s[None], win_s[None])
```

```python
import jax, jax.numpy as jnp
from jax import lax
from jax.experimental import pallas as pl
from jax.experimental.pallas import tpu as pltpu

D_MODEL = 1024
HEAD_DIM = 64
MIX_WIDTH = D_MODEL
SB_WIDTH = MIX_WIDTH // 2
NSA_WIDTH = MIX_WIDTH - SB_WIDTH
H_SB = SB_WIDTH // HEAD_DIM
H_NSA = NSA_WIDTH // HEAD_DIM
KV_GROUPS = 2
HPG = H_NSA // KV_GROUPS
COMP_BLOCK = 32
COMP_STRIDE = 16
COMP_HID = 128
SEL_BLOCK = 64
TOP_N = 8
WINDOW = 512
Q_BLOCK = 128
PEER_HEADS = 8
N_KEYS = 128
N_EXPERTS = N_KEYS * N_KEYS
PEER_TOPK = 16
D_KEY = 256
DEPTH = 1
ALPHA = (2.0 * DEPTH) ** 0.25
LN_EPS = 1e-5
NEG = -1e30
FORCE = 1e4
SCALE = HEAD_DIM ** -0.5

OFF_SBKV = SB_WIDTH
OFF_QB = 3 * SB_WIDTH
OFF_NSAKV = OFF_QB + NSA_WIDTH
OFF_WIN = OFF_NSAKV + 4 * KV_GROUPS * HEAD_DIM
OFF_GATE = OFF_WIN + 2 * KV_GROUPS * HEAD_DIM
IN_COLS = OFF_GATE + 3 * H_NSA

LANE = 128
V7X_VMEM_BYTES = 64 * 1024 * 1024
VMEM_LIMIT_BYTES = V7X_VMEM_BYTES * 3 // 4
STEP_TOKENS = 4
IN_COLS_PAD = -(-IN_COLS // LANE) * LANE
PROJ_ROWS = 512


PROJ_EDGES = (0, OFF_SBKV, OFF_QB, OFF_NSAKV, OFF_WIN, OFF_GATE, IN_COLS_PAD)


def _proj_kernel(x_ref, w_ref, *out_refs):
    z = jnp.dot(x_ref[...].astype(jnp.bfloat16), w_ref[...], preferred_element_type=jnp.float32)
    for o_ref, lo, hi in zip(out_refs, PROJ_EDGES[:-1], PROJ_EDGES[1:]):
        o_ref[...] = z[:, lo:hi]


def project_pallas(x2d, w_in_pad_bf16):
    n = x2d.shape[0]
    rows = min(PROJ_ROWS, n)
    assert all(e % LANE == 0 for e in PROJ_EDGES)
    widths = [hi - lo for lo, hi in zip(PROJ_EDGES[:-1], PROJ_EDGES[1:])]
    return pl.pallas_call(
        _proj_kernel,
        grid=(n // rows,),
        in_specs=[pl.BlockSpec((rows, D_MODEL), lambda i: (i, 0)),
                  pl.BlockSpec((D_MODEL, IN_COLS_PAD), lambda i: (0, 0))],
        out_specs=[pl.BlockSpec((rows, w), lambda i: (i, 0)) for w in widths],
        out_shape=[jax.ShapeDtypeStruct((n, w), jnp.float32) for w in widths],
        compiler_params=pltpu.CompilerParams(
            dimension_semantics=("arbitrary",), vmem_limit_bytes=VMEM_LIMIT_BYTES),
        name="in_proj",
    )(x2d, w_in_pad_bf16)


N_SEL = 32
N_CMP_PAD = 128
SEL_TILE = 512
ALIBI = [[2.0 ** (-8.0 * (g * HPG + r + 1) / H_NSA) for r in range(HPG)] for g in range(KV_GROUPS)]


def _dot_nt(a, b):
    return lax.dot_general(a, b, (((1,), (1,)), ((), ())), preferred_element_type=jnp.float32)


def _split_dot(a, b_exact):
    hi = a.astype(jnp.bfloat16)
    r1 = a - hi.astype(jnp.float32)
    mid = r1.astype(jnp.bfloat16)
    lo = (r1 - mid.astype(jnp.float32)).astype(jnp.bfloat16)
    d = lambda x: jnp.dot(x, b_exact, preferred_element_type=jnp.float32)
    return d(hi) + d(mid) + d(lo)


def _nsa_prompt_kernel(q_ref, gz_ref, ck_ref, cv_ref, nsa_ref, win_ref, o_ref, m_ref, l_ref, acc_ref):
    QB = Q_BLOCK
    qb = pl.program_id(1)
    q0 = qb * QB
    bf = jnp.bfloat16
    f32 = jnp.float32
    gate = jax.nn.sigmoid(gz_ref[0])
    qpos = q0 + lax.broadcasted_iota(jnp.int32, (QB, 1), 0)
    qh = [(q_ref[0, :, h * HEAD_DIM:(h + 1) * HEAD_DIM] * SCALE).astype(bf) for h in range(H_NSA)]

    def attend(qs, k, v, biases, valid):
        n = range(len(qs))
        s = [jnp.where(valid, _dot_nt(qs[i], k) + biases[i], NEG) for i in n]
        e = [jnp.where(valid, jnp.exp(s[i] - s[i].max(-1, keepdims=True)), 0.0) for i in n]
        tot = [e[i].sum(-1, keepdims=True) for i in n]
        inv = [1.0 / jnp.where(tot[i] > 0, tot[i], 1.0) for i in n]
        o = [jnp.dot(e[i].astype(bf), v, preferred_element_type=f32) * inv[i] for i in n]
        return o, [e[i] * inv[i] for i in n]

    def emit(h, branch, o, first):
        o = gate[:, 3 * h + branch:3 * h + branch + 1] * o
        sl = (0, slice(None), slice(h * HEAD_DIM, (h + 1) * HEAD_DIM))
        o_ref[sl] = o if first else o_ref[sl] + o

    chosen = []
    for g in range(KV_GROUPS):
        n_i = lax.broadcasted_iota(jnp.int32, (1, N_CMP_PAD), 1)
        cpos = n_i * COMP_STRIDE + (COMP_BLOCK - 1)
        n_cmp = (pl.num_programs(1) * QB - COMP_BLOCK) // COMP_STRIDE + 1
        valid_c = (cpos <= qpos) & (n_i < n_cmp)
        dist_c = (qpos - cpos).astype(f32)
        ckg = ck_ref[0, g].astype(bf)
        cvg = cv_ref[0, g].astype(bf)
        o_c, p_c = attend(qh[g * HPG:(g + 1) * HPG], ckg, cvg, [-ALIBI[g][r] * dist_c for r in range(HPG)], valid_c)
        psum = p_c[0]
        for r in range(HPG):
            emit(g * HPG + r, 0, o_c[r], True)
            if r:
                psum = psum + p_c[r]

        nn = lax.broadcasted_iota(jnp.int32, (N_CMP_PAD, N_SEL), 0)
        jj = lax.broadcasted_iota(jnp.int32, (N_CMP_PAD, N_SEL), 1)
        sel_map = ((nn * COMP_STRIDE) // SEL_BLOCK == jj).astype(bf)
        imp = _split_dot(psum, sel_map)
        blk = lax.broadcasted_iota(jnp.int32, (QB, N_SEL), 1)
        forced = (blk == qpos // SEL_BLOCK) | (blk == 0)
        imp = jnp.where(forced, FORCE, jnp.where(blk * SEL_BLOCK <= qpos, imp, -1.0))
        picked = jnp.zeros((QB, N_SEL), f32)
        for _ in range(TOP_N):
            mx = imp.max(-1, keepdims=True)
            first_max = jnp.where(imp == mx, blk, N_SEL).min(-1, keepdims=True)
            pick = blk == first_max
            picked = jnp.where(pick, 1.0, picked)
            imp = jnp.where(pick, -2.0, imp)
        chosen.append(picked.astype(bf))

        k0 = pl.multiple_of(jnp.maximum(q0 - WINDOW, 0), QB)
        k = win_ref[0, pl.ds(k0, WINDOW + QB), g * HEAD_DIM:(g + 1) * HEAD_DIM].astype(bf)
        v = win_ref[0, pl.ds(k0, WINDOW + QB), (2 + g) * HEAD_DIM:(3 + g) * HEAD_DIM].astype(bf)
        dist = qpos - (lax.broadcasted_iota(jnp.int32, (1, WINDOW + QB), 1) + k0)
        valid_w = (dist >= 0) & (dist < WINDOW)
        dist_w = dist.astype(f32)
        o_w, _ = attend(qh[g * HPG:(g + 1) * HPG], k, v, [-ALIBI[g][r] * dist_w for r in range(HPG)], valid_w)
        for r in range(HPG):
            emit(g * HPG + r, 2, o_w[r], False)

    m_ref[...] = jnp.full(m_ref.shape, NEG, f32)
    l_ref[...] = jnp.zeros(l_ref.shape, f32)
    acc_ref[...] = jnp.zeros(acc_ref.shape, f32)

    def sel_body(kt, carry):
        k0 = pl.multiple_of(kt * SEL_TILE, SEL_TILE)
        col = lax.broadcasted_iota(jnp.int32, (1, SEL_TILE), 1) + k0
        dist = qpos - col
        distf = dist.astype(f32)
        ej = lax.broadcasted_iota(jnp.int32, (N_SEL, SEL_TILE), 0)
        ec = lax.broadcasted_iota(jnp.int32, (N_SEL, SEL_TILE), 1) + k0
        expand = (ec // SEL_BLOCK == ej).astype(bf)
        heads = range(H_NSA)
        k = [nsa_ref[0, pl.ds(k0, SEL_TILE), (4 + g) * HEAD_DIM:(5 + g) * HEAD_DIM].astype(bf)
             for g in range(KV_GROUPS)]
        v = [nsa_ref[0, pl.ds(k0, SEL_TILE), (6 + g) * HEAD_DIM:(7 + g) * HEAD_DIM].astype(bf)
             for g in range(KV_GROUPS)]
        valid = [(jnp.dot(chosen[g], expand, preferred_element_type=f32) > 0.5) & (dist >= 0)
                 for g in range(KV_GROUPS)]
        s = [jnp.where(valid[h // HPG], _dot_nt(qh[h], k[h // HPG]) - ALIBI[h // HPG][h % HPG] * distf, NEG)
             for h in heads]
        m_old = [m_ref[h] for h in heads]
        m_new = [jnp.maximum(m_old[h], s[h].max(-1, keepdims=True)) for h in heads]
        p = [jnp.where(valid[h // HPG], jnp.exp(s[h] - m_new[h]), 0.0) for h in heads]
        for h in heads:
            alpha = jnp.exp(m_old[h] - m_new[h])
            l_ref[h] = alpha * l_ref[h] + p[h].sum(-1, keepdims=True)
            acc_ref[h] = alpha * acc_ref[h] + jnp.dot(p[h].astype(bf), v[h // HPG], preferred_element_type=f32)
            m_ref[h] = m_new[h]
        return carry

    lax.fori_loop(0, (q0 + QB + SEL_TILE - 1) // SEL_TILE, sel_body, 0)
    for h in range(H_NSA):
        tot = l_ref[h]
        emit(h, 1, acc_ref[h] / jnp.where(tot > 0, tot, 1.0), False)


def nsa_prompt_pallas(q_b, gate_z, ck, cv, nsa_kv, win_kv):
    B, T, _ = q_b.shape
    assert T % SEL_TILE == 0 and T >= WINDOW + Q_BLOCK and T // SEL_BLOCK <= N_SEL
    assert (T - COMP_BLOCK) // COMP_STRIDE + 1 <= N_CMP_PAD
    per_b = lambda b, i: (b, 0, 0)
    return pl.pallas_call(
        _nsa_prompt_kernel,
        grid=(B, T // Q_BLOCK),
        in_specs=[pl.BlockSpec((1, Q_BLOCK, NSA_WIDTH), lambda b, i: (b, i, 0)),
                  pl.BlockSpec((1, Q_BLOCK, LANE), lambda b, i: (b, i, 0)),
                  pl.BlockSpec((1, KV_GROUPS, N_CMP_PAD, HEAD_DIM), lambda b, i: (b, 0, 0, 0)),
                  pl.BlockSpec((1, KV_GROUPS, N_CMP_PAD, HEAD_DIM), lambda b, i: (b, 0, 0, 0)),
                  pl.BlockSpec((1, T, 4 * KV_GROUPS * HEAD_DIM), per_b),
                  pl.BlockSpec((1, T, 2 * KV_GROUPS * HEAD_DIM), per_b)],
        out_specs=pl.BlockSpec((1, Q_BLOCK, NSA_WIDTH), lambda b, i: (b, i, 0)),
        out_shape=jax.ShapeDtypeStruct((B, T, NSA_WIDTH), jnp.float32),
        scratch_shapes=[pltpu.VMEM((H_NSA, Q_BLOCK, 1), jnp.float32),
                        pltpu.VMEM((H_NSA, Q_BLOCK, 1), jnp.float32),
                        pltpu.VMEM((H_NSA, Q_BLOCK, HEAD_DIM), jnp.float32)],
        compiler_params=pltpu.CompilerParams(
            dimension_semantics=("arbitrary", "arbitrary"), vmem_limit_bytes=VMEM_LIMIT_BYTES),
        name="nsa_prompt",
    )(q_b, gate_z, ck, cv, nsa_kv, win_kv)


SB_TILE = 128
N_PICK = PEER_HEADS * PEER_TOPK


def _split2_dot(a, b_exact):
    hi = a.astype(jnp.bfloat16)
    lo = (a - hi.astype(jnp.float32)).astype(jnp.bfloat16)
    return (jnp.dot(hi, b_exact, preferred_element_type=jnp.float32)
            + jnp.dot(lo, b_exact, preferred_element_type=jnp.float32))


def _sb_prompt_kernel(q_ref, kv_ref, o_ref, carry_ref, acc_ref):
    TQ = TK = SB_TILE
    bf, f32 = jnp.bfloat16, jnp.float32
    qb = pl.program_id(1)
    qpos = qb * TQ + lax.broadcasted_iota(jnp.int32, (TQ, 1), 0)
    later = (lax.broadcasted_iota(jnp.int32, (TK, TK), 0) > lax.broadcasted_iota(jnp.int32, (TK, TK), 1)).astype(bf)
    carry_ref[...] = jnp.zeros(carry_ref.shape, f32)
    acc_ref[...] = jnp.zeros(acc_ref.shape, f32)

    def body(i, c):
        k0 = pl.multiple_of((qb - i) * TK, TK)
        valid = (lax.broadcasted_iota(jnp.int32, (1, TK), 1) + k0) < qpos
        heads = range(H_SB)
        z = [_dot_nt((q_ref[0, :, h * HEAD_DIM:(h + 1) * HEAD_DIM] * SCALE).astype(bf),
                     kv_ref[0, pl.ds(k0, TK), h * HEAD_DIM:(h + 1) * HEAD_DIM].astype(bf)) for h in heads]
        log_sig = [jnp.minimum(z[h], 0.0) - jnp.log(1.0 + jnp.exp(-jnp.abs(z[h]))) for h in heads]
        log_keep = [jnp.where(valid, log_sig[h] - z[h], 0.0) for h in heads]
        after = [_split2_dot(log_keep[h], later) for h in heads]
        w = [jnp.where(valid, jnp.exp(log_sig[h] + after[h] + carry_ref[h]), 0.0).astype(bf) for h in heads]
        for h in heads:
            v = kv_ref[0, pl.ds(k0, TK), SB_WIDTH + h * HEAD_DIM:SB_WIDTH + (h + 1) * HEAD_DIM].astype(bf)
            acc_ref[h] = acc_ref[h] + jnp.dot(w[h], v, preferred_element_type=f32)
            carry_ref[h] = carry_ref[h] + log_keep[h].sum(-1, keepdims=True)
        return c

    lax.fori_loop(0, qb + 1, body, 0)
    for h in range(H_SB):
        o_ref[0, :, h * HEAD_DIM:(h + 1) * HEAD_DIM] = acc_ref[h]


def sb_prompt_pallas(q_a, sb_kv):
    B, T, _ = q_a.shape
    assert T % SB_TILE == 0
    return pl.pallas_call(
        _sb_prompt_kernel,
        grid=(B, T // SB_TILE),
        in_specs=[pl.BlockSpec((1, SB_TILE, SB_WIDTH), lambda b, i: (b, i, 0)),
                  pl.BlockSpec((1, T, 2 * SB_WIDTH), lambda b, i: (b, 0, 0))],
        out_specs=pl.BlockSpec((1, SB_TILE, SB_WIDTH), lambda b, i: (b, i, 0)),
        out_shape=jax.ShapeDtypeStruct((B, T, SB_WIDTH), jnp.float32),
        scratch_shapes=[pltpu.VMEM((H_SB, SB_TILE, 1), jnp.float32),
                        pltpu.VMEM((H_SB, SB_TILE, HEAD_DIM), jnp.float32)],
        compiler_params=pltpu.CompilerParams(
            dimension_semantics=("arbitrary", "arbitrary"), vmem_limit_bytes=VMEM_LIMIT_BYTES),
        name="sb_prompt",
    )(q_a, sb_kv)


ROUTE_TOK = 256


def _top_rows(arrays, ids, n_top, outs):
    arrays = list(arrays)
    if ids is None:
        ids = lax.broadcasted_iota(jnp.int32, arrays[0].shape, 0)
    big = jnp.int32(1 << 30)
    for a in range(n_top):
        m = [s.max(0, keepdims=True) for s in arrays]
        ix = [jnp.where(s == mi, ids, big).min(0, keepdims=True) for s, mi in zip(arrays, m)]
        arrays = [jnp.where(ids == i, -jnp.inf, s) for s, i in zip(arrays, ix)]
        for (vals_ref, idx_ref), mi, i in zip(outs, m, ix):
            vals_ref[a:a + 1, :] = mi
            idx_ref[a:a + 1, :] = i


CAND_HEAD = PEER_TOPK
CAND_SIDE = 8
N_CAND_ROWS = CAND_HEAD + CAND_SIDE * (PEER_TOPK - 1)


def _peer_route_kernel(h_ref, wq_t_ref, keys_ref, e_ref, g_ref, s1_ref, i1_ref, s2_ref, i2_ref, ts_ref, ti_ref,
                       et_ref):
    bf, f32 = jnp.bfloat16, jnp.float32
    q_t = _dot_nt(wq_t_ref[...], h_ref[...].astype(bf))
    half = D_KEY // 2
    tn = q_t.shape[1]
    ci = lax.broadcasted_iota(jnp.int32, (N_CAND_ROWS, tn), 0)
    side = jnp.maximum(ci - CAND_HEAD, 0)
    cand_id = jnp.where(ci < CAND_HEAD, ci * PEER_TOPK, (side % CAND_SIDE) * PEER_TOPK + 1 + side // CAND_SIDE)
    for p in range(PEER_HEADS):
        scores = [jnp.dot(keys_ref[2 * p + hf], q_t[(2 * p + hf) * half:(2 * p + hf + 1) * half, :].astype(bf),
                          preferred_element_type=f32) for hf in range(2)]
        _top_rows(scores, None, PEER_TOPK, ((s1_ref, i1_ref), (s2_ref, i2_ref)))
        s1 = s1_ref[...]
        cand = jnp.concatenate([s1 + s2_ref[0:1, :]]
                               + [s1[:CAND_SIDE] + s2_ref[b:b + 1, :] for b in range(1, PEER_TOPK)], axis=0)
        _top_rows([cand], cand_id, PEER_TOPK, ((ts_ref, ti_ref),))
        ti = ti_ref[...]
        a_of, b_of = ti // PEER_TOPK, ti % PEER_TOPK
        k1 = jnp.zeros(ti.shape, jnp.int32)
        k2 = jnp.zeros(ti.shape, jnp.int32)
        for a in range(PEER_TOPK):
            k1 = jnp.where(a_of == a, i1_ref[a:a + 1, :], k1)
            k2 = jnp.where(b_of == a, i2_ref[a:a + 1, :], k2)
        ts = ts_ref[...]
        ex = jnp.exp(ts - ts.max(0, keepdims=True))
        et_ref[p * PEER_TOPK:(p + 1) * PEER_TOPK, :] = k1 * N_KEYS + k2
        g_ref[p * PEER_TOPK:(p + 1) * PEER_TOPK, :] = ex / ex.sum(0, keepdims=True)
    e_ref[...] = et_ref[...].T


def peer_route_pallas(h, wq_t, keys):
    n = h.shape[0]
    assert n % ROUTE_TOK == 0
    tn = ROUTE_TOK
    top = lambda dt: pltpu.VMEM((PEER_TOPK, tn), dt)
    return pl.pallas_call(
        _peer_route_kernel,
        grid=(n // tn,),
        in_specs=[pl.BlockSpec((tn, D_MODEL), lambda i: (i, 0)),
                  pl.BlockSpec((PEER_HEADS * D_KEY, D_MODEL), lambda i: (0, 0)),
                  pl.BlockSpec((2 * PEER_HEADS, N_KEYS, D_KEY // 2), lambda i: (0, 0, 0))],
        out_specs=[pl.BlockSpec((tn, N_PICK), lambda i: (i, 0)),
                   pl.BlockSpec((N_PICK, tn), lambda i: (0, i))],
        out_shape=[jax.ShapeDtypeStruct((n, N_PICK), jnp.int32),
                   jax.ShapeDtypeStruct((N_PICK, n), jnp.float32)],
        scratch_shapes=[top(jnp.float32), top(jnp.int32), top(jnp.float32), top(jnp.int32),
                        top(jnp.float32), top(jnp.int32), pltpu.VMEM((N_PICK, tn), jnp.int32)],
        compiler_params=pltpu.CompilerParams(
            dimension_semantics=("arbitrary",), vmem_limit_bytes=VMEM_LIMIT_BYTES),
        name="peer_route",
    )(h, wq_t, keys)


PEER_TOK_BLOCK = 128
PEER_SLOTS = 4
ID_PAD = 8
SUBLANE_PICKS = 8
VEC_ROWS = D_MODEL // LANE
SLAB = 2 * VEC_ROWS


def pack_expert_slabs(u_tab, v_tab):
    rows = lambda tab: tab.reshape(N_EXPERTS, VEC_ROWS, LANE)
    return jnp.concatenate([rows(u_tab), rows(v_tab)], axis=1).astype(jnp.bfloat16)


def _peer_expert_kernel(e_hbm, h_ref, coef_ref, uv_hbm, g_ref, b_ref, y_ref,
                        e_smem, uvbuf, a_ref, w_ref, sem_e, sem_rows):
    TB = coef_ref.shape[1]
    i = pl.program_id(0)
    ahead = PEER_SLOTS - 2
    ids = pltpu.make_async_copy(e_hbm.at[pl.ds(i * TB, TB + ID_PAD), :], e_smem, sem_e)
    ids.start()
    ids.wait()

    def issue(t, slot):
        for k in range(N_PICK):
            pltpu.make_async_copy(uv_hbm.at[e_smem[t, k]], uvbuf.at[slot, k],
                                  sem_rows.at[slot]).start()

    def wait_rows(slot):
        pltpu.make_async_copy(uv_hbm.at[pl.ds(0, N_PICK)], uvbuf.at[slot], sem_rows.at[slot]).wait()

    coef_t = coef_ref[...]
    tok = lax.broadcasted_iota(jnp.int32, coef_t.shape, 1)
    g_ln, b_ln = g_ref[...], b_ref[...]

    sub = lax.broadcasted_iota(jnp.int32, (VEC_ROWS, LANE), 0)

    def fold(x, y, d):
        return jnp.where(sub % (2 * d) < d, x + pltpu.roll(x, VEC_ROWS - d, 0), y + pltpu.roll(y, d, 0))

    def slab(t, k):
        return uvbuf[t % PEER_SLOTS, k].astype(jnp.float32)

    def h_rows(t):
        return h_ref[pl.ds(pl.multiple_of(t * VEC_ROWS, VEC_ROWS), VEC_ROWS), :]

    def stage_a(t):
        h = h_rows(t)
        for k0 in range(0, N_PICK, SUBLANE_PICKS):
            q = [slab(t, k)[:VEC_ROWS] * h for k in range(k0, k0 + SUBLANE_PICKS)]
            for d in (4, 2, 1):
                q = [fold(q[j], q[j + d], d) for j in range(d)]
            a_ref[t % 2, k0:k0 + SUBLANE_PICKS, :] = q[0]

    def stage_b_weights(t):
        a = jnp.sum(a_ref[t % 2], axis=-1, keepdims=True)
        c = jnp.sum(jnp.where(tok == t, coef_t, 0.0), axis=-1, keepdims=True)
        w_ref[...] = jnp.broadcast_to(c * jax.nn.gelu(a), (N_PICK, LANE))

    def stage_b_apply(t):
        parts = [w_ref[k:k + 1, :] * slab(t, k)[VEC_ROWS:] for k in range(SUBLANE_PICKS)]
        for k in range(SUBLANE_PICKS, N_PICK):
            parts[k % SUBLANE_PICKS] = parts[k % SUBLANE_PICKS] + w_ref[k:k + 1, :] * slab(t, k)[VEC_ROWS:]
        while len(parts) > 1:
            parts = [parts[j] + parts[j + 1] for j in range(0, len(parts), 2)]
        x = ALPHA * h_rows(t) + parts[0]
        mu = jnp.mean(x, keepdims=True)
        var = jnp.mean(jnp.square(x - mu), keepdims=True)
        y_ref[pl.ds(pl.multiple_of(t * VEC_ROWS, VEC_ROWS), VEC_ROWS), :] = (
            (x - mu) * lax.rsqrt(var + LN_EPS) * g_ln + b_ln)

    @pl.when(i == 0)
    def _():
        for t in range(ahead):
            issue(t, t)

    issue(ahead, ahead)
    wait_rows(0)
    stage_a(0)

    def body(t, carry):
        issue(t + ahead, (t + ahead) % PEER_SLOTS)
        wait_rows(t % PEER_SLOTS)
        stage_b_weights(t - 1)
        stage_a(t)
        stage_b_apply(t - 1)
        return carry

    lax.fori_loop(1, TB, body, 0)
    stage_b_weights(TB - 1)
    stage_b_apply(TB - 1)

    @pl.when(i == pl.num_programs(0) - 1)
    def _():
        for s in range(ahead):
            wait_rows(s)


def peer_expert_pallas(e, h, coef_t, uv_slabs, ln_g, ln_b):
    n = h.shape[0]
    tb = min(PEER_TOK_BLOCK, n)
    assert n % tb == 0 and tb % PEER_SLOTS == 0 and PEER_SLOTS - 1 <= ID_PAD
    fixed = lambda i: (0, 0)
    y = pl.pallas_call(
        _peer_expert_kernel,
        grid=(n // tb,),
        in_specs=[pl.BlockSpec(memory_space=pl.ANY),
                  pl.BlockSpec((tb * VEC_ROWS, LANE), lambda i: (i, 0)),
                  pl.BlockSpec((N_PICK, tb), lambda i: (0, i)),
                  pl.BlockSpec(memory_space=pl.ANY),
                  pl.BlockSpec((VEC_ROWS, LANE), fixed),
                  pl.BlockSpec((VEC_ROWS, LANE), fixed)],
        out_specs=pl.BlockSpec((tb * VEC_ROWS, LANE), lambda i: (i, 0)),
        out_shape=jax.ShapeDtypeStruct((n * VEC_ROWS, LANE), jnp.float32),
        scratch_shapes=[pltpu.SMEM((tb + ID_PAD, N_PICK), jnp.int32),
                        pltpu.VMEM((PEER_SLOTS, N_PICK, SLAB, LANE), jnp.bfloat16),
                        pltpu.VMEM((2, N_PICK, LANE), jnp.float32),
                        pltpu.VMEM((N_PICK, LANE), jnp.float32),
                        pltpu.SemaphoreType.DMA,
                        pltpu.SemaphoreType.DMA((PEER_SLOTS,))],
        compiler_params=pltpu.CompilerParams(dimension_semantics=("arbitrary",)),
        name="peer_experts",
    )(jnp.pad(e, ((0, ID_PAD), (0, 0))), h.reshape(n * VEC_ROWS, LANE), coef_t, uv_slabs,
      ln_g.reshape(VEC_ROWS, LANE), ln_b.reshape(VEC_ROWS, LANE))
    return y.reshape(n, D_MODEL)


MID_ROWS = 512


def _mid_kernel(x_ref, oa_ref, ob_ref, mg_ref, w_ref, g_ref, b_ref, h_ref):
    bf, f32 = jnp.bfloat16, jnp.float32

    def normed(o, g):
        return (o * lax.rsqrt(jnp.mean(jnp.square(o), -1, keepdims=True) + LN_EPS) * g).astype(bf)

    m = (jnp.dot(normed(oa_ref[...], mg_ref[:, :SB_WIDTH]), w_ref[:SB_WIDTH, :], preferred_element_type=f32)
         + jnp.dot(normed(ob_ref[...], mg_ref[:, SB_WIDTH:]), w_ref[SB_WIDTH:, :], preferred_element_type=f32))
    x = ALPHA * x_ref[...] + m
    mu = x.mean(-1, keepdims=True)
    var = jnp.square(x - mu).mean(-1, keepdims=True)
    h_ref[...] = (x - mu) * lax.rsqrt(var + LN_EPS) * g_ref[...] + b_ref[...]


def mixer_out_pallas(x, o_a, o_b, mix_g, w_out_bf16, ln_g, ln_b):
    n = x.shape[0]
    rows = min(MID_ROWS, n)
    assert n % rows == 0
    row = lambda i: (i, 0)
    fixed = lambda i: (0, 0)
    return pl.pallas_call(
        _mid_kernel,
        grid=(n // rows,),
        in_specs=[pl.BlockSpec((rows, D_MODEL), row),
                  pl.BlockSpec((rows, SB_WIDTH), row),
                  pl.BlockSpec((rows, NSA_WIDTH), row),
                  pl.BlockSpec((1, MIX_WIDTH), fixed),
                  pl.BlockSpec((MIX_WIDTH, D_MODEL), fixed),
                  pl.BlockSpec((1, D_MODEL), fixed),
                  pl.BlockSpec((1, D_MODEL), fixed)],
        out_specs=pl.BlockSpec((rows, D_MODEL), row),
        out_shape=jax.ShapeDtypeStruct((n, D_MODEL), jnp.float32),
        compiler_params=pltpu.CompilerParams(
            dimension_semantics=("arbitrary",), vmem_limit_bytes=VMEM_LIMIT_BYTES),
        name="mixer_out",
    )(x, o_a, o_b, mix_g.reshape(1, MIX_WIDTH), w_out_bf16, ln_g.reshape(1, D_MODEL), ln_b.reshape(1, D_MODEL))


PAGE_ROWS = 128
ROW_PAD = 8


def _pad_rows(x, rows):
    return jnp.concatenate([x, jnp.zeros((rows - x.shape[0], x.shape[1]), x.dtype)], axis=0)


def _sb_sample_kernel(pt_ref, q_ref, new_ref, *rest):
    n_pages = len(rest) - 1
    pages, o_ref = rest[:n_pages], rest[n_pages]
    bf, f32 = jnp.bfloat16, jnp.float32
    R = STEP_TOKENS * H_SB
    q_t = q_ref[0]
    rowi = lax.broadcasted_iota(jnp.int32, (R, 1), 0)
    later = (lax.broadcasted_iota(jnp.int32, (PAGE_ROWS, PAGE_ROWS), 0)
             > lax.broadcasted_iota(jnp.int32, (PAGE_ROWS, PAGE_ROWS), 1)).astype(bf)

    key_i = lax.broadcasted_iota(jnp.int32, (1, PAGE_ROWS), 1)
    newest = key_i < rowi // H_SB
    tiles = [_pad_rows(new_ref[0], PAGE_ROWS)] + [pages[p][0] for p in reversed(range(n_pages))]
    n = range(len(tiles))
    z = [_dot_nt(q_t, tiles[i][:, :SB_WIDTH].astype(bf)) for i in n]
    log_sig = [jnp.minimum(z[i], 0.0) - jnp.log(1.0 + jnp.exp(-jnp.abs(z[i]))) for i in n]
    log_keep = [log_sig[i] - z[i] for i in n]
    log_keep[0] = jnp.where(newest, log_keep[0], 0.0)
    after = [_split2_dot(log_keep[i], later) for i in n]
    carry = jnp.zeros((R, 1), f32)
    acc = jnp.zeros((R, SB_WIDTH), f32)
    for i in n:
        w = jnp.exp(log_sig[i] + after[i] + carry)
        if i == 0:
            w = jnp.where(newest, w, 0.0)
        acc = acc + jnp.dot(w.astype(bf), tiles[i][:, SB_WIDTH:].astype(bf), preferred_element_type=f32)
        carry = carry + log_keep[i].sum(-1, keepdims=True)
    head_of_col = lax.broadcasted_iota(jnp.int32, (1, SB_WIDTH), 1) // HEAD_DIM
    own = jnp.where(head_of_col == rowi % H_SB, acc, 0.0)
    for t in range(STEP_TOKENS):
        o_ref[0, t:t + 1, :] = own[t * H_SB:(t + 1) * H_SB, :].sum(0, keepdims=True)


def sb_sample_pallas(page_table, q_a, sb_new, cache_sb):
    B, T, _ = q_a.shape
    n_pages = page_table.shape[1]
    assert T == STEP_TOKENS
    eye = jnp.eye(H_SB, dtype=jnp.float32)
    q_t = (q_a.reshape(B, T, 1, H_SB, HEAD_DIM) * eye[None, None, :, :, None] * SCALE)
    q_t = q_t.reshape(B, T * H_SB, SB_WIDTH).astype(jnp.bfloat16)
    new = jnp.pad(sb_new, ((0, 0), (0, ROW_PAD - T), (0, 0)))
    page_spec = lambda p: pl.BlockSpec((1, PAGE_ROWS, 2 * SB_WIDTH), lambda b, pt, p=p: (pt[b, p], 0, 0))
    return pl.pallas_call(
        _sb_sample_kernel,
        grid_spec=pltpu.PrefetchScalarGridSpec(
            num_scalar_prefetch=1, grid=(B,),
            in_specs=[pl.BlockSpec((1, T * H_SB, SB_WIDTH), lambda b, pt: (b, 0, 0)),
                      pl.BlockSpec((1, ROW_PAD, 2 * SB_WIDTH), lambda b, pt: (b, 0, 0))]
                     + [page_spec(p) for p in range(n_pages)],
            out_specs=pl.BlockSpec((1, T, SB_WIDTH), lambda b, pt: (b, 0, 0))),
        out_shape=jax.ShapeDtypeStruct((B, T, SB_WIDTH), jnp.float32),
        compiler_params=pltpu.CompilerParams(
            dimension_semantics=("arbitrary",), vmem_limit_bytes=VMEM_LIMIT_BYTES),
        name="sb_sample",
    )(page_table, q_t, new, *([cache_sb] * n_pages))


def _compress_tokens(chunk_rows, pe_ref, w1_ref, w2_ref):
    bf, f32 = jnp.bfloat16, jnp.float32
    out = []
    for j in range(2):
        h_a = h_b = None
        for l in range(COMP_STRIDE):
            x = chunk_rows(j, l)
            a = jnp.dot((x + pe_ref[j, l:l + 1, :]).astype(bf), w1_ref[j, l], preferred_element_type=f32)
            b = jnp.dot((x + pe_ref[j, COMP_STRIDE + l:COMP_STRIDE + l + 1, :]).astype(bf),
                        w1_ref[j, COMP_STRIDE + l], preferred_element_type=f32)
            h_a = a if h_a is None else h_a + a
            h_b = b if h_b is None else h_b + b
        hdn = jax.nn.gelu(h_a + pltpu.roll(h_b, h_b.shape[0] - 1, 0)).astype(bf)
        out.append([jnp.dot(hdn[:, g * COMP_HID:(g + 1) * COMP_HID], w2_ref[j], preferred_element_type=f32)
                    for g in range(KV_GROUPS)])
    return out


def _compress_prompt_kernel(xk_ref, xv_ref, w1_ref, w2_ref, pe_ref, ck_ref, cv_ref):
    x_refs = (xk_ref, xv_ref)
    chunks = xk_ref.shape[1] // COMP_STRIDE
    cmp_kv = _compress_tokens(lambda j, l: x_refs[j][0, pl.ds(l, chunks, stride=COMP_STRIDE), :],
                              pe_ref, w1_ref, w2_ref)
    for g in range(KV_GROUPS):
        ck_ref[0, g] = cmp_kv[0][g]
        cv_ref[0, g] = cmp_kv[1][g]


def compress_prompt_pallas(nsa_kv, w1, w2, pe):
    B, T, _ = nsa_kv.shape
    chunks = T // COMP_STRIDE
    assert chunks == N_CMP_PAD
    out = jax.ShapeDtypeStruct((B, KV_GROUPS, chunks, HEAD_DIM), jnp.float32)
    return pl.pallas_call(
        _compress_prompt_kernel,
        grid=(B,),
        in_specs=[pl.BlockSpec((1, T, LANE), lambda b: (b, 0, 0)), pl.BlockSpec((1, T, LANE), lambda b: (b, 0, 1)),
                  pl.BlockSpec(w1.shape, lambda b: (0, 0, 0, 0)),
                  pl.BlockSpec(w2.shape, lambda b: (0, 0, 0)),
                  pl.BlockSpec(pe.shape, lambda b: (0, 0, 0))],
        out_specs=[pl.BlockSpec((1, KV_GROUPS, chunks, HEAD_DIM), lambda b: (b, 0, 0, 0))] * 2,
        out_shape=[out, out],
        compiler_params=pltpu.CompilerParams(
            dimension_semantics=("arbitrary",), vmem_limit_bytes=VMEM_LIMIT_BYTES),
        name="compress_prompt",
    )(nsa_kv, nsa_kv, w1, w2, pe)


def compress_params(cmp_pe, cmp_w1, cmp_w2):
    zero = jnp.zeros_like(cmp_w1)
    w1 = jnp.concatenate([jnp.concatenate([cmp_w1, zero], axis=-1),
                          jnp.concatenate([zero, cmp_w1], axis=-1)], axis=-2).astype(jnp.bfloat16)
    return w1, cmp_w2.astype(jnp.bfloat16), jnp.concatenate([cmp_pe, cmp_pe], axis=-1)


def _nsa_sample_kernel(pt_ref, q_ref, gz_ref, new_ref, neww_ref, win_ref, w1_ref, w2_ref, pe_ref, *rest):
    n_pages = len(rest) - 2
    pages, o_ref, x_ref = rest[:n_pages], rest[n_pages], rest[n_pages + 1]
    bf, f32 = jnp.bfloat16, jnp.float32
    past = n_pages * PAGE_ROWS
    n_cmp = (past + STEP_TOKENS - COMP_BLOCK) // COMP_STRIDE + 1
    n_sel = -(-(past + STEP_TOKENS) // SEL_BLOCK)
    R = HPG * ROW_PAD

    for i, pg in enumerate(pages):
        for j in range(2):
            x_ref[j, i * PAGE_ROWS:(i + 1) * PAGE_ROWS, :] = pg[0, :, j * LANE:(j + 1) * LANE]
    cmp_kv = _compress_tokens(lambda j, l: x_ref[j, pl.ds(l, past // COMP_STRIDE, stride=COMP_STRIDE), :],
                              pe_ref, w1_ref, w2_ref)

    rowi = lax.broadcasted_iota(jnp.int32, (R, 1), 0)
    qpos = past + rowi % ROW_PAD
    gate = jax.nn.sigmoid(gz_ref[0])

    def softmax_rows(s, valid):
        s = jnp.where(valid, s, NEG)
        e = jnp.where(valid, jnp.exp(s - s.max(-1, keepdims=True)), 0.0)
        tot = e.sum(-1, keepdims=True)
        return e / jnp.where(tot > 0, tot, 1.0)

    G = range(KV_GROUPS)
    q = [q_ref[0, g] for g in G]
    slope = []
    for g in G:
        sl = jnp.full((R, 1), ALIBI[g][HPG - 1], f32)
        for r in range(HPG - 1):
            sl = jnp.where(rowi // ROW_PAD == r, ALIBI[g][r], sl)
        slope.append(sl)

    n_i = lax.broadcasted_iota(jnp.int32, (1, PAGE_ROWS), 1)
    cpos = n_i * COMP_STRIDE + (COMP_BLOCK - 1)
    valid_c = (cpos <= qpos) & (n_i < n_cmp)
    dist_c = (qpos - cpos).astype(f32)
    p_c = [softmax_rows(_dot_nt(q[g], cmp_kv[0][g].astype(bf)) - slope[g] * dist_c, valid_c) for g in G]
    o_c = [jnp.dot(p_c[g].astype(bf), cmp_kv[1][g].astype(bf), preferred_element_type=f32) for g in G]

    nn = lax.broadcasted_iota(jnp.int32, (PAGE_ROWS, LANE), 0)
    jj = lax.broadcasted_iota(jnp.int32, (PAGE_ROWS, LANE), 1)
    sel_map = ((nn * COMP_STRIDE) // SEL_BLOCK == jj).astype(bf)
    blk = lax.broadcasted_iota(jnp.int32, (ROW_PAD, LANE), 1)
    qp8 = past + lax.broadcasted_iota(jnp.int32, (ROW_PAD, 1), 0)
    forced = (blk == qp8 // SEL_BLOCK) | (blk == 0)
    imp = []
    for g in G:
        psum = p_c[g][0:ROW_PAD]
        for r in range(1, HPG):
            psum = psum + p_c[g][r * ROW_PAD:(r + 1) * ROW_PAD]
        im = jnp.where(forced, FORCE, jnp.where(blk * SEL_BLOCK <= qp8, _split_dot(psum, sel_map), -1.0))
        imp.append(jnp.where(blk < n_sel, im, -3.0))
    picked = [jnp.zeros((ROW_PAD, LANE), f32) for g in G]
    for _ in range(min(TOP_N, n_sel)):
        for g in G:
            mx = imp[g].max(-1, keepdims=True)
            first_max = jnp.where(imp[g] == mx, blk, LANE).min(-1, keepdims=True)
            pick = blk == first_max
            picked[g] = jnp.where(pick, 1.0, picked[g])
            imp[g] = jnp.where(pick, -4.0, imp[g])
    chosen = [jnp.concatenate([picked[g].astype(bf)] * HPG, axis=0) for g in G]

    new_kv = _pad_rows(new_ref[0], PAGE_ROWS)
    kcol = [slice((4 + g) * HEAD_DIM, (5 + g) * HEAD_DIM) for g in G]
    vcol = [slice((6 + g) * HEAD_DIM, (7 + g) * HEAD_DIM) for g in G]
    n_key = past + PAGE_ROWS
    col = lax.broadcasted_iota(jnp.int32, (1, n_key), 1)
    ej = lax.broadcasted_iota(jnp.int32, (LANE, n_key), 0)
    ec = lax.broadcasted_iota(jnp.int32, (LANE, n_key), 1)
    expand = (ec // SEL_BLOCK == ej).astype(bf)
    dist = qpos - col
    dist_s = dist.astype(f32)
    s = [jnp.concatenate([_dot_nt(q[g], pg[0, :, kcol[g]].astype(bf)) for pg in pages]
                         + [_dot_nt(q[g], new_kv[:, kcol[g]].astype(bf))], axis=1) for g in G]
    valid_s = [(jnp.dot(chosen[g], expand, preferred_element_type=f32) > 0.5) & (dist >= 0) for g in G]
    p_s = [softmax_rows(s[g] - slope[g] * dist_s, valid_s[g]).astype(bf) for g in G]
    o_s = [jnp.dot(p_s[g][:, past:], new_kv[:, vcol[g]].astype(bf), preferred_element_type=f32) for g in G]
    for i, pg in enumerate(pages):
        for g in G:
            o_s[g] = o_s[g] + jnp.dot(p_s[g][:, i * PAGE_ROWS:(i + 1) * PAGE_ROWS], pg[0, :, vcol[g]].astype(bf),
                                      preferred_element_type=f32)

    wb = win_ref.shape[1]
    new_w = _pad_rows(neww_ref[0], PAGE_ROWS)
    wk = [slice(g * HEAD_DIM, (g + 1) * HEAD_DIM) for g in G]
    wv = [slice((2 + g) * HEAD_DIM, (3 + g) * HEAD_DIM) for g in G]
    wpos = past - wb + lax.broadcasted_iota(jnp.int32, (1, wb + PAGE_ROWS), 1)
    dist = qpos - wpos
    valid_w = (dist >= 0) & (dist < WINDOW) & (wpos >= 0)
    dist_w = dist.astype(f32)
    s = [jnp.concatenate([_dot_nt(q[g], win_ref[0, :, wk[g]].astype(bf)),
                          _dot_nt(q[g], new_w[:, wk[g]].astype(bf))], axis=1) for g in G]
    p_w = [softmax_rows(s[g] - slope[g] * dist_w, valid_w).astype(bf) for g in G]
    o_w = [jnp.dot(p_w[g][:, :wb], win_ref[0, :, wv[g]].astype(bf), preferred_element_type=f32)
           + jnp.dot(p_w[g][:, wb:], new_w[:, wv[g]].astype(bf), preferred_element_type=f32) for g in G]

    for h in range(H_NSA):
        g, r = divmod(h, HPG)
        rs = slice(r * ROW_PAD, (r + 1) * ROW_PAD)
        o_ref[0, :, h * HEAD_DIM:(h + 1) * HEAD_DIM] = (gate[:, 3 * h:3 * h + 1] * o_c[g][rs]
                                                        + gate[:, 3 * h + 1:3 * h + 2] * o_s[g][rs]
                                                        + gate[:, 3 * h + 2:3 * h + 3] * o_w[g][rs])


def nsa_sample_pallas(page_table, q_b, gate_z, nsa_new, win_new, cache_nsa, cache_win, w1, w2, pe):
    B, T, _ = q_b.shape
    n_pages = page_table.shape[1]
    assert T == STEP_TOKENS and cache_win.shape[1] == min(WINDOW, n_pages * PAGE_ROWS)
    bf = jnp.bfloat16
    pad_t = lambda x: jnp.pad(x, ((0, 0), (0, ROW_PAD - T), (0, 0)))
    q = pad_t(q_b * SCALE).reshape(B, ROW_PAD, KV_GROUPS, HPG, HEAD_DIM).transpose(0, 2, 3, 1, 4)
    q = q.reshape(B, KV_GROUPS, HPG * ROW_PAD, HEAD_DIM).astype(bf)
    per_b = lambda b, pt: (b, 0, 0)
    page_spec = lambda p: pl.BlockSpec((1, PAGE_ROWS, 4 * KV_GROUPS * HEAD_DIM), lambda b, pt, p=p: (pt[b, p], 0, 0))
    out = pl.pallas_call(
        _nsa_sample_kernel,
        grid_spec=pltpu.PrefetchScalarGridSpec(
            num_scalar_prefetch=1, grid=(B,),
            in_specs=[pl.BlockSpec((1, KV_GROUPS, HPG * ROW_PAD, HEAD_DIM), lambda b, pt: (b, 0, 0, 0)),
                      pl.BlockSpec((1, ROW_PAD, LANE), per_b),
                      pl.BlockSpec((1, ROW_PAD, 4 * KV_GROUPS * HEAD_DIM), per_b),
                      pl.BlockSpec((1, ROW_PAD, 2 * KV_GROUPS * HEAD_DIM), per_b),
                      pl.BlockSpec((1, cache_win.shape[1], 2 * KV_GROUPS * HEAD_DIM), per_b),
                      pl.BlockSpec(w1.shape, lambda b, pt: (0, 0, 0, 0)),
                      pl.BlockSpec(w2.shape, lambda b, pt: (0, 0, 0)),
                      pl.BlockSpec(pe.shape, lambda b, pt: (0, 0, 0))]
                     + [page_spec(p) for p in range(n_pages)],
            out_specs=pl.BlockSpec((1, ROW_PAD, NSA_WIDTH), per_b),
            scratch_shapes=[pltpu.VMEM((2, n_pages * PAGE_ROWS, LANE), jnp.float32)]),
        out_shape=jax.ShapeDtypeStruct((B, ROW_PAD, NSA_WIDTH), jnp.float32),
        compiler_params=pltpu.CompilerParams(
            dimension_semantics=("arbitrary",), vmem_limit_bytes=VMEM_LIMIT_BYTES),
        name="nsa_sample",
    )(page_table, q, pad_t(gate_z), pad_t(nsa_new), pad_t(win_new), cache_win, w1, w2, pe,
      *([cache_nsa] * n_pages))
    return out[:, :T]


def mixer_prompt(x, w_in_pad, cmp_params):
    B, T, _ = x.shape
    q_a, sb_kv, q_b, nsa_kv, win_kv, gate_z = [
        z.reshape(B, T, -1) for z in project_pallas(x.reshape(B * T, D_MODEL), w_in_pad)]
    ck, cv = compress_prompt_pallas(nsa_kv, *cmp_params)
    o_b = nsa_prompt_pallas(q_b, gate_z, ck, cv, nsa_kv, win_kv)
    o_a = sb_prompt_pallas(q_a, sb_kv)
    new_win = win_kv[:, T - min(WINDOW, T):].reshape(B, -1, 2, KV_GROUPS, HEAD_DIM)
    return (o_a, o_b, sb_kv.reshape(B, T, 2, H_SB, HEAD_DIM), nsa_kv.reshape(B, T, 4, KV_GROUPS, HEAD_DIM), new_win)


def mixer_sample(x, cache_sb, cache_nsa, cache_win, page_table, w_in_pad, cmp_params):
    B, T, _ = x.shape
    q_a, sb_new, q_b, nsa_new, win_new, gate_z = [
        z.reshape(B, T, -1) for z in project_pallas(x.reshape(B * T, D_MODEL), w_in_pad)]
    n_pool, page = cache_sb.shape[:2]
    o_a = sb_sample_pallas(page_table, q_a, sb_new, cache_sb.reshape(n_pool, page, 2 * SB_WIDTH))
    win_flat = cache_win.reshape(B, cache_win.shape[1], 2 * KV_GROUPS * HEAD_DIM)
    o_b = nsa_sample_pallas(page_table, q_b, gate_z, nsa_new, win_new,
                            cache_nsa.reshape(n_pool, page, 4 * KV_GROUPS * HEAD_DIM), win_flat, *cmp_params)
    new_win = jnp.concatenate([win_flat, win_new], axis=1)[:, T:].reshape(B, -1, 2, KV_GROUPS, HEAD_DIM)
    return (o_a, o_b, sb_new.reshape(B, T, 2, H_SB, HEAD_DIM), nsa_new.reshape(B, T, 4, KV_GROUPS, HEAD_DIM), new_win)


def block_out(x, o_a, o_b, mix_g, w_out, ln1_g, ln1_b, wq_t, keys, uv_tab, ln2_g, ln2_b):
    h = mixer_out_pallas(x.reshape(-1, D_MODEL), o_a.reshape(-1, SB_WIDTH), o_b.reshape(-1, NSA_WIDTH),
                         mix_g, w_out, ln1_g, ln1_b)
    e, g_t = peer_route_pallas(h, wq_t, keys)
    return peer_expert_pallas(e, h, g_t, uv_tab, ln2_g, ln2_b).reshape(x.shape)


def kernel(x_prompt, x_sample, cache_sb_kv, cache_nsa_kv, cache_win_kv, page_table, w_in, cmp_pe, cmp_w1, cmp_w2, mix_norm_g, w_out, ln1_g, ln1_b, peer_w_q, peer_sub_keys, peer_u, peer_v, ln2_g, ln2_b):
    l = 0
    w_in_pad = jnp.pad(w_in[l], ((0, 0), (0, IN_COLS_PAD - IN_COLS))).astype(jnp.bfloat16)
    cmp_params = compress_params(cmp_pe[l], cmp_w1[l], cmp_w2[l])
    o_a, o_b, sb_p, nsa_p, win_p = mixer_prompt(x_prompt, w_in_pad, cmp_params)
    wq_t = peer_w_q[l].T.astype(jnp.bfloat16)
    keys = peer_sub_keys[l].reshape(2 * PEER_HEADS, N_KEYS, D_KEY // 2).astype(jnp.bfloat16)
    uv_tab = pack_expert_slabs(peer_u[l], peer_v[l])
    w_out_bf = w_out[l].astype(jnp.bfloat16)
    h_p = block_out(x_prompt, o_a, o_b, mix_norm_g[l], w_out_bf, ln1_g[l], ln1_b[l],
                    wq_t, keys, uv_tab, ln2_g[l], ln2_b[l])
    o_a, o_b, sb_s, nsa_s, win_s = mixer_sample(x_sample, cache_sb_kv[l], cache_nsa_kv[l], cache_win_kv[l],
                                                page_table, w_in_pad, cmp_params)
    h_s = block_out(x_sample, o_a, o_b, mix_norm_g[l], w_out_bf, ln1_g[l], ln1_b[l],
                    wq_t, keys, uv_tab, ln2_g[l], ln2_b[l])
    return (h_p, h_s, sb_p[None], nsa_p[None], win_p[None], sb_s[None], nsa_s[None], win_s[None])
```

```python
import jax, jax.numpy as jnp
from jax import lax
from jax.experimental import pallas as pl
from jax.experimental.pallas import tpu as pltpu

D_MODEL = 1024
HEAD_DIM = 64
MIX_WIDTH = D_MODEL
SB_WIDTH = MIX_WIDTH // 2
NSA_WIDTH = MIX_WIDTH - SB_WIDTH
H_SB = SB_WIDTH // HEAD_DIM
H_NSA = NSA_WIDTH // HEAD_DIM
KV_GROUPS = 2
HPG = H_NSA // KV_GROUPS
COMP_BLOCK = 32
COMP_STRIDE = 16
COMP_HID = 128
SEL_BLOCK = 64
TOP_N = 8
WINDOW = 512
Q_BLOCK = 128
PEER_HEADS = 8
N_KEYS = 128
N_EXPERTS = N_KEYS * N_KEYS
PEER_TOPK = 16
D_KEY = 256
DEPTH = 1
ALPHA = (2.0 * DEPTH) ** 0.25
LN_EPS = 1e-5
NEG = -1e30
FORCE = 1e4
SCALE = HEAD_DIM ** -0.5

OFF_SBKV = SB_WIDTH
OFF_QB = 3 * SB_WIDTH
OFF_NSAKV = OFF_QB + NSA_WIDTH
OFF_WIN = OFF_NSAKV + 4 * KV_GROUPS * HEAD_DIM
OFF_GATE = OFF_WIN + 2 * KV_GROUPS * HEAD_DIM
IN_COLS = OFF_GATE + 3 * H_NSA

LANE = 128
V7X_VMEM_BYTES = 64 * 1024 * 1024
VMEM_LIMIT_BYTES = V7X_VMEM_BYTES * 3 // 4
STEP_TOKENS = 4
IN_COLS_PAD = -(-IN_COLS // LANE) * LANE
PROJ_ROWS = 512


PROJ_EDGES = (0, OFF_SBKV, OFF_QB, OFF_NSAKV, OFF_WIN, OFF_GATE, IN_COLS_PAD)


def _proj_kernel(x_ref, w_ref, *out_refs):
    z = jnp.dot(x_ref[...].astype(jnp.bfloat16), w_ref[...], preferred_element_type=jnp.float32)
    for o_ref, lo, hi in zip(out_refs, PROJ_EDGES[:-1], PROJ_EDGES[1:]):
        o_ref[...] = z[:, lo:hi]


def project_pallas(x2d, w_in_pad_bf16):
    n = x2d.shape[0]
    rows = min(PROJ_ROWS, n)
    assert all(e % LANE == 0 for e in PROJ_EDGES)
    widths = [hi - lo for lo, hi in zip(PROJ_EDGES[:-1], PROJ_EDGES[1:])]
    return pl.pallas_call(
        _proj_kernel,
        grid=(n // rows,),
        in_specs=[pl.BlockSpec((rows, D_MODEL), lambda i: (i, 0)),
                  pl.BlockSpec((D_MODEL, IN_COLS_PAD), lambda i: (0, 0))],
        out_specs=[pl.BlockSpec((rows, w), lambda i: (i, 0)) for w in widths],
        out_shape=[jax.ShapeDtypeStruct((n, w), jnp.float32) for w in widths],
        compiler_params=pltpu.CompilerParams(
            dimension_semantics=("arbitrary",), vmem_limit_bytes=VMEM_LIMIT_BYTES),
        name="in_proj",
    )(x2d, w_in_pad_bf16)


N_SEL = 32
N_CMP_PAD = 128
SEL_TILE = 512
ALIBI = [[2.0 ** (-8.0 * (g * HPG + r + 1) / H_NSA) for r in range(HPG)] for g in range(KV_GROUPS)]


def _dot_nt(a, b):
    return lax.dot_general(a, b, (((1,), (1,)), ((), ())), preferred_element_type=jnp.float32)


def _split_dot(a, b_exact):
    hi = a.astype(jnp.bfloat16)
    r1 = a - hi.astype(jnp.float32)
    mid = r1.astype(jnp.bfloat16)
    lo = (r1 - mid.astype(jnp.float32)).astype(jnp.bfloat16)
    d = lambda x: jnp.dot(x, b_exact, preferred_element_type=jnp.float32)
    return d(hi) + d(mid) + d(lo)


def _nsa_prompt_kernel(q_ref, gz_ref, ck_ref, cv_ref, nsa_ref, win_ref, o_ref, m_ref, l_ref, acc_ref):
    QB = Q_BLOCK
    qb = pl.program_id(1)
    q0 = qb * QB
    bf = jnp.bfloat16
    f32 = jnp.float32
    gate = jax.nn.sigmoid(gz_ref[0])
    qpos = q0 + lax.broadcasted_iota(jnp.int32, (QB, 1), 0)
    qh = [(q_ref[0, :, h * HEAD_DIM:(h + 1) * HEAD_DIM] * SCALE).astype(bf) for h in range(H_NSA)]

    def attend(qs, k, v, biases, valid):
        n = range(len(qs))
        s = [jnp.where(valid, _dot_nt(qs[i], k) + biases[i], NEG) for i in n]
        e = [jnp.where(valid, jnp.exp(s[i] - s[i].max(-1, keepdims=True)), 0.0) for i in n]
        tot = [e[i].sum(-1, keepdims=True) for i in n]
        inv = [1.0 / jnp.where(tot[i] > 0, tot[i], 1.0) for i in n]
        o = [jnp.dot(e[i].astype(bf), v, preferred_element_type=f32) * inv[i] for i in n]
        return o, [e[i] * inv[i] for i in n]

    def emit(h, branch, o, first):
        o = gate[:, 3 * h + branch:3 * h + branch + 1] * o
        sl = (0, slice(None), slice(h * HEAD_DIM, (h + 1) * HEAD_DIM))
        o_ref[sl] = o if first else o_ref[sl] + o

    n_i = lax.broadcasted_iota(jnp.int32, (1, N_CMP_PAD), 1)
    cpos = n_i * COMP_STRIDE + (COMP_BLOCK - 1)
    n_cmp = (pl.num_programs(1) * QB - COMP_BLOCK) // COMP_STRIDE + 1
    valid_c = (cpos <= qpos) & (n_i < n_cmp)
    dist_c = (qpos - cpos).astype(f32)
    psum = []
    for g in range(KV_GROUPS):
        o_c, p_c = attend(qh[g * HPG:(g + 1) * HPG], ck_ref[0, g].astype(bf), cv_ref[0, g].astype(bf),
                          [-ALIBI[g][r] * dist_c for r in range(HPG)], valid_c)
        tot = p_c[0]
        for r in range(HPG):
            emit(g * HPG + r, 0, o_c[r], True)
            if r:
                tot = tot + p_c[r]
        psum.append(tot)

    nn = lax.broadcasted_iota(jnp.int32, (N_CMP_PAD, N_SEL), 0)
    jj = lax.broadcasted_iota(jnp.int32, (N_CMP_PAD, N_SEL), 1)
    sel_map = ((nn * COMP_STRIDE) // SEL_BLOCK == jj).astype(bf)
    blk = lax.broadcasted_iota(jnp.int32, (QB, N_SEL), 1)
    forced = (blk == qpos // SEL_BLOCK) | (blk == 0)
    imp = [jnp.where(forced, FORCE, jnp.where(blk * SEL_BLOCK <= qpos, _split_dot(psum[g], sel_map), -1.0))
           for g in range(KV_GROUPS)]
    picked = [jnp.zeros((QB, N_SEL), f32) for g in range(KV_GROUPS)]
    for _ in range(TOP_N):
        for g in range(KV_GROUPS):
            mx = imp[g].max(-1, keepdims=True)
            first_max = jnp.where(imp[g] == mx, blk, N_SEL).min(-1, keepdims=True)
            pick = blk == first_max
            picked[g] = jnp.where(pick, 1.0, picked[g])
            imp[g] = jnp.where(pick, -2.0, imp[g])
    chosen = [picked[g].astype(bf) for g in range(KV_GROUPS)]

    k0 = pl.multiple_of(jnp.maximum(q0 - WINDOW, 0), QB)
    dist = qpos - (lax.broadcasted_iota(jnp.int32, (1, WINDOW + QB), 1) + k0)
    valid_w = (dist >= 0) & (dist < WINDOW)
    dist_w = dist.astype(f32)
    for g in range(KV_GROUPS):
        k = win_ref[0, pl.ds(k0, WINDOW + QB), g * HEAD_DIM:(g + 1) * HEAD_DIM].astype(bf)
        v = win_ref[0, pl.ds(k0, WINDOW + QB), (2 + g) * HEAD_DIM:(3 + g) * HEAD_DIM].astype(bf)
        o_w, _ = attend(qh[g * HPG:(g + 1) * HPG], k, v, [-ALIBI[g][r] * dist_w for r in range(HPG)], valid_w)
        for r in range(HPG):
            emit(g * HPG + r, 2, o_w[r], False)

    m_ref[...] = jnp.full(m_ref.shape, NEG, f32)
    l_ref[...] = jnp.zeros(l_ref.shape, f32)
    acc_ref[...] = jnp.zeros(acc_ref.shape, f32)

    def sel_body(kt, carry):
        k0 = pl.multiple_of(kt * SEL_TILE, SEL_TILE)
        col = lax.broadcasted_iota(jnp.int32, (1, SEL_TILE), 1) + k0
        dist = qpos - col
        distf = dist.astype(f32)
        ej = lax.broadcasted_iota(jnp.int32, (N_SEL, SEL_TILE), 0)
        ec = lax.broadcasted_iota(jnp.int32, (N_SEL, SEL_TILE), 1) + k0
        expand = (ec // SEL_BLOCK == ej).astype(bf)
        heads = range(H_NSA)
        k = [nsa_ref[0, pl.ds(k0, SEL_TILE), (4 + g) * HEAD_DIM:(5 + g) * HEAD_DIM].astype(bf)
             for g in range(KV_GROUPS)]
        v = [nsa_ref[0, pl.ds(k0, SEL_TILE), (6 + g) * HEAD_DIM:(7 + g) * HEAD_DIM].astype(bf)
             for g in range(KV_GROUPS)]
        valid = [(jnp.dot(chosen[g], expand, preferred_element_type=f32) > 0.5) & (dist >= 0)
                 for g in range(KV_GROUPS)]
        s = [jnp.where(valid[h // HPG], _dot_nt(qh[h], k[h // HPG]) - ALIBI[h // HPG][h % HPG] * distf, NEG)
             for h in heads]
        m_old = [m_ref[h] for h in heads]
        m_new = [jnp.maximum(m_old[h], s[h].max(-1, keepdims=True)) for h in heads]
        p = [jnp.where(valid[h // HPG], jnp.exp(s[h] - m_new[h]), 0.0) for h in heads]
        for h in heads:
            alpha = jnp.exp(m_old[h] - m_new[h])
            l_ref[h] = alpha * l_ref[h] + p[h].sum(-1, keepdims=True)
            acc_ref[h] = alpha * acc_ref[h] + jnp.dot(p[h].astype(bf), v[h // HPG], preferred_element_type=f32)
            m_ref[h] = m_new[h]
        return carry

    lax.fori_loop(0, (q0 + QB + SEL_TILE - 1) // SEL_TILE, sel_body, 0)
    for h in range(H_NSA):
        tot = l_ref[h]
        emit(h, 1, acc_ref[h] / jnp.where(tot > 0, tot, 1.0), False)


def nsa_prompt_pallas(q_b, gate_z, ck, cv, nsa_kv, win_kv):
    B, T, _ = q_b.shape
    assert T % SEL_TILE == 0 and T >= WINDOW + Q_BLOCK and T // SEL_BLOCK <= N_SEL
    assert (T - COMP_BLOCK) // COMP_STRIDE + 1 <= N_CMP_PAD
    per_b = lambda b, i: (b, 0, 0)
    return pl.pallas_call(
        _nsa_prompt_kernel,
        grid=(B, T // Q_BLOCK),
        in_specs=[pl.BlockSpec((1, Q_BLOCK, NSA_WIDTH), lambda b, i: (b, i, 0)),
                  pl.BlockSpec((1, Q_BLOCK, LANE), lambda b, i: (b, i, 0)),
                  pl.BlockSpec((1, KV_GROUPS, N_CMP_PAD, HEAD_DIM), lambda b, i: (b, 0, 0, 0)),
                  pl.BlockSpec((1, KV_GROUPS, N_CMP_PAD, HEAD_DIM), lambda b, i: (b, 0, 0, 0)),
                  pl.BlockSpec((1, T, 4 * KV_GROUPS * HEAD_DIM), per_b),
                  pl.BlockSpec((1, T, 2 * KV_GROUPS * HEAD_DIM), per_b)],
        out_specs=pl.BlockSpec((1, Q_BLOCK, NSA_WIDTH), lambda b, i: (b, i, 0)),
        out_shape=jax.ShapeDtypeStruct((B, T, NSA_WIDTH), jnp.float32),
        scratch_shapes=[pltpu.VMEM((H_NSA, Q_BLOCK, 1), jnp.float32),
                        pltpu.VMEM((H_NSA, Q_BLOCK, 1), jnp.float32),
                        pltpu.VMEM((H_NSA, Q_BLOCK, HEAD_DIM), jnp.float32)],
        compiler_params=pltpu.CompilerParams(
            dimension_semantics=("arbitrary", "arbitrary"), vmem_limit_bytes=VMEM_LIMIT_BYTES),
        name="nsa_prompt",
    )(q_b, gate_z, ck, cv, nsa_kv, win_kv)


SB_TILE = 128
N_PICK = PEER_HEADS * PEER_TOPK


def _split2_dot(a, b_exact):
    hi = a.astype(jnp.bfloat16)
    lo = (a - hi.astype(jnp.float32)).astype(jnp.bfloat16)
    return (jnp.dot(hi, b_exact, preferred_element_type=jnp.float32)
            + jnp.dot(lo, b_exact, preferred_element_type=jnp.float32))


def _sb_prompt_kernel(q_ref, kv_ref, o_ref, carry_ref, acc_ref):
    TQ = TK = SB_TILE
    bf, f32 = jnp.bfloat16, jnp.float32
    qb = pl.program_id(1)
    qpos = qb * TQ + lax.broadcasted_iota(jnp.int32, (TQ, 1), 0)
    later = (lax.broadcasted_iota(jnp.int32, (TK, TK), 0) > lax.broadcasted_iota(jnp.int32, (TK, TK), 1)).astype(bf)
    carry_ref[...] = jnp.zeros(carry_ref.shape, f32)
    acc_ref[...] = jnp.zeros(acc_ref.shape, f32)

    def tile_step(k0, valid):
        keep = (lambda x: x) if valid is None else (lambda x: jnp.where(valid, x, 0.0))
        heads = range(H_SB)
        z = [_dot_nt((q_ref[0, :, h * HEAD_DIM:(h + 1) * HEAD_DIM] * SCALE).astype(bf),
                     kv_ref[0, pl.ds(k0, TK), h * HEAD_DIM:(h + 1) * HEAD_DIM].astype(bf)) for h in heads]
        log_sig = [jnp.minimum(z[h], 0.0) - jnp.log(1.0 + jnp.exp(-jnp.abs(z[h]))) for h in heads]
        log_keep = [keep(log_sig[h] - z[h]) for h in heads]
        after = [_split2_dot(log_keep[h], later) for h in heads]
        w = [keep(jnp.exp(log_sig[h] + after[h] + carry_ref[h])).astype(bf) for h in heads]
        for h in heads:
            v = kv_ref[0, pl.ds(k0, TK), SB_WIDTH + h * HEAD_DIM:SB_WIDTH + (h + 1) * HEAD_DIM].astype(bf)
            acc_ref[h] = acc_ref[h] + jnp.dot(w[h], v, preferred_element_type=f32)
            carry_ref[h] = carry_ref[h] + log_keep[h].sum(-1, keepdims=True)

    q0 = pl.multiple_of(qb * TK, TK)
    tile_step(q0, (lax.broadcasted_iota(jnp.int32, (1, TK), 1) + q0) < qpos)

    def body(i, c):
        tile_step(pl.multiple_of((qb - i) * TK, TK), None)
        return c

    lax.fori_loop(1, qb + 1, body, 0)
    for h in range(H_SB):
        o_ref[0, :, h * HEAD_DIM:(h + 1) * HEAD_DIM] = acc_ref[h]


def sb_prompt_pallas(q_a, sb_kv):
    B, T, _ = q_a.shape
    assert T % SB_TILE == 0
    return pl.pallas_call(
        _sb_prompt_kernel,
        grid=(B, T // SB_TILE),
        in_specs=[pl.BlockSpec((1, SB_TILE, SB_WIDTH), lambda b, i: (b, i, 0)),
                  pl.BlockSpec((1, T, 2 * SB_WIDTH), lambda b, i: (b, 0, 0))],
        out_specs=pl.BlockSpec((1, SB_TILE, SB_WIDTH), lambda b, i: (b, i, 0)),
        out_shape=jax.ShapeDtypeStruct((B, T, SB_WIDTH), jnp.float32),
        scratch_shapes=[pltpu.VMEM((H_SB, SB_TILE, 1), jnp.float32),
                        pltpu.VMEM((H_SB, SB_TILE, HEAD_DIM), jnp.float32)],
        compiler_params=pltpu.CompilerParams(
            dimension_semantics=("arbitrary", "arbitrary"), vmem_limit_bytes=VMEM_LIMIT_BYTES),
        name="sb_prompt",
    )(q_a, sb_kv)


ROUTE_TOK = 256


def _top_rows(arrays, ids, n_top, outs):
    arrays = list(arrays)
    if ids is None:
        ids = lax.broadcasted_iota(jnp.int32, arrays[0].shape, 0)
    big = jnp.int32(1 << 30)
    for a in range(n_top):
        m = [s.max(0, keepdims=True) for s in arrays]
        ix = [jnp.where(s == mi, ids, big).min(0, keepdims=True) for s, mi in zip(arrays, m)]
        arrays = [jnp.where(ids == i, -jnp.inf, s) for s, i in zip(arrays, ix)]
        for (vals_ref, idx_ref), mi, i in zip(outs, m, ix):
            vals_ref[a:a + 1, :] = mi
            idx_ref[a:a + 1, :] = i


CAND_HEAD = PEER_TOPK
CAND_SIDE = 8
N_CAND_ROWS = CAND_HEAD + CAND_SIDE * (PEER_TOPK - 1)


def _peer_route_kernel(h_ref, wq_t_ref, keys_ref, e_ref, g_ref, s1_ref, i1_ref, s2_ref, i2_ref, ts_ref, ti_ref,
                       et_ref):
    bf, f32 = jnp.bfloat16, jnp.float32
    q_t = _dot_nt(wq_t_ref[...], h_ref[...].astype(bf))
    half = D_KEY // 2
    tn = q_t.shape[1]
    ci = lax.broadcasted_iota(jnp.int32, (N_CAND_ROWS, tn), 0)
    side = jnp.maximum(ci - CAND_HEAD, 0)
    cand_id = jnp.where(ci < CAND_HEAD, ci * PEER_TOPK, (side % CAND_SIDE) * PEER_TOPK + 1 + side // CAND_SIDE)
    for p in range(PEER_HEADS):
        scores = [jnp.dot(keys_ref[2 * p + hf], q_t[(2 * p + hf) * half:(2 * p + hf + 1) * half, :].astype(bf),
                          preferred_element_type=f32) for hf in range(2)]
        _top_rows(scores, None, PEER_TOPK, ((s1_ref, i1_ref), (s2_ref, i2_ref)))
        s1 = s1_ref[...]
        cand = jnp.concatenate([s1 + s2_ref[0:1, :]]
                               + [s1[:CAND_SIDE] + s2_ref[b:b + 1, :] for b in range(1, PEER_TOPK)], axis=0)
        _top_rows([cand], cand_id, PEER_TOPK, ((ts_ref, ti_ref),))
        ti = ti_ref[...]
        a_of, b_of = ti // PEER_TOPK, ti % PEER_TOPK
        k1 = jnp.zeros(ti.shape, jnp.int32)
        k2 = jnp.zeros(ti.shape, jnp.int32)
        for a in range(PEER_TOPK):
            k1 = jnp.where(a_of == a, i1_ref[a:a + 1, :], k1)
            k2 = jnp.where(b_of == a, i2_ref[a:a + 1, :], k2)
        ts = ts_ref[...]
        ex = jnp.exp(ts - ts.max(0, keepdims=True))
        et_ref[p * PEER_TOPK:(p + 1) * PEER_TOPK, :] = k1 * N_KEYS + k2
        g_ref[p * PEER_TOPK:(p + 1) * PEER_TOPK, :] = ex / ex.sum(0, keepdims=True)
    e_ref[...] = et_ref[...].T


def peer_route_pallas(h, wq_t, keys):
    n = h.shape[0]
    assert n % ROUTE_TOK == 0
    tn = ROUTE_TOK
    top = lambda dt: pltpu.VMEM((PEER_TOPK, tn), dt)
    return pl.pallas_call(
        _peer_route_kernel,
        grid=(n // tn,),
        in_specs=[pl.BlockSpec((tn, D_MODEL), lambda i: (i, 0)),
                  pl.BlockSpec((PEER_HEADS * D_KEY, D_MODEL), lambda i: (0, 0)),
                  pl.BlockSpec((2 * PEER_HEADS, N_KEYS, D_KEY // 2), lambda i: (0, 0, 0))],
        out_specs=[pl.BlockSpec((tn, N_PICK), lambda i: (i, 0)),
                   pl.BlockSpec((N_PICK, tn), lambda i: (0, i))],
        out_shape=[jax.ShapeDtypeStruct((n, N_PICK), jnp.int32),
                   jax.ShapeDtypeStruct((N_PICK, n), jnp.float32)],
        scratch_shapes=[top(jnp.float32), top(jnp.int32), top(jnp.float32), top(jnp.int32),
                        top(jnp.float32), top(jnp.int32), pltpu.VMEM((N_PICK, tn), jnp.int32)],
        compiler_params=pltpu.CompilerParams(
            dimension_semantics=("arbitrary",), vmem_limit_bytes=VMEM_LIMIT_BYTES),
        name="peer_route",
    )(h, wq_t, keys)


PEER_TOK_BLOCK = 128
PEER_SLOTS = 4
VEC_ROWS = D_MODEL // LANE
HALF_ROWS = VEC_ROWS // 2
SLAB = 2 * HALF_ROWS


def pack_expert_slabs(u_tab, v_tab):
    def pack(tab):
        bits = lax.bitcast_convert_type(tab.astype(jnp.bfloat16), jnp.uint16).astype(jnp.uint32)
        bits = bits.reshape(N_EXPERTS, 2, HALF_ROWS, LANE)
        return bits[:, 0] | (bits[:, 1] << 16)
    return jnp.concatenate([pack(u_tab), pack(v_tab)], axis=1).reshape(N_EXPERTS * SLAB, LANE)


def _peer_expert_kernel(e_hbm, h_ref, coef_ref, uv_hbm, g_ref, b_ref, y_ref,
                        e_smem, uvbuf, f_ref, sem_e, sem_rows):
    TB = coef_ref.shape[1]
    i = pl.program_id(0)
    ids = pltpu.make_async_copy(e_hbm.at[pl.ds(i * TB, TB), :], e_smem, sem_e)
    ids.start()
    ids.wait()

    def issue(t, slot):
        for k in range(N_PICK):
            row0 = pl.multiple_of(e_smem[t, k] * SLAB, SLAB)
            pltpu.async_copy(uv_hbm.at[pl.ds(row0, SLAB), :], uvbuf.at[slot, pl.ds(k * SLAB, SLAB), :],
                             sem_rows.at[slot], priority=k % 2)

    def wait_rows(slot):
        pltpu.make_async_copy(uv_hbm.at[pl.ds(0, N_PICK * SLAB), :], uvbuf.at[slot], sem_rows.at[slot]).wait()

    coef_t = coef_ref[...]
    tok = lax.broadcasted_iota(jnp.int32, coef_t.shape, 1)
    low = lambda w: lax.bitcast_convert_type(w << 16, jnp.float32)
    high = lambda w: lax.bitcast_convert_type(w & jnp.uint32(0xFFFF0000), jnp.float32)

    def compute(t, slot):
        rows = lambda r: uvbuf[slot, pl.ds(r, N_PICK, stride=SLAB), :]
        r0 = pl.multiple_of(t * VEC_ROWS, VEC_ROWS)
        h = h_ref[pl.ds(r0, VEC_ROWS), :]
        part = None
        for r in range(HALF_ROWS):
            w = rows(r)
            term = low(w) * h[r:r + 1, :] + high(w) * h[HALF_ROWS + r:HALF_ROWS + r + 1, :]
            part = term if part is None else part + term
        a = jnp.sum(part, axis=-1, keepdims=True)
        c = jnp.sum(jnp.where(tok == t, coef_t, 0.0), axis=-1, keepdims=True)
        wgt = c * jax.nn.gelu(a)
        for r in range(HALF_ROWS):
            w = rows(HALF_ROWS + r)
            f_ref[pl.ds(r0 + r, 1), :] = jnp.sum(low(w) * wgt, axis=0, keepdims=True)
            f_ref[pl.ds(r0 + HALF_ROWS + r, 1), :] = jnp.sum(high(w) * wgt, axis=0, keepdims=True)

    ahead = PEER_SLOTS - 1
    for t in range(ahead):
        issue(t, t)

    def body(j, carry):
        for s in range(PEER_SLOTS):
            t = PEER_SLOTS * j + s

            @pl.when(t + ahead < TB)
            def _():
                issue(t + ahead, (s + ahead) % PEER_SLOTS)

            wait_rows(s)
            compute(t, s)
        return carry

    lax.fori_loop(0, TB // PEER_SLOTS, body, 0)
    x = (ALPHA * h_ref[...] + f_ref[...]).reshape(TB, VEC_ROWS, LANE)
    mean = lambda v: v.sum(axis=2, keepdims=True).sum(axis=1, keepdims=True) * (1.0 / D_MODEL)
    mu = mean(x)
    var = mean(jnp.square(x - mu))
    y = (x - mu) * lax.rsqrt(var + LN_EPS) * g_ref[...][None] + b_ref[...][None]
    y_ref[...] = y.reshape(TB * VEC_ROWS, LANE)


def peer_expert_pallas(e, h, coef_t, uv_slabs, ln_g, ln_b):
    n = h.shape[0]
    tb = min(PEER_TOK_BLOCK, n)
    assert n % tb == 0 and tb % PEER_SLOTS == 0
    fixed = lambda i: (0, 0)
    y = pl.pallas_call(
        _peer_expert_kernel,
        grid=(n // tb,),
        in_specs=[pl.BlockSpec(memory_space=pl.ANY),
                  pl.BlockSpec((tb * VEC_ROWS, LANE), lambda i: (i, 0)),
                  pl.BlockSpec((N_PICK, tb), lambda i: (0, i)),
                  pl.BlockSpec(memory_space=pl.ANY),
                  pl.BlockSpec((VEC_ROWS, LANE), fixed),
                  pl.BlockSpec((VEC_ROWS, LANE), fixed)],
        out_specs=pl.BlockSpec((tb * VEC_ROWS, LANE), lambda i: (i, 0)),
        out_shape=jax.ShapeDtypeStruct((n * VEC_ROWS, LANE), jnp.float32),
        scratch_shapes=[pltpu.SMEM((tb, N_PICK), jnp.int32),
                        pltpu.VMEM((PEER_SLOTS, N_PICK * SLAB, LANE), jnp.uint32),
                        pltpu.VMEM((tb * VEC_ROWS, LANE), jnp.float32),
                        pltpu.SemaphoreType.DMA,
                        pltpu.SemaphoreType.DMA((PEER_SLOTS,))],
        compiler_params=pltpu.CompilerParams(dimension_semantics=("arbitrary",)),
        name="peer_experts",
    )(e, h.reshape(n * VEC_ROWS, LANE), coef_t, uv_slabs,
      ln_g.reshape(VEC_ROWS, LANE), ln_b.reshape(VEC_ROWS, LANE))
    return y.reshape(n, D_MODEL)


MID_ROWS = 512


def _mid_kernel(x_ref, oa_ref, ob_ref, mg_ref, w_ref, g_ref, b_ref, h_ref):
    bf, f32 = jnp.bfloat16, jnp.float32

    def normed(o, g):
        return (o * lax.rsqrt(jnp.mean(jnp.square(o), -1, keepdims=True) + LN_EPS) * g).astype(bf)

    m = (jnp.dot(normed(oa_ref[...], mg_ref[:, :SB_WIDTH]), w_ref[:SB_WIDTH, :], preferred_element_type=f32)
         + jnp.dot(normed(ob_ref[...], mg_ref[:, SB_WIDTH:]), w_ref[SB_WIDTH:, :], preferred_element_type=f32))
    x = ALPHA * x_ref[...] + m
    mu = x.mean(-1, keepdims=True)
    var = jnp.square(x - mu).mean(-1, keepdims=True)
    h_ref[...] = (x - mu) * lax.rsqrt(var + LN_EPS) * g_ref[...] + b_ref[...]


def mixer_out_pallas(x, o_a, o_b, mix_g, w_out_bf16, ln_g, ln_b):
    n = x.shape[0]
    rows = min(MID_ROWS, n)
    assert n % rows == 0
    row = lambda i: (i, 0)
    fixed = lambda i: (0, 0)
    return pl.pallas_call(
        _mid_kernel,
        grid=(n // rows,),
        in_specs=[pl.BlockSpec((rows, D_MODEL), row),
                  pl.BlockSpec((rows, SB_WIDTH), row),
                  pl.BlockSpec((rows, NSA_WIDTH), row),
                  pl.BlockSpec((1, MIX_WIDTH), fixed),
                  pl.BlockSpec((MIX_WIDTH, D_MODEL), fixed),
                  pl.BlockSpec((1, D_MODEL), fixed),
                  pl.BlockSpec((1, D_MODEL), fixed)],
        out_specs=pl.BlockSpec((rows, D_MODEL), row),
        out_shape=jax.ShapeDtypeStruct((n, D_MODEL), jnp.float32),
        compiler_params=pltpu.CompilerParams(
            dimension_semantics=("arbitrary",), vmem_limit_bytes=VMEM_LIMIT_BYTES),
        name="mixer_out",
    )(x, o_a, o_b, mix_g.reshape(1, MIX_WIDTH), w_out_bf16, ln_g.reshape(1, D_MODEL), ln_b.reshape(1, D_MODEL))


PAGE_ROWS = 128
ROW_PAD = 8


def _pad_rows(x, rows):
    return jnp.concatenate([x, jnp.zeros((rows - x.shape[0], x.shape[1]), x.dtype)], axis=0)


def _sb_sample_kernel(pt_ref, q_ref, new_ref, *rest):
    n_pages = len(rest) - 1
    pages, o_ref = rest[:n_pages], rest[n_pages]
    bf, f32 = jnp.bfloat16, jnp.float32
    R = STEP_TOKENS * H_SB
    q_t = q_ref[0]
    rowi = lax.broadcasted_iota(jnp.int32, (R, 1), 0)
    later = (lax.broadcasted_iota(jnp.int32, (PAGE_ROWS, PAGE_ROWS), 0)
             > lax.broadcasted_iota(jnp.int32, (PAGE_ROWS, PAGE_ROWS), 1)).astype(bf)

    key_i = lax.broadcasted_iota(jnp.int32, (1, PAGE_ROWS), 1)
    newest = key_i < rowi // H_SB
    tiles = [_pad_rows(new_ref[0], PAGE_ROWS)] + [pages[p][0] for p in reversed(range(n_pages))]
    n = range(len(tiles))
    z = [_dot_nt(q_t, tiles[i][:, :SB_WIDTH].astype(bf)) for i in n]
    log_sig = [jnp.minimum(z[i], 0.0) - jnp.log(1.0 + jnp.exp(-jnp.abs(z[i]))) for i in n]
    log_keep = [log_sig[i] - z[i] for i in n]
    log_keep[0] = jnp.where(newest, log_keep[0], 0.0)
    after = [_split2_dot(log_keep[i], later) for i in n]
    carry = jnp.zeros((R, 1), f32)
    acc = jnp.zeros((R, SB_WIDTH), f32)
    for i in n:
        w = jnp.exp(log_sig[i] + after[i] + carry)
        if i == 0:
            w = jnp.where(newest, w, 0.0)
        acc = acc + jnp.dot(w.astype(bf), tiles[i][:, SB_WIDTH:].astype(bf), preferred_element_type=f32)
        carry = carry + log_keep[i].sum(-1, keepdims=True)
    head_of_col = lax.broadcasted_iota(jnp.int32, (1, SB_WIDTH), 1) // HEAD_DIM
    own = jnp.where(head_of_col == rowi % H_SB, acc, 0.0)
    for t in range(STEP_TOKENS):
        o_ref[0, t:t + 1, :] = own[t * H_SB:(t + 1) * H_SB, :].sum(0, keepdims=True)


def sb_sample_pallas(page_table, q_a, sb_new, cache_sb):
    B, T, _ = q_a.shape
    n_pages = page_table.shape[1]
    assert T == STEP_TOKENS
    eye = jnp.eye(H_SB, dtype=jnp.float32)
    q_t = (q_a.reshape(B, T, 1, H_SB, HEAD_DIM) * eye[None, None, :, :, None] * SCALE)
    q_t = q_t.reshape(B, T * H_SB, SB_WIDTH).astype(jnp.bfloat16)
    new = jnp.pad(sb_new, ((0, 0), (0, ROW_PAD - T), (0, 0)))
    page_spec = lambda p: pl.BlockSpec((1, PAGE_ROWS, 2 * SB_WIDTH), lambda b, pt, p=p: (pt[b, p], 0, 0))
    return pl.pallas_call(
        _sb_sample_kernel,
        grid_spec=pltpu.PrefetchScalarGridSpec(
            num_scalar_prefetch=1, grid=(B,),
            in_specs=[pl.BlockSpec((1, T * H_SB, SB_WIDTH), lambda b, pt: (b, 0, 0)),
                      pl.BlockSpec((1, ROW_PAD, 2 * SB_WIDTH), lambda b, pt: (b, 0, 0))]
                     + [page_spec(p) for p in range(n_pages)],
            out_specs=pl.BlockSpec((1, T, SB_WIDTH), lambda b, pt: (b, 0, 0))),
        out_shape=jax.ShapeDtypeStruct((B, T, SB_WIDTH), jnp.float32),
        compiler_params=pltpu.CompilerParams(
            dimension_semantics=("arbitrary",), vmem_limit_bytes=VMEM_LIMIT_BYTES),
        name="sb_sample",
    )(page_table, q_t, new, *([cache_sb] * n_pages))


def _compress_tokens(chunk_rows, pe_ref, w1_ref, w2_ref):
    bf, f32 = jnp.bfloat16, jnp.float32
    out = []
    for j in range(2):
        h_a = h_b = None
        for l in range(COMP_STRIDE):
            x = chunk_rows(j, l)
            a = jnp.dot((x + pe_ref[j, l:l + 1, :]).astype(bf), w1_ref[j, l], preferred_element_type=f32)
            b = jnp.dot((x + pe_ref[j, COMP_STRIDE + l:COMP_STRIDE + l + 1, :]).astype(bf),
                        w1_ref[j, COMP_STRIDE + l], preferred_element_type=f32)
            h_a = a if h_a is None else h_a + a
            h_b = b if h_b is None else h_b + b
        hdn = jax.nn.gelu(h_a + pltpu.roll(h_b, h_b.shape[0] - 1, 0)).astype(bf)
        out.append([jnp.dot(hdn[:, g * COMP_HID:(g + 1) * COMP_HID], w2_ref[j], preferred_element_type=f32)
                    for g in range(KV_GROUPS)])
    return out


def _compress_prompt_kernel(xk_ref, xv_ref, w1_ref, w2_ref, pe_ref, ck_ref, cv_ref):
    x_refs = (xk_ref, xv_ref)
    chunks = xk_ref.shape[1] // COMP_STRIDE
    cmp_kv = _compress_tokens(lambda j, l: x_refs[j][0, pl.ds(l, chunks, stride=COMP_STRIDE), :],
                              pe_ref, w1_ref, w2_ref)
    for g in range(KV_GROUPS):
        ck_ref[0, g] = cmp_kv[0][g]
        cv_ref[0, g] = cmp_kv[1][g]


def compress_prompt_pallas(nsa_kv, w1, w2, pe):
    B, T, _ = nsa_kv.shape
    chunks = T // COMP_STRIDE
    assert chunks == N_CMP_PAD
    out = jax.ShapeDtypeStruct((B, KV_GROUPS, chunks, HEAD_DIM), jnp.float32)
    return pl.pallas_call(
        _compress_prompt_kernel,
        grid=(B,),
        in_specs=[pl.BlockSpec((1, T, LANE), lambda b: (b, 0, 0)), pl.BlockSpec((1, T, LANE), lambda b: (b, 0, 1)),
                  pl.BlockSpec(w1.shape, lambda b: (0, 0, 0, 0)),
                  pl.BlockSpec(w2.shape, lambda b: (0, 0, 0)),
                  pl.BlockSpec(pe.shape, lambda b: (0, 0, 0))],
        out_specs=[pl.BlockSpec((1, KV_GROUPS, chunks, HEAD_DIM), lambda b: (b, 0, 0, 0))] * 2,
        out_shape=[out, out],
        compiler_params=pltpu.CompilerParams(
            dimension_semantics=("arbitrary",), vmem_limit_bytes=VMEM_LIMIT_BYTES),
        name="compress_prompt",
    )(nsa_kv, nsa_kv, w1, w2, pe)


def compress_params(cmp_pe, cmp_w1, cmp_w2):
    zero = jnp.zeros_like(cmp_w1)
    w1 = jnp.concatenate([jnp.concatenate([cmp_w1, zero], axis=-1),
                          jnp.concatenate([zero, cmp_w1], axis=-1)], axis=-2).astype(jnp.bfloat16)
    return w1, cmp_w2.astype(jnp.bfloat16), jnp.concatenate([cmp_pe, cmp_pe], axis=-1)


def _nsa_sample_kernel(pt_ref, q_ref, gz_ref, new_ref, neww_ref, win_ref, w1_ref, w2_ref, pe_ref, *rest):
    n_pages = len(rest) - 2
    pages, o_ref, x_ref = rest[:n_pages], rest[n_pages], rest[n_pages + 1]
    bf, f32 = jnp.bfloat16, jnp.float32
    past = n_pages * PAGE_ROWS
    n_cmp = (past + STEP_TOKENS - COMP_BLOCK) // COMP_STRIDE + 1
    n_sel = -(-(past + STEP_TOKENS) // SEL_BLOCK)
    R = HPG * ROW_PAD

    for i, pg in enumerate(pages):
        for j in range(2):
            x_ref[j, i * PAGE_ROWS:(i + 1) * PAGE_ROWS, :] = pg[0, :, j * LANE:(j + 1) * LANE]
    cmp_kv = _compress_tokens(lambda j, l: x_ref[j, pl.ds(l, past // COMP_STRIDE, stride=COMP_STRIDE), :],
                              pe_ref, w1_ref, w2_ref)

    rowi = lax.broadcasted_iota(jnp.int32, (R, 1), 0)
    qpos = past + rowi % ROW_PAD
    gate = jax.nn.sigmoid(gz_ref[0])

    def softmax_rows(s, valid):
        s = jnp.where(valid, s, NEG)
        e = jnp.where(valid, jnp.exp(s - s.max(-1, keepdims=True)), 0.0)
        tot = e.sum(-1, keepdims=True)
        return e / jnp.where(tot > 0, tot, 1.0)

    G = range(KV_GROUPS)
    q = [q_ref[0, g] for g in G]
    slope = []
    for g in G:
        sl = jnp.full((R, 1), ALIBI[g][HPG - 1], f32)
        for r in range(HPG - 1):
            sl = jnp.where(rowi // ROW_PAD == r, ALIBI[g][r], sl)
        slope.append(sl)

    n_i = lax.broadcasted_iota(jnp.int32, (1, PAGE_ROWS), 1)
    cpos = n_i * COMP_STRIDE + (COMP_BLOCK - 1)
    valid_c = (cpos <= qpos) & (n_i < n_cmp)
    dist_c = (qpos - cpos).astype(f32)
    p_c = [softmax_rows(_dot_nt(q[g], cmp_kv[0][g].astype(bf)) - slope[g] * dist_c, valid_c) for g in G]
    o_c = [jnp.dot(p_c[g].astype(bf), cmp_kv[1][g].astype(bf), preferred_element_type=f32) for g in G]

    nn = lax.broadcasted_iota(jnp.int32, (PAGE_ROWS, LANE), 0)
    jj = lax.broadcasted_iota(jnp.int32, (PAGE_ROWS, LANE), 1)
    sel_map = ((nn * COMP_STRIDE) // SEL_BLOCK == jj).astype(bf)
    blk = lax.broadcasted_iota(jnp.int32, (ROW_PAD, LANE), 1)
    qp8 = past + lax.broadcasted_iota(jnp.int32, (ROW_PAD, 1), 0)
    forced = (blk == qp8 // SEL_BLOCK) | (blk == 0)
    imp = []
    for g in G:
        psum = p_c[g][0:ROW_PAD]
        for r in range(1, HPG):
            psum = psum + p_c[g][r * ROW_PAD:(r + 1) * ROW_PAD]
        im = jnp.where(forced, FORCE, jnp.where(blk * SEL_BLOCK <= qp8, _split_dot(psum, sel_map), -1.0))
        imp.append(jnp.where(blk < n_sel, im, -3.0))
    picked = [jnp.zeros((ROW_PAD, LANE), f32) for g in G]
    for _ in range(min(TOP_N, n_sel)):
        for g in G:
            mx = imp[g].max(-1, keepdims=True)
            first_max = jnp.where(imp[g] == mx, blk, LANE).min(-1, keepdims=True)
            pick = blk == first_max
            picked[g] = jnp.where(pick, 1.0, picked[g])
            imp[g] = jnp.where(pick, -4.0, imp[g])
    chosen = [jnp.concatenate([picked[g].astype(bf)] * HPG, axis=0) for g in G]

    new_kv = _pad_rows(new_ref[0], PAGE_ROWS)
    kcol = [slice((4 + g) * HEAD_DIM, (5 + g) * HEAD_DIM) for g in G]
    vcol = [slice((6 + g) * HEAD_DIM, (7 + g) * HEAD_DIM) for g in G]
    n_key = past + PAGE_ROWS
    col = lax.broadcasted_iota(jnp.int32, (1, n_key), 1)
    ej = lax.broadcasted_iota(jnp.int32, (LANE, n_key), 0)
    ec = lax.broadcasted_iota(jnp.int32, (LANE, n_key), 1)
    expand = (ec // SEL_BLOCK == ej).astype(bf)
    dist = qpos - col
    dist_s = dist.astype(f32)
    s = [jnp.concatenate([_dot_nt(q[g], pg[0, :, kcol[g]].astype(bf)) for pg in pages]
                         + [_dot_nt(q[g], new_kv[:, kcol[g]].astype(bf))], axis=1) for g in G]
    valid_s = [(jnp.dot(chosen[g], expand, preferred_element_type=f32) > 0.5) & (dist >= 0) for g in G]
    p_s = [softmax_rows(s[g] - slope[g] * dist_s, valid_s[g]).astype(bf) for g in G]
    o_s = [jnp.dot(p_s[g][:, past:], new_kv[:, vcol[g]].astype(bf), preferred_element_type=f32) for g in G]
    for i, pg in enumerate(pages):
        for g in G:
            o_s[g] = o_s[g] + jnp.dot(p_s[g][:, i * PAGE_ROWS:(i + 1) * PAGE_ROWS], pg[0, :, vcol[g]].astype(bf),
                                      preferred_element_type=f32)

    wb = win_ref.shape[1]
    new_w = _pad_rows(neww_ref[0], PAGE_ROWS)
    wk = [slice(g * HEAD_DIM, (g + 1) * HEAD_DIM) for g in G]
    wv = [slice((2 + g) * HEAD_DIM, (3 + g) * HEAD_DIM) for g in G]
    wpos = past - wb + lax.broadcasted_iota(jnp.int32, (1, wb + PAGE_ROWS), 1)
    dist = qpos - wpos
    valid_w = (dist >= 0) & (dist < WINDOW) & (wpos >= 0)
    dist_w = dist.astype(f32)
    s = [jnp.concatenate([_dot_nt(q[g], win_ref[0, :, wk[g]].astype(bf)),
                          _dot_nt(q[g], new_w[:, wk[g]].astype(bf))], axis=1) for g in G]
    p_w = [softmax_rows(s[g] - slope[g] * dist_w, valid_w).astype(bf) for g in G]
    o_w = [jnp.dot(p_w[g][:, :wb], win_ref[0, :, wv[g]].astype(bf), preferred_element_type=f32)
           + jnp.dot(p_w[g][:, wb:], new_w[:, wv[g]].astype(bf), preferred_element_type=f32) for g in G]

    for h in range(H_NSA):
        g, r = divmod(h, HPG)
        rs = slice(r * ROW_PAD, (r + 1) * ROW_PAD)
        o_ref[0, :, h * HEAD_DIM:(h + 1) * HEAD_DIM] = (gate[:, 3 * h:3 * h + 1] * o_c[g][rs]
                                                        + gate[:, 3 * h + 1:3 * h + 2] * o_s[g][rs]
                                                        + gate[:, 3 * h + 2:3 * h + 3] * o_w[g][rs])


def nsa_sample_pallas(page_table, q_b, gate_z, nsa_new, win_new, cache_nsa, cache_win, w1, w2, pe):
    B, T, _ = q_b.shape
    n_pages = page_table.shape[1]
    assert T == STEP_TOKENS and cache_win.shape[1] == min(WINDOW, n_pages * PAGE_ROWS)
    bf = jnp.bfloat16
    pad_t = lambda x: jnp.pad(x, ((0, 0), (0, ROW_PAD - T), (0, 0)))
    q = pad_t(q_b * SCALE).reshape(B, ROW_PAD, KV_GROUPS, HPG, HEAD_DIM).transpose(0, 2, 3, 1, 4)
    q = q.reshape(B, KV_GROUPS, HPG * ROW_PAD, HEAD_DIM).astype(bf)
    per_b = lambda b, pt: (b, 0, 0)
    page_spec = lambda p: pl.BlockSpec((1, PAGE_ROWS, 4 * KV_GROUPS * HEAD_DIM), lambda b, pt, p=p: (pt[b, p], 0, 0))
    out = pl.pallas_call(
        _nsa_sample_kernel,
        grid_spec=pltpu.PrefetchScalarGridSpec(
            num_scalar_prefetch=1, grid=(B,),
            in_specs=[pl.BlockSpec((1, KV_GROUPS, HPG * ROW_PAD, HEAD_DIM), lambda b, pt: (b, 0, 0, 0)),
                      pl.BlockSpec((1, ROW_PAD, LANE), per_b),
                      pl.BlockSpec((1, ROW_PAD, 4 * KV_GROUPS * HEAD_DIM), per_b),
                      pl.BlockSpec((1, ROW_PAD, 2 * KV_GROUPS * HEAD_DIM), per_b),
                      pl.BlockSpec((1, cache_win.shape[1], 2 * KV_GROUPS * HEAD_DIM), per_b),
                      pl.BlockSpec(w1.shape, lambda b, pt: (0, 0, 0, 0)),
                      pl.BlockSpec(w2.shape, lambda b, pt: (0, 0, 0)),
                      pl.BlockSpec(pe.shape, lambda b, pt: (0, 0, 0))]
                     + [page_spec(p) for p in range(n_pages)],
            out_specs=pl.BlockSpec((1, ROW_PAD, NSA_WIDTH), per_b),
            scratch_shapes=[pltpu.VMEM((2, n_pages * PAGE_ROWS, LANE), jnp.float32)]),
        out_shape=jax.ShapeDtypeStruct((B, ROW_PAD, NSA_WIDTH), jnp.float32),
        compiler_params=pltpu.CompilerParams(
            dimension_semantics=("arbitrary",), vmem_limit_bytes=VMEM_LIMIT_BYTES),
        name="nsa_sample",
    )(page_table, q, pad_t(gate_z), pad_t(nsa_new), pad_t(win_new), cache_win, w1, w2, pe,
      *([cache_nsa] * n_pages))
    return out[:, :T]


def mixer_prompt(x, w_in_pad, cmp_params):
    B, T, _ = x.shape
    q_a, sb_kv, q_b, nsa_kv, win_kv, gate_z = [
        z.reshape(B, T, -1) for z in project_pallas(x.reshape(B * T, D_MODEL), w_in_pad)]
    ck, cv = compress_prompt_pallas(nsa_kv, *cmp_params)
    o_b = nsa_prompt_pallas(q_b, gate_z, ck, cv, nsa_kv, win_kv)
    o_a = sb_prompt_pallas(q_a, sb_kv)
    new_win = win_kv[:, T - min(WINDOW, T):].reshape(B, -1, 2, KV_GROUPS, HEAD_DIM)
    return (o_a, o_b, sb_kv.reshape(B, T, 2, H_SB, HEAD_DIM), nsa_kv.reshape(B, T, 4, KV_GROUPS, HEAD_DIM), new_win)


def mixer_sample(x, cache_sb, cache_nsa, cache_win, page_table, w_in_pad, cmp_params):
    B, T, _ = x.shape
    q_a, sb_new, q_b, nsa_new, win_new, gate_z = [
        z.reshape(B, T, -1) for z in project_pallas(x.reshape(B * T, D_MODEL), w_in_pad)]
    n_pool, page = cache_sb.shape[:2]
    o_a = sb_sample_pallas(page_table, q_a, sb_new, cache_sb.reshape(n_pool, page, 2 * SB_WIDTH))
    win_flat = cache_win.reshape(B, cache_win.shape[1], 2 * KV_GROUPS * HEAD_DIM)
    o_b = nsa_sample_pallas(page_table, q_b, gate_z, nsa_new, win_new,
                            cache_nsa.reshape(n_pool, page, 4 * KV_GROUPS * HEAD_DIM), win_flat, *cmp_params)
    new_win = jnp.concatenate([win_flat, win_new], axis=1)[:, T:].reshape(B, -1, 2, KV_GROUPS, HEAD_DIM)
    return (o_a, o_b, sb_new.reshape(B, T, 2, H_SB, HEAD_DIM), nsa_new.reshape(B, T, 4, KV_GROUPS, HEAD_DIM), new_win)


def block_out(x, o_a, o_b, mix_g, w_out, ln1_g, ln1_b, wq_t, keys, uv_tab, ln2_g, ln2_b):
    h = mixer_out_pallas(x.reshape(-1, D_MODEL), o_a.reshape(-1, SB_WIDTH), o_b.reshape(-1, NSA_WIDTH),
                         mix_g, w_out, ln1_g, ln1_b)
    e, g_t = peer_route_pallas(h, wq_t, keys)
    return peer_expert_pallas(e, h, g_t, uv_tab, ln2_g, ln2_b).reshape(x.shape)


def kernel(x_prompt, x_sample, cache_sb_kv, cache_nsa_kv, cache_win_kv, page_table, w_in, cmp_pe, cmp_w1, cmp_w2, mix_norm_g, w_out, ln1_g, ln1_b, peer_w_q, peer_sub_keys, peer_u, peer_v, ln2_g, ln2_b):
    l = 0
    w_in_pad = jnp.pad(w_in[l], ((0, 0), (0, IN_COLS_PAD - IN_COLS))).astype(jnp.bfloat16)
    cmp_params = compress_params(cmp_pe[l], cmp_w1[l], cmp_w2[l])
    o_a, o_b, sb_p, nsa_p, win_p = mixer_prompt(x_prompt, w_in_pad, cmp_params)
    wq_t = peer_w_q[l].T.astype(jnp.bfloat16)
    keys = peer_sub_keys[l].reshape(2 * PEER_HEADS, N_KEYS, D_KEY // 2).astype(jnp.bfloat16)
    uv_tab = pack_expert_slabs(peer_u[l], peer_v[l])
    w_out_bf = w_out[l].astype(jnp.bfloat16)
    h_p = block_out(x_prompt, o_a, o_b, mix_norm_g[l], w_out_bf, ln1_g[l], ln1_b[l],
                    wq_t, keys, uv_tab, ln2_g[l], ln2_b[l])
    o_a, o_b, sb_s, nsa_s, win_s = mixer_sample(x_sample, cache_sb_kv[l], cache_nsa_kv[l], cache_win_kv[l],
                                                page_table, w_in_pad, cmp_params)
    h_s = block_out(x_sample, o_a, o_b, mix_norm_g[l], w_out_bf, ln1_g[l], ln1_b[l],
                    wq_t, keys, uv_tab, ln2_g[l], ln2_b[l])
    return (h_p, h_s, sb_p[None], nsa_p[None], win_p[None], sb_s[None], nsa_s[None], win_s[None])
```

```python
import jax, jax.numpy as jnp
from jax import lax
from jax.experimental import pallas as pl
from jax.experimental.pallas import tpu as pltpu

D_MODEL = 1024
HEAD_DIM = 64
MIX_WIDTH = D_MODEL
SB_WIDTH = MIX_WIDTH // 2
NSA_WIDTH = MIX_WIDTH - SB_WIDTH
H_SB = SB_WIDTH // HEAD_DIM
H_NSA = NSA_WIDTH // HEAD_DIM
KV_GROUPS = 2
HPG = H_NSA // KV_GROUPS
COMP_BLOCK = 32
COMP_STRIDE = 16
COMP_HID = 128
SEL_BLOCK = 64
TOP_N = 8
WINDOW = 512
Q_BLOCK = 128
PEER_HEADS = 8
N_KEYS = 128
N_EXPERTS = N_KEYS * N_KEYS
PEER_TOPK = 16
D_KEY = 256
DEPTH = 1
ALPHA = (2.0 * DEPTH) ** 0.25
LN_EPS = 1e-5
NEG = -1e30
FORCE = 1e4
SCALE = HEAD_DIM ** -0.5

OFF_SBKV = SB_WIDTH
OFF_QB = 3 * SB_WIDTH
OFF_NSAKV = OFF_QB + NSA_WIDTH
OFF_WIN = OFF_NSAKV + 4 * KV_GROUPS * HEAD_DIM
OFF_GATE = OFF_WIN + 2 * KV_GROUPS * HEAD_DIM
IN_COLS = OFF_GATE + 3 * H_NSA

LANE = 128
V7X_VMEM_BYTES = 64 * 1024 * 1024
VMEM_LIMIT_BYTES = V7X_VMEM_BYTES * 3 // 4
STEP_TOKENS = 4
IN_COLS_PAD = -(-IN_COLS // LANE) * LANE
PROJ_ROWS = 512


PROJ_EDGES = (0, OFF_SBKV, OFF_QB, OFF_NSAKV, OFF_WIN, OFF_GATE, IN_COLS_PAD)


def _proj_kernel(x_ref, w_ref, *out_refs):
    z = jnp.dot(x_ref[...].astype(jnp.bfloat16), w_ref[...], preferred_element_type=jnp.float32)
    for o_ref, lo, hi in zip(out_refs, PROJ_EDGES[:-1], PROJ_EDGES[1:]):
        o_ref[...] = z[:, lo:hi]


def project_pallas(x2d, w_in_pad_bf16):
    n = x2d.shape[0]
    rows = min(PROJ_ROWS, n)
    assert all(e % LANE == 0 for e in PROJ_EDGES)
    widths = [hi - lo for lo, hi in zip(PROJ_EDGES[:-1], PROJ_EDGES[1:])]
    return pl.pallas_call(
        _proj_kernel,
        grid=(n // rows,),
        in_specs=[pl.BlockSpec((rows, D_MODEL), lambda i: (i, 0)),
                  pl.BlockSpec((D_MODEL, IN_COLS_PAD), lambda i: (0, 0))],
        out_specs=[pl.BlockSpec((rows, w), lambda i: (i, 0)) for w in widths],
        out_shape=[jax.ShapeDtypeStruct((n, w), jnp.float32) for w in widths],
        compiler_params=pltpu.CompilerParams(
            dimension_semantics=("arbitrary",), vmem_limit_bytes=VMEM_LIMIT_BYTES),
        name="in_proj",
    )(x2d, w_in_pad_bf16)


N_SEL = 32
N_CMP_PAD = 128
SEL_TILE = 512
ALIBI = [[2.0 ** (-8.0 * (g * HPG + r + 1) / H_NSA) for r in range(HPG)] for g in range(KV_GROUPS)]


def _dot_nt(a, b):
    return lax.dot_general(a, b, (((1,), (1,)), ((), ())), preferred_element_type=jnp.float32)


def _split_dot(a, b_exact):
    hi = a.astype(jnp.bfloat16)
    r1 = a - hi.astype(jnp.float32)
    mid = r1.astype(jnp.bfloat16)
    lo = (r1 - mid.astype(jnp.float32)).astype(jnp.bfloat16)
    d = lambda x: jnp.dot(x, b_exact, preferred_element_type=jnp.float32)
    return d(hi) + d(mid) + d(lo)


def _nsa_prompt_kernel(q_ref, gz_ref, ck_ref, cv_ref, nsa_ref, win_ref, o_ref, m_ref, l_ref, acc_ref):
    QB = Q_BLOCK
    qb = pl.program_id(1)
    q0 = qb * QB
    bf = jnp.bfloat16
    f32 = jnp.float32
    gate = jax.nn.sigmoid(gz_ref[0])
    qpos = q0 + lax.broadcasted_iota(jnp.int32, (QB, 1), 0)
    qh = [(q_ref[0, :, h * HEAD_DIM:(h + 1) * HEAD_DIM] * SCALE).astype(bf) for h in range(H_NSA)]

    def attend(qs, k, v, biases, valid):
        n = range(len(qs))
        s = [jnp.where(valid, _dot_nt(qs[i], k) + biases[i], NEG) for i in n]
        e = [jnp.where(valid, jnp.exp(s[i] - s[i].max(-1, keepdims=True)), 0.0) for i in n]
        tot = [e[i].sum(-1, keepdims=True) for i in n]
        inv = [1.0 / jnp.where(tot[i] > 0, tot[i], 1.0) for i in n]
        o = [jnp.dot(e[i].astype(bf), v, preferred_element_type=f32) * inv[i] for i in n]
        return o, [e[i] * inv[i] for i in n]

    def emit(h, branch, o, first):
        o = gate[:, 3 * h + branch:3 * h + branch + 1] * o
        sl = (0, slice(None), slice(h * HEAD_DIM, (h + 1) * HEAD_DIM))
        o_ref[sl] = o if first else o_ref[sl] + o

    n_i = lax.broadcasted_iota(jnp.int32, (1, N_CMP_PAD), 1)
    cpos = n_i * COMP_STRIDE + (COMP_BLOCK - 1)
    n_cmp = (pl.num_programs(1) * QB - COMP_BLOCK) // COMP_STRIDE + 1
    valid_c = (cpos <= qpos) & (n_i < n_cmp)
    dist_c = (qpos - cpos).astype(f32)
    psum = []
    for g in range(KV_GROUPS):
        o_c, p_c = attend(qh[g * HPG:(g + 1) * HPG], ck_ref[0, g].astype(bf), cv_ref[0, g].astype(bf),
                          [-ALIBI[g][r] * dist_c for r in range(HPG)], valid_c)
        tot = p_c[0]
        for r in range(HPG):
            emit(g * HPG + r, 0, o_c[r], True)
            if r:
                tot = tot + p_c[r]
        psum.append(tot)

    nn = lax.broadcasted_iota(jnp.int32, (N_CMP_PAD, N_SEL), 0)
    jj = lax.broadcasted_iota(jnp.int32, (N_CMP_PAD, N_SEL), 1)
    sel_map = ((nn * COMP_STRIDE) // SEL_BLOCK == jj).astype(bf)
    blk = lax.broadcasted_iota(jnp.int32, (QB, N_SEL), 1)
    forced = (blk == qpos // SEL_BLOCK) | (blk == 0)
    imp = [jnp.where(forced, FORCE, jnp.where(blk * SEL_BLOCK <= qpos, _split_dot(psum[g], sel_map), -1.0))
           for g in range(KV_GROUPS)]
    picked = [jnp.zeros((QB, N_SEL), f32) for g in range(KV_GROUPS)]
    for _ in range(TOP_N):
        for g in range(KV_GROUPS):
            mx = imp[g].max(-1, keepdims=True)
            first_max = jnp.where(imp[g] == mx, blk, N_SEL).min(-1, keepdims=True)
            pick = blk == first_max
            picked[g] = jnp.where(pick, 1.0, picked[g])
            imp[g] = jnp.where(pick, -2.0, imp[g])
    chosen = [picked[g].astype(bf) for g in range(KV_GROUPS)]

    k0 = pl.multiple_of(jnp.maximum(q0 - WINDOW, 0), QB)
    dist = qpos - (lax.broadcasted_iota(jnp.int32, (1, WINDOW + QB), 1) + k0)
    valid_w = (dist >= 0) & (dist < WINDOW)
    dist_w = dist.astype(f32)
    for g in range(KV_GROUPS):
        k = win_ref[0, pl.ds(k0, WINDOW + QB), g * HEAD_DIM:(g + 1) * HEAD_DIM].astype(bf)
        v = win_ref[0, pl.ds(k0, WINDOW + QB), (2 + g) * HEAD_DIM:(3 + g) * HEAD_DIM].astype(bf)
        o_w, _ = attend(qh[g * HPG:(g + 1) * HPG], k, v, [-ALIBI[g][r] * dist_w for r in range(HPG)], valid_w)
        for r in range(HPG):
            emit(g * HPG + r, 2, o_w[r], False)

    m_ref[...] = jnp.full(m_ref.shape, NEG, f32)
    l_ref[...] = jnp.zeros(l_ref.shape, f32)
    acc_ref[...] = jnp.zeros(acc_ref.shape, f32)

    def sel_body(kt, carry):
        k0 = pl.multiple_of(kt * SEL_TILE, SEL_TILE)
        col = lax.broadcasted_iota(jnp.int32, (1, SEL_TILE), 1) + k0
        dist = qpos - col
        distf = dist.astype(f32)
        ej = lax.broadcasted_iota(jnp.int32, (N_SEL, SEL_TILE), 0)
        ec = lax.broadcasted_iota(jnp.int32, (N_SEL, SEL_TILE), 1) + k0
        expand = (ec // SEL_BLOCK == ej).astype(bf)
        heads = range(H_NSA)
        k = [nsa_ref[0, pl.ds(k0, SEL_TILE), (4 + g) * HEAD_DIM:(5 + g) * HEAD_DIM].astype(bf)
             for g in range(KV_GROUPS)]
        v = [nsa_ref[0, pl.ds(k0, SEL_TILE), (6 + g) * HEAD_DIM:(7 + g) * HEAD_DIM].astype(bf)
             for g in range(KV_GROUPS)]
        valid = [(jnp.dot(chosen[g], expand, preferred_element_type=f32) > 0.5) & (dist >= 0)
                 for g in range(KV_GROUPS)]
        s = [jnp.where(valid[h // HPG], _dot_nt(qh[h], k[h // HPG]) - ALIBI[h // HPG][h % HPG] * distf, NEG)
             for h in heads]
        m_old = [m_ref[h] for h in heads]
        m_new = [jnp.maximum(m_old[h], s[h].max(-1, keepdims=True)) for h in heads]
        p = [jnp.where(valid[h // HPG], jnp.exp(s[h] - m_new[h]), 0.0) for h in heads]
        for h in heads:
            alpha = jnp.exp(m_old[h] - m_new[h])
            l_ref[h] = alpha * l_ref[h] + p[h].sum(-1, keepdims=True)
            acc_ref[h] = alpha * acc_ref[h] + jnp.dot(p[h].astype(bf), v[h // HPG], preferred_element_type=f32)
            m_ref[h] = m_new[h]
        return carry

    lax.fori_loop(0, (q0 + QB + SEL_TILE - 1) // SEL_TILE, sel_body, 0)
    for h in range(H_NSA):
        tot = l_ref[h]
        emit(h, 1, acc_ref[h] / jnp.where(tot > 0, tot, 1.0), False)


def nsa_prompt_pallas(q_b, gate_z, ck, cv, nsa_kv, win_kv):
    B, T, _ = q_b.shape
    assert T % SEL_TILE == 0 and T >= WINDOW + Q_BLOCK and T // SEL_BLOCK <= N_SEL
    assert (T - COMP_BLOCK) // COMP_STRIDE + 1 <= N_CMP_PAD
    per_b = lambda b, i: (b, 0, 0)
    return pl.pallas_call(
        _nsa_prompt_kernel,
        grid=(B, T // Q_BLOCK),
        in_specs=[pl.BlockSpec((1, Q_BLOCK, NSA_WIDTH), lambda b, i: (b, i, 0)),
                  pl.BlockSpec((1, Q_BLOCK, LANE), lambda b, i: (b, i, 0)),
                  pl.BlockSpec((1, KV_GROUPS, N_CMP_PAD, HEAD_DIM), lambda b, i: (b, 0, 0, 0)),
                  pl.BlockSpec((1, KV_GROUPS, N_CMP_PAD, HEAD_DIM), lambda b, i: (b, 0, 0, 0)),
                  pl.BlockSpec((1, T, 4 * KV_GROUPS * HEAD_DIM), per_b),
                  pl.BlockSpec((1, T, 2 * KV_GROUPS * HEAD_DIM), per_b)],
        out_specs=pl.BlockSpec((1, Q_BLOCK, NSA_WIDTH), lambda b, i: (b, i, 0)),
        out_shape=jax.ShapeDtypeStruct((B, T, NSA_WIDTH), jnp.float32),
        scratch_shapes=[pltpu.VMEM((H_NSA, Q_BLOCK, 1), jnp.float32),
                        pltpu.VMEM((H_NSA, Q_BLOCK, 1), jnp.float32),
                        pltpu.VMEM((H_NSA, Q_BLOCK, HEAD_DIM), jnp.float32)],
        compiler_params=pltpu.CompilerParams(
            dimension_semantics=("arbitrary", "arbitrary"), vmem_limit_bytes=VMEM_LIMIT_BYTES),
        name="nsa_prompt",
    )(q_b, gate_z, ck, cv, nsa_kv, win_kv)


SB_TILE = 128
N_PICK = PEER_HEADS * PEER_TOPK


def _split2_dot(a, b_exact):
    hi = a.astype(jnp.bfloat16)
    lo = (a - hi.astype(jnp.float32)).astype(jnp.bfloat16)
    return (jnp.dot(hi, b_exact, preferred_element_type=jnp.float32)
            + jnp.dot(lo, b_exact, preferred_element_type=jnp.float32))


def _sb_prompt_kernel(q_ref, kv_ref, o_ref, carry_ref, acc_ref):
    TQ = TK = SB_TILE
    bf, f32 = jnp.bfloat16, jnp.float32
    qb = pl.program_id(1)
    qpos = qb * TQ + lax.broadcasted_iota(jnp.int32, (TQ, 1), 0)
    later = (lax.broadcasted_iota(jnp.int32, (TK, TK), 0) > lax.broadcasted_iota(jnp.int32, (TK, TK), 1)).astype(bf)
    carry_ref[...] = jnp.zeros(carry_ref.shape, f32)
    acc_ref[...] = jnp.zeros(acc_ref.shape, f32)

    def tile_step(k0, valid):
        keep = (lambda x: x) if valid is None else (lambda x: jnp.where(valid, x, 0.0))
        heads = range(H_SB)
        z = [_dot_nt((q_ref[0, :, h * HEAD_DIM:(h + 1) * HEAD_DIM] * SCALE).astype(bf),
                     kv_ref[0, pl.ds(k0, TK), h * HEAD_DIM:(h + 1) * HEAD_DIM].astype(bf)) for h in heads]
        log_sig = [jnp.minimum(z[h], 0.0) - jnp.log(1.0 + jnp.exp(-jnp.abs(z[h]))) for h in heads]
        log_keep = [keep(log_sig[h] - z[h]) for h in heads]
        after = [_split2_dot(log_keep[h], later) for h in heads]
        w = [keep(jnp.exp(log_sig[h] + after[h] + carry_ref[h])).astype(bf) for h in heads]
        for h in heads:
            v = kv_ref[0, pl.ds(k0, TK), SB_WIDTH + h * HEAD_DIM:SB_WIDTH + (h + 1) * HEAD_DIM].astype(bf)
            acc_ref[h] = acc_ref[h] + jnp.dot(w[h], v, preferred_element_type=f32)
            carry_ref[h] = carry_ref[h] + log_keep[h].sum(-1, keepdims=True)

    q0 = pl.multiple_of(qb * TK, TK)
    tile_step(q0, (lax.broadcasted_iota(jnp.int32, (1, TK), 1) + q0) < qpos)

    def body(i, c):
        tile_step(pl.multiple_of((qb - i) * TK, TK), None)
        return c

    lax.fori_loop(1, qb + 1, body, 0)
    for h in range(H_SB):
        o_ref[0, :, h * HEAD_DIM:(h + 1) * HEAD_DIM] = acc_ref[h]


def sb_prompt_pallas(q_a, sb_kv):
    B, T, _ = q_a.shape
    assert T % SB_TILE == 0
    return pl.pallas_call(
        _sb_prompt_kernel,
        grid=(B, T // SB_TILE),
        in_specs=[pl.BlockSpec((1, SB_TILE, SB_WIDTH), lambda b, i: (b, i, 0)),
                  pl.BlockSpec((1, T, 2 * SB_WIDTH), lambda b, i: (b, 0, 0))],
        out_specs=pl.BlockSpec((1, SB_TILE, SB_WIDTH), lambda b, i: (b, i, 0)),
        out_shape=jax.ShapeDtypeStruct((B, T, SB_WIDTH), jnp.float32),
        scratch_shapes=[pltpu.VMEM((H_SB, SB_TILE, 1), jnp.float32),
                        pltpu.VMEM((H_SB, SB_TILE, HEAD_DIM), jnp.float32)],
        compiler_params=pltpu.CompilerParams(
            dimension_semantics=("arbitrary", "arbitrary"), vmem_limit_bytes=VMEM_LIMIT_BYTES),
        name="sb_prompt",
    )(q_a, sb_kv)


ROUTE_TOK = 256


def _top_rows(arrays, ids, n_top, outs):
    arrays = list(arrays)
    if ids is None:
        ids = lax.broadcasted_iota(jnp.int32, arrays[0].shape, 0)
    big = jnp.int32(1 << 30)
    for a in range(n_top):
        m = [s.max(0, keepdims=True) for s in arrays]
        ix = [jnp.where(s == mi, ids, big).min(0, keepdims=True) for s, mi in zip(arrays, m)]
        arrays = [jnp.where(ids == i, -jnp.inf, s) for s, i in zip(arrays, ix)]
        for (vals_ref, idx_ref), mi, i in zip(outs, m, ix):
            vals_ref[a:a + 1, :] = mi
            idx_ref[a:a + 1, :] = i


CAND_HEAD = PEER_TOPK
CAND_SIDE = 8
N_CAND_ROWS = CAND_HEAD + CAND_SIDE * (PEER_TOPK - 1)


def _peer_route_kernel(h_ref, wq_t_ref, keys_ref, e_ref, g_ref, s1_ref, i1_ref, s2_ref, i2_ref, ts_ref, ti_ref,
                       et_ref):
    bf, f32 = jnp.bfloat16, jnp.float32
    q_t = _dot_nt(wq_t_ref[...], h_ref[...].astype(bf))
    half = D_KEY // 2
    tn = q_t.shape[1]
    ci = lax.broadcasted_iota(jnp.int32, (N_CAND_ROWS, tn), 0)
    side = jnp.maximum(ci - CAND_HEAD, 0)
    cand_id = jnp.where(ci < CAND_HEAD, ci * PEER_TOPK, (side % CAND_SIDE) * PEER_TOPK + 1 + side // CAND_SIDE)
    for p in range(PEER_HEADS):
        scores = [jnp.dot(keys_ref[2 * p + hf], q_t[(2 * p + hf) * half:(2 * p + hf + 1) * half, :].astype(bf),
                          preferred_element_type=f32) for hf in range(2)]
        _top_rows(scores, None, PEER_TOPK, ((s1_ref, i1_ref), (s2_ref, i2_ref)))
        s1 = s1_ref[...]
        cand = jnp.concatenate([s1 + s2_ref[0:1, :]]
                               + [s1[:CAND_SIDE] + s2_ref[b:b + 1, :] for b in range(1, PEER_TOPK)], axis=0)
        _top_rows([cand], cand_id, PEER_TOPK, ((ts_ref, ti_ref),))
        ti = ti_ref[...]
        a_of, b_of = ti // PEER_TOPK, ti % PEER_TOPK
        k1 = jnp.zeros(ti.shape, jnp.int32)
        k2 = jnp.zeros(ti.shape, jnp.int32)
        for a in range(PEER_TOPK):
            k1 = jnp.where(a_of == a, i1_ref[a:a + 1, :], k1)
            k2 = jnp.where(b_of == a, i2_ref[a:a + 1, :], k2)
        ts = ts_ref[...]
        ex = jnp.exp(ts - ts.max(0, keepdims=True))
        et_ref[p * PEER_TOPK:(p + 1) * PEER_TOPK, :] = k1 * N_KEYS + k2
        g_ref[p * PEER_TOPK:(p + 1) * PEER_TOPK, :] = ex / ex.sum(0, keepdims=True)
    e_ref[...] = et_ref[...].T


def peer_route_pallas(h, wq_t, keys):
    n = h.shape[0]
    assert n % ROUTE_TOK == 0
    tn = ROUTE_TOK
    top = lambda dt: pltpu.VMEM((PEER_TOPK, tn), dt)
    return pl.pallas_call(
        _peer_route_kernel,
        grid=(n // tn,),
        in_specs=[pl.BlockSpec((tn, D_MODEL), lambda i: (i, 0)),
                  pl.BlockSpec((PEER_HEADS * D_KEY, D_MODEL), lambda i: (0, 0)),
                  pl.BlockSpec((2 * PEER_HEADS, N_KEYS, D_KEY // 2), lambda i: (0, 0, 0))],
        out_specs=[pl.BlockSpec((tn, N_PICK), lambda i: (i, 0)),
                   pl.BlockSpec((N_PICK, tn), lambda i: (0, i))],
        out_shape=[jax.ShapeDtypeStruct((n, N_PICK), jnp.int32),
                   jax.ShapeDtypeStruct((N_PICK, n), jnp.float32)],
        scratch_shapes=[top(jnp.float32), top(jnp.int32), top(jnp.float32), top(jnp.int32),
                        top(jnp.float32), top(jnp.int32), pltpu.VMEM((N_PICK, tn), jnp.int32)],
        compiler_params=pltpu.CompilerParams(
            dimension_semantics=("arbitrary",), vmem_limit_bytes=VMEM_LIMIT_BYTES),
        name="peer_route",
    )(h, wq_t, keys)


PEER_TOK_BLOCK = 128
PEER_SLOTS = 4
ID_PAD = 8
VEC_ROWS = D_MODEL // LANE
HALF_ROWS = VEC_ROWS // 2
SLAB = 2 * HALF_ROWS
SLAB_PITCH = SLAB + 4


def pack_expert_slabs(u_tab, v_tab):
    def pack(tab):
        bits = lax.bitcast_convert_type(tab.astype(jnp.bfloat16), jnp.uint16).astype(jnp.uint32)
        bits = bits.reshape(N_EXPERTS, 2, HALF_ROWS, LANE)
        return bits[:, 0] | (bits[:, 1] << 16)
    return jnp.concatenate([pack(u_tab), pack(v_tab)], axis=1).reshape(N_EXPERTS * SLAB, LANE)


def _peer_expert_kernel(e_hbm, h_ref, coef_ref, uv_hbm, g_ref, b_ref, y_ref,
                        e_smem, uvbuf, f_ref, sem_e, sem_rows):
    TB = coef_ref.shape[1]
    i = pl.program_id(0)
    ids = pltpu.make_async_copy(e_hbm.at[pl.ds(i * TB, TB + ID_PAD), :], e_smem, sem_e)
    ids.start()
    ids.wait()

    def issue(t, slot):
        for k in range(N_PICK):
            row0 = pl.multiple_of(e_smem[t, k] * SLAB, SLAB)
            pltpu.async_copy(uv_hbm.at[pl.ds(row0, SLAB), :], uvbuf.at[slot, pl.ds(k * SLAB_PITCH, SLAB), :],
                             sem_rows.at[slot], priority=k % 2)

    def wait_rows(slot):
        pltpu.make_async_copy(uv_hbm.at[pl.ds(0, N_PICK * SLAB), :], uvbuf.at[slot, pl.ds(0, N_PICK * SLAB), :],
                              sem_rows.at[slot]).wait()

    coef_t = coef_ref[...]
    tok = lax.broadcasted_iota(jnp.int32, coef_t.shape, 1)
    low = lambda w: lax.bitcast_convert_type(w << 16, jnp.float32)
    high = lambda w: lax.bitcast_convert_type(w & jnp.uint32(0xFFFF0000), jnp.float32)

    def compute(t, slot):
        rows = lambda r: uvbuf[slot, pl.ds(r, N_PICK, stride=SLAB_PITCH), :]
        r0 = pl.multiple_of(t * VEC_ROWS, VEC_ROWS)
        h = h_ref[pl.ds(r0, VEC_ROWS), :]
        part = None
        for r in range(HALF_ROWS):
            w = rows(r)
            term = low(w) * h[r:r + 1, :] + high(w) * h[HALF_ROWS + r:HALF_ROWS + r + 1, :]
            part = term if part is None else part + term
        a = jnp.sum(part, axis=-1, keepdims=True)
        c = jnp.sum(jnp.where(tok == t, coef_t, 0.0), axis=-1, keepdims=True)
        wgt = c * jax.nn.gelu(a)
        for r in range(HALF_ROWS):
            w = rows(HALF_ROWS + r)
            f_ref[pl.ds(r0 + r, 1), :] = jnp.sum(low(w) * wgt, axis=0, keepdims=True)
            f_ref[pl.ds(r0 + HALF_ROWS + r, 1), :] = jnp.sum(high(w) * wgt, axis=0, keepdims=True)

    ahead = PEER_SLOTS - 1

    @pl.when(i == 0)
    def _():
        for t in range(ahead):
            issue(t, t)

    def body(j, carry):
        for s in range(PEER_SLOTS):
            t = PEER_SLOTS * j + s
            issue(t + ahead, (s + ahead) % PEER_SLOTS)
            wait_rows(s)
            compute(t, s)
        return carry

    lax.fori_loop(0, TB // PEER_SLOTS, body, 0)

    @pl.when(i == pl.num_programs(0) - 1)
    def _():
        for s in range(ahead):
            wait_rows(s)

    x = (ALPHA * h_ref[...] + f_ref[...]).reshape(TB, VEC_ROWS, LANE)
    mean = lambda v: v.sum(axis=2, keepdims=True).sum(axis=1, keepdims=True) * (1.0 / D_MODEL)
    mu = mean(x)
    var = mean(jnp.square(x - mu))
    y = (x - mu) * lax.rsqrt(var + LN_EPS) * g_ref[...][None] + b_ref[...][None]
    y_ref[...] = y.reshape(TB * VEC_ROWS, LANE)


def peer_expert_pallas(e, h, coef_t, uv_slabs, ln_g, ln_b):
    n = h.shape[0]
    tb = min(PEER_TOK_BLOCK, n)
    assert n % tb == 0 and tb % PEER_SLOTS == 0 and PEER_SLOTS - 1 <= ID_PAD
    fixed = lambda i: (0, 0)
    y = pl.pallas_call(
        _peer_expert_kernel,
        grid=(n // tb,),
        in_specs=[pl.BlockSpec(memory_space=pl.ANY),
                  pl.BlockSpec((tb * VEC_ROWS, LANE), lambda i: (i, 0)),
                  pl.BlockSpec((N_PICK, tb), lambda i: (0, i)),
                  pl.BlockSpec(memory_space=pl.ANY),
                  pl.BlockSpec((VEC_ROWS, LANE), fixed),
                  pl.BlockSpec((VEC_ROWS, LANE), fixed)],
        out_specs=pl.BlockSpec((tb * VEC_ROWS, LANE), lambda i: (i, 0)),
        out_shape=jax.ShapeDtypeStruct((n * VEC_ROWS, LANE), jnp.float32),
        scratch_shapes=[pltpu.SMEM((tb + ID_PAD, N_PICK), jnp.int32),
                        pltpu.VMEM((PEER_SLOTS, N_PICK * SLAB_PITCH, LANE), jnp.uint32),
                        pltpu.VMEM((tb * VEC_ROWS, LANE), jnp.float32),
                        pltpu.SemaphoreType.DMA,
                        pltpu.SemaphoreType.DMA((PEER_SLOTS,))],
        compiler_params=pltpu.CompilerParams(dimension_semantics=("arbitrary",)),
        name="peer_experts",
    )(jnp.pad(e, ((0, ID_PAD), (0, 0))), h.reshape(n * VEC_ROWS, LANE), coef_t, uv_slabs,
      ln_g.reshape(VEC_ROWS, LANE), ln_b.reshape(VEC_ROWS, LANE))
    return y.reshape(n, D_MODEL)


MID_ROWS = 512


def _mid_kernel(x_ref, oa_ref, ob_ref, mg_ref, w_ref, g_ref, b_ref, h_ref):
    bf, f32 = jnp.bfloat16, jnp.float32

    def normed(o, g):
        return (o * lax.rsqrt(jnp.mean(jnp.square(o), -1, keepdims=True) + LN_EPS) * g).astype(bf)

    m = (jnp.dot(normed(oa_ref[...], mg_ref[:, :SB_WIDTH]), w_ref[:SB_WIDTH, :], preferred_element_type=f32)
         + jnp.dot(normed(ob_ref[...], mg_ref[:, SB_WIDTH:]), w_ref[SB_WIDTH:, :], preferred_element_type=f32))
    x = ALPHA * x_ref[...] + m
    mu = x.mean(-1, keepdims=True)
    var = jnp.square(x - mu).mean(-1, keepdims=True)
    h_ref[...] = (x - mu) * lax.rsqrt(var + LN_EPS) * g_ref[...] + b_ref[...]


def mixer_out_pallas(x, o_a, o_b, mix_g, w_out_bf16, ln_g, ln_b):
    n = x.shape[0]
    rows = min(MID_ROWS, n)
    assert n % rows == 0
    row = lambda i: (i, 0)
    fixed = lambda i: (0, 0)
    return pl.pallas_call(
        _mid_kernel,
        grid=(n // rows,),
        in_specs=[pl.BlockSpec((rows, D_MODEL), row),
                  pl.BlockSpec((rows, SB_WIDTH), row),
                  pl.BlockSpec((rows, NSA_WIDTH), row),
                  pl.BlockSpec((1, MIX_WIDTH), fixed),
                  pl.BlockSpec((MIX_WIDTH, D_MODEL), fixed),
                  pl.BlockSpec((1, D_MODEL), fixed),
                  pl.BlockSpec((1, D_MODEL), fixed)],
        out_specs=pl.BlockSpec((rows, D_MODEL), row),
        out_shape=jax.ShapeDtypeStruct((n, D_MODEL), jnp.float32),
        compiler_params=pltpu.CompilerParams(
            dimension_semantics=("arbitrary",), vmem_limit_bytes=VMEM_LIMIT_BYTES),
        name="mixer_out",
    )(x, o_a, o_b, mix_g.reshape(1, MIX_WIDTH), w_out_bf16, ln_g.reshape(1, D_MODEL), ln_b.reshape(1, D_MODEL))


PAGE_ROWS = 128
ROW_PAD = 8


def _pad_rows(x, rows):
    return jnp.concatenate([x, jnp.zeros((rows - x.shape[0], x.shape[1]), x.dtype)], axis=0)


def _sb_sample_kernel(pt_ref, q_ref, new_ref, *rest):
    n_pages = len(rest) - 1
    pages, o_ref = rest[:n_pages], rest[n_pages]
    bf, f32 = jnp.bfloat16, jnp.float32
    R = STEP_TOKENS * H_SB
    q_t = q_ref[0]
    rowi = lax.broadcasted_iota(jnp.int32, (R, 1), 0)
    later = (lax.broadcasted_iota(jnp.int32, (PAGE_ROWS, PAGE_ROWS), 0)
             > lax.broadcasted_iota(jnp.int32, (PAGE_ROWS, PAGE_ROWS), 1)).astype(bf)

    key_i = lax.broadcasted_iota(jnp.int32, (1, PAGE_ROWS), 1)
    newest = key_i < rowi // H_SB
    tiles = [_pad_rows(new_ref[0], PAGE_ROWS)] + [pages[p][0] for p in reversed(range(n_pages))]
    n = range(len(tiles))
    z = [_dot_nt(q_t, tiles[i][:, :SB_WIDTH].astype(bf)) for i in n]
    log_sig = [jnp.minimum(z[i], 0.0) - jnp.log(1.0 + jnp.exp(-jnp.abs(z[i]))) for i in n]
    log_keep = [log_sig[i] - z[i] for i in n]
    log_keep[0] = jnp.where(newest, log_keep[0], 0.0)
    after = [_split2_dot(log_keep[i], later) for i in n]
    carry = jnp.zeros((R, 1), f32)
    acc = jnp.zeros((R, SB_WIDTH), f32)
    for i in n:
        w = jnp.exp(log_sig[i] + after[i] + carry)
        if i == 0:
            w = jnp.where(newest, w, 0.0)
        acc = acc + jnp.dot(w.astype(bf), tiles[i][:, SB_WIDTH:].astype(bf), preferred_element_type=f32)
        carry = carry + log_keep[i].sum(-1, keepdims=True)
    head_of_col = lax.broadcasted_iota(jnp.int32, (1, SB_WIDTH), 1) // HEAD_DIM
    own = jnp.where(head_of_col == rowi % H_SB, acc, 0.0)
    for t in range(STEP_TOKENS):
        o_ref[0, t:t + 1, :] = own[t * H_SB:(t + 1) * H_SB, :].sum(0, keepdims=True)


def sb_sample_pallas(page_table, q_a, sb_new, cache_sb):
    B, T, _ = q_a.shape
    n_pages = page_table.shape[1]
    assert T == STEP_TOKENS
    eye = jnp.eye(H_SB, dtype=jnp.float32)
    q_t = (q_a.reshape(B, T, 1, H_SB, HEAD_DIM) * eye[None, None, :, :, None] * SCALE)
    q_t = q_t.reshape(B, T * H_SB, SB_WIDTH).astype(jnp.bfloat16)
    new = jnp.pad(sb_new, ((0, 0), (0, ROW_PAD - T), (0, 0)))
    page_spec = lambda p: pl.BlockSpec((1, PAGE_ROWS, 2 * SB_WIDTH), lambda b, pt, p=p: (pt[b, p], 0, 0))
    return pl.pallas_call(
        _sb_sample_kernel,
        grid_spec=pltpu.PrefetchScalarGridSpec(
            num_scalar_prefetch=1, grid=(B,),
            in_specs=[pl.BlockSpec((1, T * H_SB, SB_WIDTH), lambda b, pt: (b, 0, 0)),
                      pl.BlockSpec((1, ROW_PAD, 2 * SB_WIDTH), lambda b, pt: (b, 0, 0))]
                     + [page_spec(p) for p in range(n_pages)],
            out_specs=pl.BlockSpec((1, T, SB_WIDTH), lambda b, pt: (b, 0, 0))),
        out_shape=jax.ShapeDtypeStruct((B, T, SB_WIDTH), jnp.float32),
        compiler_params=pltpu.CompilerParams(
            dimension_semantics=("arbitrary",), vmem_limit_bytes=VMEM_LIMIT_BYTES),
        name="sb_sample",
    )(page_table, q_t, new, *([cache_sb] * n_pages))


def _compress_tokens(chunk_rows, pe_ref, w1_ref, w2_ref):
    bf, f32 = jnp.bfloat16, jnp.float32
    out = []
    for j in range(2):
        h_a = h_b = None
        for l in range(COMP_STRIDE):
            x = chunk_rows(j, l)
            a = jnp.dot((x + pe_ref[j, l:l + 1, :]).astype(bf), w1_ref[j, l], preferred_element_type=f32)
            b = jnp.dot((x + pe_ref[j, COMP_STRIDE + l:COMP_STRIDE + l + 1, :]).astype(bf),
                        w1_ref[j, COMP_STRIDE + l], preferred_element_type=f32)
            h_a = a if h_a is None else h_a + a
            h_b = b if h_b is None else h_b + b
        hdn = jax.nn.gelu(h_a + pltpu.roll(h_b, h_b.shape[0] - 1, 0)).astype(bf)
        out.append([jnp.dot(hdn[:, g * COMP_HID:(g + 1) * COMP_HID], w2_ref[j], preferred_element_type=f32)
                    for g in range(KV_GROUPS)])
    return out


def _compress_prompt_kernel(xk_ref, xv_ref, w1_ref, w2_ref, pe_ref, ck_ref, cv_ref):
    x_refs = (xk_ref, xv_ref)
    chunks = xk_ref.shape[1] // COMP_STRIDE
    cmp_kv = _compress_tokens(lambda j, l: x_refs[j][0, pl.ds(l, chunks, stride=COMP_STRIDE), :],
                              pe_ref, w1_ref, w2_ref)
    for g in range(KV_GROUPS):
        ck_ref[0, g] = cmp_kv[0][g]
        cv_ref[0, g] = cmp_kv[1][g]


def compress_prompt_pallas(nsa_kv, w1, w2, pe):
    B, T, _ = nsa_kv.shape
    chunks = T // COMP_STRIDE
    assert chunks == N_CMP_PAD
    out = jax.ShapeDtypeStruct((B, KV_GROUPS, chunks, HEAD_DIM), jnp.float32)
    return pl.pallas_call(
        _compress_prompt_kernel,
        grid=(B,),
        in_specs=[pl.BlockSpec((1, T, LANE), lambda b: (b, 0, 0)), pl.BlockSpec((1, T, LANE), lambda b: (b, 0, 1)),
                  pl.BlockSpec(w1.shape, lambda b: (0, 0, 0, 0)),
                  pl.BlockSpec(w2.shape, lambda b: (0, 0, 0)),
                  pl.BlockSpec(pe.shape, lambda b: (0, 0, 0))],
        out_specs=[pl.BlockSpec((1, KV_GROUPS, chunks, HEAD_DIM), lambda b: (b, 0, 0, 0))] * 2,
        out_shape=[out, out],
        compiler_params=pltpu.CompilerParams(
            dimension_semantics=("arbitrary",), vmem_limit_bytes=VMEM_LIMIT_BYTES),
        name="compress_prompt",
    )(nsa_kv, nsa_kv, w1, w2, pe)


def compress_params(cmp_pe, cmp_w1, cmp_w2):
    zero = jnp.zeros_like(cmp_w1)
    w1 = jnp.concatenate([jnp.concatenate([cmp_w1, zero], axis=-1),
                          jnp.concatenate([zero, cmp_w1], axis=-1)], axis=-2).astype(jnp.bfloat16)
    return w1, cmp_w2.astype(jnp.bfloat16), jnp.concatenate([cmp_pe, cmp_pe], axis=-1)


def _nsa_sample_kernel(pt_ref, q_ref, gz_ref, new_ref, neww_ref, win_ref, w1_ref, w2_ref, pe_ref, *rest):
    n_pages = len(rest) - 2
    pages, o_ref, x_ref = rest[:n_pages], rest[n_pages], rest[n_pages + 1]
    bf, f32 = jnp.bfloat16, jnp.float32
    past = n_pages * PAGE_ROWS
    n_cmp = (past + STEP_TOKENS - COMP_BLOCK) // COMP_STRIDE + 1
    n_sel = -(-(past + STEP_TOKENS) // SEL_BLOCK)
    R = HPG * ROW_PAD

    for i, pg in enumerate(pages):
        for j in range(2):
            x_ref[j, i * PAGE_ROWS:(i + 1) * PAGE_ROWS, :] = pg[0, :, j * LANE:(j + 1) * LANE]
    cmp_kv = _compress_tokens(lambda j, l: x_ref[j, pl.ds(l, past // COMP_STRIDE, stride=COMP_STRIDE), :],
                              pe_ref, w1_ref, w2_ref)

    rowi = lax.broadcasted_iota(jnp.int32, (R, 1), 0)
    qpos = past + rowi % ROW_PAD
    gate = jax.nn.sigmoid(gz_ref[0])

    def softmax_rows(s, valid):
        s = jnp.where(valid, s, NEG)
        e = jnp.where(valid, jnp.exp(s - s.max(-1, keepdims=True)), 0.0)
        tot = e.sum(-1, keepdims=True)
        return e / jnp.where(tot > 0, tot, 1.0)

    G = range(KV_GROUPS)
    q = [q_ref[0, g] for g in G]
    slope = []
    for g in G:
        sl = jnp.full((R, 1), ALIBI[g][HPG - 1], f32)
        for r in range(HPG - 1):
            sl = jnp.where(rowi // ROW_PAD == r, ALIBI[g][r], sl)
        slope.append(sl)

    n_i = lax.broadcasted_iota(jnp.int32, (1, PAGE_ROWS), 1)
    cpos = n_i * COMP_STRIDE + (COMP_BLOCK - 1)
    valid_c = (cpos <= qpos) & (n_i < n_cmp)
    dist_c = (qpos - cpos).astype(f32)
    p_c = [softmax_rows(_dot_nt(q[g], cmp_kv[0][g].astype(bf)) - slope[g] * dist_c, valid_c) for g in G]
    o_c = [jnp.dot(p_c[g].astype(bf), cmp_kv[1][g].astype(bf), preferred_element_type=f32) for g in G]

    nn = lax.broadcasted_iota(jnp.int32, (PAGE_ROWS, LANE), 0)
    jj = lax.broadcasted_iota(jnp.int32, (PAGE_ROWS, LANE), 1)
    sel_map = ((nn * COMP_STRIDE) // SEL_BLOCK == jj).astype(bf)
    blk = lax.broadcasted_iota(jnp.int32, (ROW_PAD, LANE), 1)
    qp8 = past + lax.broadcasted_iota(jnp.int32, (ROW_PAD, 1), 0)
    forced = (blk == qp8 // SEL_BLOCK) | (blk == 0)
    imp = []
    for g in G:
        psum = p_c[g][0:ROW_PAD]
        for r in range(1, HPG):
            psum = psum + p_c[g][r * ROW_PAD:(r + 1) * ROW_PAD]
        im = jnp.where(forced, FORCE, jnp.where(blk * SEL_BLOCK <= qp8, _split_dot(psum, sel_map), -1.0))
        imp.append(jnp.where(blk < n_sel, im, -3.0))
    picked = [jnp.zeros((ROW_PAD, LANE), f32) for g in G]
    for _ in range(min(TOP_N, n_sel)):
        for g in G:
            mx = imp[g].max(-1, keepdims=True)
            first_max = jnp.where(imp[g] == mx, blk, LANE).min(-1, keepdims=True)
            pick = blk == first_max
            picked[g] = jnp.where(pick, 1.0, picked[g])
            imp[g] = jnp.where(pick, -4.0, imp[g])
    chosen = [jnp.concatenate([picked[g].astype(bf)] * HPG, axis=0) for g in G]

    new_kv = _pad_rows(new_ref[0], PAGE_ROWS)
    kcol = [slice((4 + g) * HEAD_DIM, (5 + g) * HEAD_DIM) for g in G]
    vcol = [slice((6 + g) * HEAD_DIM, (7 + g) * HEAD_DIM) for g in G]
    n_key = past + PAGE_ROWS
    col = lax.broadcasted_iota(jnp.int32, (1, n_key), 1)
    ej = lax.broadcasted_iota(jnp.int32, (LANE, n_key), 0)
    ec = lax.broadcasted_iota(jnp.int32, (LANE, n_key), 1)
    expand = (ec // SEL_BLOCK == ej).astype(bf)
    dist = qpos - col
    dist_s = dist.astype(f32)
    s = [jnp.concatenate([_dot_nt(q[g], pg[0, :, kcol[g]].astype(bf)) for pg in pages]
                         + [_dot_nt(q[g], new_kv[:, kcol[g]].astype(bf))], axis=1) for g in G]
    valid_s = [(jnp.dot(chosen[g], expand, preferred_element_type=f32) > 0.5) & (dist >= 0) for g in G]
    p_s = [softmax_rows(s[g] - slope[g] * dist_s, valid_s[g]).astype(bf) for g in G]
    o_s = [jnp.dot(p_s[g][:, past:], new_kv[:, vcol[g]].astype(bf), preferred_element_type=f32) for g in G]
    for i, pg in enumerate(pages):
        for g in G:
            o_s[g] = o_s[g] + jnp.dot(p_s[g][:, i * PAGE_ROWS:(i + 1) * PAGE_ROWS], pg[0, :, vcol[g]].astype(bf),
                                      preferred_element_type=f32)

    wb = win_ref.shape[1]
    new_w = _pad_rows(neww_ref[0], PAGE_ROWS)
    wk = [slice(g * HEAD_DIM, (g + 1) * HEAD_DIM) for g in G]
    wv = [slice((2 + g) * HEAD_DIM, (3 + g) * HEAD_DIM) for g in G]
    wpos = past - wb + lax.broadcasted_iota(jnp.int32, (1, wb + PAGE_ROWS), 1)
    dist = qpos - wpos
    valid_w = (dist >= 0) & (dist < WINDOW) & (wpos >= 0)
    dist_w = dist.astype(f32)
    s = [jnp.concatenate([_dot_nt(q[g], win_ref[0, :, wk[g]].astype(bf)),
                          _dot_nt(q[g], new_w[:, wk[g]].astype(bf))], axis=1) for g in G]
    p_w = [softmax_rows(s[g] - slope[g] * dist_w, valid_w).astype(bf) for g in G]
    o_w = [jnp.dot(p_w[g][:, :wb], win_ref[0, :, wv[g]].astype(bf), preferred_element_type=f32)
           + jnp.dot(p_w[g][:, wb:], new_w[:, wv[g]].astype(bf), preferred_element_type=f32) for g in G]

    for h in range(H_NSA):
        g, r = divmod(h, HPG)
        rs = slice(r * ROW_PAD, (r + 1) * ROW_PAD)
        o_ref[0, :, h * HEAD_DIM:(h + 1) * HEAD_DIM] = (gate[:, 3 * h:3 * h + 1] * o_c[g][rs]
                                                        + gate[:, 3 * h + 1:3 * h + 2] * o_s[g][rs]
                                                        + gate[:, 3 * h + 2:3 * h + 3] * o_w[g][rs])


def nsa_sample_pallas(page_table, q_b, gate_z, nsa_new, win_new, cache_nsa, cache_win, w1, w2, pe):
    B, T, _ = q_b.shape
    n_pages = page_table.shape[1]
    assert T == STEP_TOKENS and cache_win.shape[1] == min(WINDOW, n_pages * PAGE_ROWS)
    bf = jnp.bfloat16
    pad_t = lambda x: jnp.pad(x, ((0, 0), (0, ROW_PAD - T), (0, 0)))
    q = pad_t(q_b * SCALE).reshape(B, ROW_PAD, KV_GROUPS, HPG, HEAD_DIM).transpose(0, 2, 3, 1, 4)
    q = q.reshape(B, KV_GROUPS, HPG * ROW_PAD, HEAD_DIM).astype(bf)
    per_b = lambda b, pt: (b, 0, 0)
    page_spec = lambda p: pl.BlockSpec((1, PAGE_ROWS, 4 * KV_GROUPS * HEAD_DIM), lambda b, pt, p=p: (pt[b, p], 0, 0))
    out = pl.pallas_call(
        _nsa_sample_kernel,
        grid_spec=pltpu.PrefetchScalarGridSpec(
            num_scalar_prefetch=1, grid=(B,),
            in_specs=[pl.BlockSpec((1, KV_GROUPS, HPG * ROW_PAD, HEAD_DIM), lambda b, pt: (b, 0, 0, 0)),
                      pl.BlockSpec((1, ROW_PAD, LANE), per_b),
                      pl.BlockSpec((1, ROW_PAD, 4 * KV_GROUPS * HEAD_DIM), per_b),
                      pl.BlockSpec((1, ROW_PAD, 2 * KV_GROUPS * HEAD_DIM), per_b),
                      pl.BlockSpec((1, cache_win.shape[1], 2 * KV_GROUPS * HEAD_DIM), per_b),
                      pl.BlockSpec(w1.shape, lambda b, pt: (0, 0, 0, 0)),
                      pl.BlockSpec(w2.shape, lambda b, pt: (0, 0, 0)),
                      pl.BlockSpec(pe.shape, lambda b, pt: (0, 0, 0))]
                     + [page_spec(p) for p in range(n_pages)],
            out_specs=pl.BlockSpec((1, ROW_PAD, NSA_WIDTH), per_b),
            scratch_shapes=[pltpu.VMEM((2, n_pages * PAGE_ROWS, LANE), jnp.float32)]),
        out_shape=jax.ShapeDtypeStruct((B, ROW_PAD, NSA_WIDTH), jnp.float32),
        compiler_params=pltpu.CompilerParams(
            dimension_semantics=("arbitrary",), vmem_limit_bytes=VMEM_LIMIT_BYTES),
        name="nsa_sample",
    )(page_table, q, pad_t(gate_z), pad_t(nsa_new), pad_t(win_new), cache_win, w1, w2, pe,
      *([cache_nsa] * n_pages))
    return out[:, :T]


def mixer_prompt(x, w_in_pad, cmp_params):
    B, T, _ = x.shape
    q_a, sb_kv, q_b, nsa_kv, win_kv, gate_z = [
        z.reshape(B, T, -1) for z in project_pallas(x.reshape(B * T, D_MODEL), w_in_pad)]
    ck, cv = compress_prompt_pallas(nsa_kv, *cmp_params)
    o_b = nsa_prompt_pallas(q_b, gate_z, ck, cv, nsa_kv, win_kv)
    o_a = sb_prompt_pallas(q_a, sb_kv)
    new_win = win_kv[:, T - min(WINDOW, T):].reshape(B, -1, 2, KV_GROUPS, HEAD_DIM)
    return (o_a, o_b, sb_kv.reshape(B, T, 2, H_SB, HEAD_DIM), nsa_kv.reshape(B, T, 4, KV_GROUPS, HEAD_DIM), new_win)


def mixer_sample(x, cache_sb, cache_nsa, cache_win, page_table, w_in_pad, cmp_params):
    B, T, _ = x.shape
    q_a, sb_new, q_b, nsa_new, win_new, gate_z = [
        z.reshape(B, T, -1) for z in project_pallas(x.reshape(B * T, D_MODEL), w_in_pad)]
    n_pool, page = cache_sb.shape[:2]
    o_a = sb_sample_pallas(page_table, q_a, sb_new, cache_sb.reshape(n_pool, page, 2 * SB_WIDTH))
    win_flat = cache_win.reshape(B, cache_win.shape[1], 2 * KV_GROUPS * HEAD_DIM)
    o_b = nsa_sample_pallas(page_table, q_b, gate_z, nsa_new, win_new,
                            cache_nsa.reshape(n_pool, page, 4 * KV_GROUPS * HEAD_DIM), win_flat, *cmp_params)
    new_win = jnp.concatenate([win_flat, win_new], axis=1)[:, T:].reshape(B, -1, 2, KV_GROUPS, HEAD_DIM)
    return (o_a, o_b, sb_new.reshape(B, T, 2, H_SB, HEAD_DIM), nsa_new.reshape(B, T, 4, KV_GROUPS, HEAD_DIM), new_win)


def block_out(x, o_a, o_b, mix_g, w_out, ln1_g, ln1_b, wq_t, keys, uv_tab, ln2_g, ln2_b):
    h = mixer_out_pallas(x.reshape(-1, D_MODEL), o_a.reshape(-1, SB_WIDTH), o_b.reshape(-1, NSA_WIDTH),
                         mix_g, w_out, ln1_g, ln1_b)
    e, g_t = peer_route_pallas(h, wq_t, keys)
    return peer_expert_pallas(e, h, g_t, uv_tab, ln2_g, ln2_b).reshape(x.shape)


def kernel(x_prompt, x_sample, cache_sb_kv, cache_nsa_kv, cache_win_kv, page_table, w_in, cmp_pe, cmp_w1, cmp_w2, mix_norm_g, w_out, ln1_g, ln1_b, peer_w_q, peer_sub_keys, peer_u, peer_v, ln2_g, ln2_b):
    l = 0
    w_in_pad = jnp.pad(w_in[l], ((0, 0), (0, IN_COLS_PAD - IN_COLS))).astype(jnp.bfloat16)
    cmp_params = compress_params(cmp_pe[l], cmp_w1[l], cmp_w2[l])
    o_a, o_b, sb_p, nsa_p, win_p = mixer_prompt(x_prompt, w_in_pad, cmp_params)
    wq_t = peer_w_q[l].T.astype(jnp.bfloat16)
    keys = peer_sub_keys[l].reshape(2 * PEER_HEADS, N_KEYS, D_KEY // 2).astype(jnp.bfloat16)
    uv_tab = pack_expert_slabs(peer_u[l], peer_v[l])
    w_out_bf = w_out[l].astype(jnp.bfloat16)
    h_p = block_out(x_prompt, o_a, o_b, mix_norm_g[l], w_out_bf, ln1_g[l], ln1_b[l],
                    wq_t, keys, uv_tab, ln2_g[l], ln2_b[l])
    o_a, o_b, sb_s, nsa_s, win_s = mixer_sample(x_sample, cache_sb_kv[l], cache_nsa_kv[l], cache_win_kv[l],
                                                page_table, w_in_pad, cmp_params)
    h_s = block_out(x_sample, o_a, o_b, mix_norm_g[l], w_out_bf, ln1_g[l], ln1_b[l],
                    wq_t, keys, uv_tab, ln2_g[l], ln2_b[l])
    return (h_p, h_s, sb_p[None], nsa_p[None], win_p[None], sb_s[None], nsa_s[None], win_s[None])
```
